```python
import math
import jax, jax.numpy as jnp
from jax import lax
import numpy as np

D_MODEL = 2048
BATCH = 4
SEQ = 4096
DEPTH = 1

GRID_W = 64
CTX_LEN = 256
D_MIX = D_MODEL
GLA_HEADS = 4
GLA_DK = (D_MIX // 4) // GLA_HEADS
GLA_DV = (D_MIX // 2) // GLA_HEADS
GLA_RANK = 16
GLA_TAU = 16.0
GLA_CHUNK = 64
DIFF_HEADS = 8
DIFF_DH = 64
DIFF_DV = 2 * DIFF_DH
ROPE_BASE = 10000.0
Q_BLOCK = 128
N_EXPERTS = 16
EC_CAPACITY_FACTOR = 2
D_FF_EXPERT = D_MODEL
ALPHA = (2.0 * DEPTH) ** 0.25
BETA = (8.0 * DEPTH) ** -0.25
LN_EPS = 1e-5
RMS_EPS = 1e-6

IN_SIZES = (GLA_HEADS * GLA_DK, GLA_HEADS * GLA_DK, GLA_HEADS * GLA_DV, GLA_HEADS * GLA_DV, 2 * GLA_RANK,
            DIFF_HEADS * 2 * DIFF_DH, DIFF_HEADS * 2 * DIFF_DH, DIFF_HEADS * DIFF_DV)
IN_SPLITS = tuple(int(s) for s in np.cumsum(IN_SIZES)[:-1])
D_IN = int(sum(IN_SIZES))

kernel_name = 'hybrid_gla_diffattn_ec_moe_diffusion_block'


def layer_norm(x, g, b):
    xf = x.astype(jnp.float32)
    mu = jnp.mean(xf, axis=-1, keepdims=True)
    var = jnp.mean(jnp.square(xf - mu), axis=-1, keepdims=True)
    return ((xf - mu) * lax.rsqrt(var + LN_EPS) * g + b).astype(x.dtype)


def head_rms_norm(x, g):
    xf = x.astype(jnp.float32)
    return (xf * lax.rsqrt(jnp.mean(xf * xf, axis=-1, keepdims=True) + RMS_EPS) * g).astype(x.dtype)


def axial_rope_tables(rows, dh):
    row = jnp.repeat(jnp.arange(rows, dtype=jnp.float32), GRID_W)
    col = jnp.tile(jnp.arange(GRID_W, dtype=jnp.float32), rows)
    n_freq = dh // 4
    inv = ROPE_BASE ** (-jnp.arange(n_freq, dtype=jnp.float32) / n_freq)
    ang = jnp.concatenate([row[:, None] * inv, col[:, None] * inv], axis=-1)
    return jnp.cos(ang), jnp.sin(ang)


def apply_rope(x, cos, sin):
    half = x.shape[-1] // 2
    xf = x.astype(jnp.float32)
    x1, x2 = xf[..., :half], xf[..., half:]
    cs = cos[None, :, None, None, :]
    sn = sin[None, :, None, None, :]
    return jnp.concatenate([x1 * cs - x2 * sn, x1 * sn + x2 * cs], axis=-1).astype(x.dtype)


def flip(a):
    return a[:, :, ::-1]


def gla_inputs(gq, gk, gv, glr, w_gate2, b_gate):
    B, N, _ = gq.shape
    heads = lambda a: a.reshape(B, N, GLA_HEADS, -1).transpose(0, 2, 1, 3).astype(jnp.float32)
    q = heads(gq) * (GLA_DK ** -0.5)
    k = heads(gk)
    v = heads(gv)
    logg = [jax.nn.log_sigmoid(heads(glr[..., d * GLA_RANK:(d + 1) * GLA_RANK] @ w_gate2[d] + b_gate[d])) / GLA_TAU
            for d in range(2)]
    return q, k, v, logg[0], logg[1]


def gla_chunk_scan(q, k, v, logg, s0):
    B, H, N, dk = q.shape
    dv = v.shape[-1]
    nc = N // GLA_CHUNK
    split = lambda a: a.reshape(B, H, nc, GLA_CHUNK, a.shape[-1]).transpose(2, 0, 1, 3, 4)
    mask = jnp.tril(jnp.ones((GLA_CHUNK, GLA_CHUNK), dtype=bool))

    def step(s, inp):
        qb, kb, vb, gb = inp
        bcum = jnp.cumsum(gb, axis=-2)
        btot = bcum[..., -1:, :]
        q_in = qb * jnp.exp(bcum)
        k_in = kb * jnp.exp(-bcum)
        k_out = kb * jnp.exp(btot - bcum)
        att = jnp.where(mask, jnp.einsum('bhik,bhjk->bhij', q_in, k_in), 0.0)
        o = jnp.einsum('bhij,bhjv->bhiv', att, vb) + jnp.einsum('bhik,bhkv->bhiv', q_in, s)
        s_new = jnp.exp(btot)[..., 0, :, None] * s + jnp.einsum('bhjk,bhjv->bhkv', k_out, vb)
        return s_new, o

    s_fin, oc = lax.scan(step, s0, (split(q), split(k), split(v), split(logg)))
    return oc.transpose(1, 2, 0, 3, 4).reshape(B, H, N, dv), s_fin


def gla_output(o, r, g):
    B, H, N, dv = o.shape
    o = head_rms_norm(o, g).transpose(0, 2, 1, 3).reshape(B, N, H * dv)
    return o.astype(r.dtype) * jax.nn.silu(r)


def diff_heads(dq, dk, dv):
    B, N, _ = dq.shape
    return (dq.reshape(B, N, DIFF_HEADS, 2, DIFF_DH), dk.reshape(B, N, DIFF_HEADS, 2, DIFF_DH),
            dv.reshape(B, N, DIFF_HEADS, DIFF_DV))


def diff_attend(q, k, v, lam):
    s = jnp.einsum('bqhmd,bkhmd->bhmqk', q, k).astype(jnp.float32) * (DIFF_DH ** -0.5)
    p = jax.nn.softmax(s, axis=-1)
    a = p[:, :, 0] - lam * p[:, :, 1]
    return jnp.einsum('bhqk,bkhv->bqhv', a.astype(v.dtype), v)


def diff_output(o, g, lam_init):
    B, N, H, dv = o.shape
    return (head_rms_norm(o, g) * (1.0 - lam_init)).reshape(B, N, H * dv)


def hybrid_mixer(u_ctx, u_lat, cos, sin, w_in, w_gate2, b_gate, gla_norm_g, diff_lambda, diff_norm_g, w_o,
                 lam_init, need_ctx_out):
    gq_c, gk_c, gv_c, gr_c, glr_c, dq_c, dk_c, dv_c = jnp.split(u_ctx @ w_in, IN_SPLITS, axis=-1)
    gq_l, gk_l, gv_l, gr_l, glr_l, dq_l, dk_l, dv_l = jnp.split(u_lat @ w_in, IN_SPLITS, axis=-1)
    B, n, _ = u_lat.shape

    qc, kc, vc, gfc, gbc = gla_inputs(gq_c, gk_c, gv_c, glr_c, w_gate2, b_gate)
    ql, kl, vl, gfl, gbl = gla_inputs(gq_l, gk_l, gv_l, glr_l, w_gate2, b_gate)
    s0 = jnp.zeros((B, GLA_HEADS, GLA_DK, GLA_DV), jnp.float32)
    o_cf, s_cf = gla_chunk_scan(qc, kc, vc, gfc, s0)
    o_cb, s_cb = gla_chunk_scan(flip(qc), flip(kc), flip(vc), flip(gbc), s0)
    o_lf, _ = gla_chunk_scan(ql, kl, vl, gfl, s_cf)
    o_lb, _ = gla_chunk_scan(flip(ql), flip(kl), flip(vl), flip(gbl), s_cb)
    gla_lat = gla_output(o_lf + flip(o_lb), gr_l, gla_norm_g)

    lq1, lk1, lq2, lk2 = diff_lambda.astype(jnp.float32)
    lam = jnp.exp(jnp.sum(lq1 * lk1)) - jnp.exp(jnp.sum(lq2 * lk2)) + lam_init
    qc_d, kc_d, vc_d = diff_heads(dq_c, dk_c, dv_c)
    ql_d, kl_d, vl_d = diff_heads(dq_l, dk_l, dv_l)
    ql_d = apply_rope(ql_d, cos, sin)
    kl_d = apply_rope(kl_d, cos, sin)
    k_all = jnp.concatenate([kc_d, kl_d], axis=1)
    v_all = jnp.concatenate([vc_d, vl_d], axis=1)
    nblk = n // Q_BLOCK
    qb = ql_d.reshape(B, nblk, Q_BLOCK, DIFF_HEADS, 2, DIFF_DH).swapaxes(0, 1)
    ob = lax.map(lambda qblk: diff_attend(qblk, k_all, v_all, lam), qb)
    diff_lat = diff_output(ob.swapaxes(0, 1).reshape(B, n, DIFF_HEADS, DIFF_DV), diff_norm_g, lam_init)

    out_lat = jnp.concatenate([gla_lat, diff_lat], axis=-1) @ w_o
    if not need_ctx_out:
        return None, out_lat
    gla_ctx = gla_output(o_cf + flip(o_cb), gr_c, gla_norm_g)
    diff_ctx = diff_output(diff_attend(qc_d, kc_d, vc_d, lam), diff_norm_g, lam_init)
    out_ctx = jnp.concatenate([gla_ctx, diff_ctx], axis=-1) @ w_o
    return out_ctx, out_lat


def expert_choice_ffn(h, w_router, w1, w3, w2):
    B, N, D = h.shape
    cap = EC_CAPACITY_FACTOR * N // N_EXPERTS
    aff = jax.nn.softmax((h @ w_router).astype(jnp.float32), axis=-1)
    gate, idx = lax.top_k(aff.swapaxes(1, 2), cap)
    xs = jax.vmap(lambda hb, ib: hb[ib])(h, idx)
    hid = jax.nn.silu(jnp.einsum('becd,edf->becf', xs, w1)) * jnp.einsum('becd,edf->becf', xs, w3)
    y = jnp.einsum('becf,efd->becd', hid, w2) * gate[..., None].astype(h.dtype)
    flat = (idx + (jnp.arange(B, dtype=idx.dtype) * N)[:, None, None]).reshape(-1)
    out = jax.ops.segment_sum(y.reshape(-1, D), flat, num_segments=B * N)
    return out.reshape(B, N, D)


def setup_inputs(seed: int = 0) -> dict:
    key = jax.random.key(seed)
    ks = jax.random.split(key, 24)
    nrm = lambda k, shape, s: jax.random.normal(k, shape, jnp.float32) * s
    col_scale = np.ones((D_IN,), np.float32)
    col_scale[IN_SPLITS[1]:IN_SPLITS[2]] = BETA
    col_scale[IN_SPLITS[6]:] = BETA
    return {
        'x': nrm(ks[0], (BATCH, SEQ, D_MODEL), 1.0),
        'c': nrm(ks[1], (BATCH, D_MODEL), 1.0),
        'ctx': nrm(ks[2], (BATCH, CTX_LEN, D_MODEL), 1.0),
        'c_ctx': nrm(ks[3], (D_MODEL,), 1.0),
        'w_ada': nrm(ks[4], (DEPTH, D_MODEL, 6 * D_MODEL), 0.5 * D_MODEL ** -0.5),
        'b_ada': nrm(ks[5], (DEPTH, 6 * D_MODEL), 0.01),
        'w_in': nrm(ks[6], (DEPTH, D_MODEL, D_IN), D_MODEL ** -0.5) * jnp.asarray(col_scale),
        'w_gate2': nrm(ks[7], (DEPTH, 2, GLA_RANK, GLA_HEADS * GLA_DK), GLA_RANK ** -0.5),
        'b_gate': nrm(ks[8], (DEPTH, 2, GLA_HEADS * GLA_DK), 0.1),
        'gla_norm_g': 1.0 + nrm(ks[9], (DEPTH, GLA_DV), 0.02),
        'diff_lambda': nrm(ks[10], (DEPTH, 4, DIFF_DH), 0.1),
        'diff_norm_g': 1.0 + nrm(ks[11], (DEPTH, DIFF_DV), 0.02),
        'w_o': nrm(ks[12], (DEPTH, D_MIX, D_MODEL), BETA * D_MIX ** -0.5),
        'ln1_g': 1.0 + nrm(ks[13], (DEPTH, D_MODEL), 0.02),
        'ln1_b': nrm(ks[14], (DEPTH, D_MODEL), 0.01),
        'w_router': nrm(ks[15], (DEPTH, D_MODEL, N_EXPERTS), D_MODEL ** -0.5),
        'w1': nrm(ks[16], (DEPTH, N_EXPERTS, D_MODEL, D_FF_EXPERT), D_MODEL ** -0.5),
        'w3': nrm(ks[17], (DEPTH, N_EXPERTS, D_MODEL, D_FF_EXPERT), D_MODEL ** -0.5),
        'w2': nrm(ks[18], (DEPTH, N_EXPERTS, D_FF_EXPERT, D_MODEL), BETA * D_FF_EXPERT ** -0.5),
        'ln2_g': 1.0 + nrm(ks[19], (DEPTH, D_MODEL), 0.02),
        'ln2_b': nrm(ks[20], (DEPTH, D_MODEL), 0.01),
    }


def reference(x, c, ctx, c_ctx, w_ada, b_ada, w_in, w_gate2, b_gate, gla_norm_g, diff_lambda, diff_norm_g,
              w_o, ln1_g, ln1_b, w_router, w1, w3, w2, ln2_g, ln2_b):
    n_lat = x.shape[1]
    rows = n_lat // GRID_W
    cos, sin = axial_rope_tables(rows, DIFF_DH)
    for l in range(DEPTH):
        last = l == DEPTH - 1
        lam_init = 0.8 - 0.6 * math.exp(-0.3 * l)
        mod_l = jax.nn.silu(c) @ w_ada[l] + b_ada[l]
        mod_c = jax.nn.silu(c_ctx) @ w_ada[l] + b_ada[l]
        sh1, sc1, g1, sh2, sc2, g2 = jnp.split(mod_l[:, None, :], 6, axis=-1)
        csh1, csc1, cg1, csh2, csc2, cg2 = jnp.split(mod_c, 6, axis=-1)
        o_ctx, o_lat = hybrid_mixer(ctx * (1.0 + csc1) + csh1, x * (1.0 + sc1) + sh1, cos, sin, w_in[l],
                                    w_gate2[l], b_gate[l], gla_norm_g[l], diff_lambda[l], diff_norm_g[l],
                                    w_o[l], lam_init, not last)
        x = layer_norm(ALPHA * x + g1 * o_lat, ln1_g[l], ln1_b[l])
        f_lat = expert_choice_ffn(x * (1.0 + sc2) + sh2, w_router[l], w1[l], w3[l], w2[l])
        x = layer_norm(ALPHA * x + g2 * f_lat, ln2_g[l], ln2_b[l])
        if not last:
            ctx = layer_norm(ALPHA * ctx + cg1 * o_ctx, ln1_g[l], ln1_b[l])
            f_ctx = expert_choice_ffn(ctx * (1.0 + csc2) + csh2, w_router[l], w1[l], w3[l], w2[l])
            ctx = layer_norm(ALPHA * ctx + cg2 * f_ctx, ln2_g[l], ln2_b[l])
    return x
```

```python
import functools
import math

import jax
import jax.numpy as jnp
import numpy as np
from jax import lax
from jax.experimental import pallas as pl
from jax.experimental.pallas import tpu as pltpu

F32 = jnp.float32
BF16 = jnp.bfloat16
I32 = jnp.int32

GRID_W = 64
GLA_HEADS = 4
GLA_RANK = 16
GLA_TAU = 16.0
GLA_CHUNK = 64
DIFF_HEADS = 8
DIFF_DH = 64
ROPE_BASE = 10000.0
EC_CAPACITY_FACTOR = 2
LN_EPS = 1e-5
RMS_EPS = 1e-6
DEPTH = 1
ALPHA = (2.0 * DEPTH) ** 0.25
LAM_INIT = 0.8 - 0.6 * math.exp(-0.3 * 0)

LANES = 128
ADA_ROWS = 8
VMEM_LIMIT = 56 * 1024 * 1024

_NT = (((1,), (1,)), ((), ()))
_TN = (((0,), (0,)), ((), ()))


def _cparams(sem):
    return pltpu.CompilerParams(dimension_semantics=sem, vmem_limit_bytes=VMEM_LIMIT)


def _silu(a):
    return a * (1.0 / (1.0 + jnp.exp(-a)))


def _log_sigmoid(z):
    return jnp.minimum(z, 0.0) - jnp.log1p(jnp.exp(-jnp.abs(z)))


def _ada_kernel(c_ref, w_ref, b_ref, o_ref):
    s = _silu(c_ref[...]).astype(BF16)
    o_ref[...] = jnp.dot(s, w_ref[...].astype(BF16), preferred_element_type=F32) + b_ref[...]


def _ada(c_rows, w_ada, b_ada):
    d, n_out = w_ada.shape
    tn = min(1024, n_out)
    return pl.pallas_call(
        _ada_kernel,
        grid=(n_out // tn,),
        in_specs=[pl.BlockSpec((ADA_ROWS, d), lambda j: (0, 0)),
                  pl.BlockSpec((d, tn), lambda j: (0, j)),
                  pl.BlockSpec((1, tn), lambda j: (0, j))],
        out_specs=pl.BlockSpec((ADA_ROWS, tn), lambda j: (0, j)),
        out_shape=jax.ShapeDtypeStruct((ADA_ROWS, n_out), F32),
        compiler_params=_cparams(("arbitrary",)),
    )(c_rows, w_ada, b_ada.reshape(1, n_out))


def _proj_kernel(x_ref, sh_ref, sc_ref, w_ref, wlr_ref, wg_ref, bg_ref, cos_ref, sin_ref,
                 of_ref, ob_ref, lg_ref, u_ref, *, rope, n_f32, n_q, n_rope):
    n = pl.program_id(1)

    @pl.when(n == 0)
    def _():
        u = (x_ref[...] * (1.0 + sc_ref[...]) + sh_ref[...]).astype(BF16)
        u_ref[...] = u
        lr = jnp.dot(u, wlr_ref[...], preferred_element_type=F32)
        z = jnp.dot(lr.astype(BF16), wg_ref[...], preferred_element_type=F32) + bg_ref[...]
        lg_ref[...] = _log_sigmoid(z) * (1.0 / GLA_TAU)

    acc = jnp.dot(u_ref[...], w_ref[...], preferred_element_type=F32)

    @pl.when(n < n_f32)
    def _():
        of_ref[...] = acc

    @pl.when(jnp.logical_and(n >= n_f32, n < n_f32 + n_rope))
    def _():
        scale = jnp.where(n < n_f32 + n_q, DIFF_DH ** -0.5, 1.0).astype(F32)
        if rope:
            lane = lax.broadcasted_iota(I32, (acc.shape[0], LANES), 1)
            first = (lane % DIFF_DH) < (DIFF_DH // 2)
            cs = cos_ref[...]
            sn = sin_ref[...]
            for j in range(acc.shape[1] // LANES):
                a = acc[:, j * LANES:(j + 1) * LANES]
                partner = jnp.where(first, pltpu.roll(a, LANES - DIFF_DH // 2, 1), pltpu.roll(a, DIFF_DH // 2, 1))
                ob_ref[:, j * LANES:(j + 1) * LANES] = ((a * cs + partner * sn) * scale).astype(BF16)
        else:
            ob_ref[...] = (acc * scale).astype(BF16)

    @pl.when(n >= n_f32 + n_rope)
    def _():
        ob_ref[...] = acc.astype(BF16)


def _proj(x2d, mod3, row_of_tile, w_main, w_lr, w_g, b_g, cos_t, sin_t, *, rope, tm, n_tab_tiles):
    m, d = x2d.shape
    n_main = w_main.shape[1]
    tn = 512
    n_tiles = n_main // tn
    n_f32 = n_tiles // 2
    n_q = n_tiles // 6
    n_rope = n_tiles // 3
    r2 = w_lr.shape[1]
    ng = w_g.shape[1]
    kern = functools.partial(_proj_kernel, rope=rope, n_f32=n_f32, n_q=n_q, n_rope=n_rope)
    return pl.pallas_call(
        kern,
        grid=(m // tm, n_tiles),
        in_specs=[pl.BlockSpec((tm, d), lambda i, j: (i, 0)),
                  pl.BlockSpec((None, 1, d), lambda i, j: (row_of_tile(i), 0, 0)),
                  pl.BlockSpec((None, 1, d), lambda i, j: (row_of_tile(i), 0, 1)),
                  pl.BlockSpec((d, tn), lambda i, j: (0, j)),
                  pl.BlockSpec((d, r2), lambda i, j: (0, 0)),
                  pl.BlockSpec((r2, ng), lambda i, j: (0, 0)),
                  pl.BlockSpec((1, ng), lambda i, j: (0, 0)),
                  pl.BlockSpec((tm, LANES), lambda i, j: (i % n_tab_tiles, 0)),
                  pl.BlockSpec((tm, LANES), lambda i, j: (i % n_tab_tiles, 0))],
        out_specs=[pl.BlockSpec((tm, tn), lambda i, j: (i, jnp.minimum(j, n_f32 - 1))),
                   pl.BlockSpec((tm, tn), lambda i, j: (i, jnp.maximum(j - n_f32, 0))),
                   pl.BlockSpec((tm, ng), lambda i, j: (i, 0))],
        out_shape=[jax.ShapeDtypeStruct((m, n_f32 * tn), F32),
                   jax.ShapeDtypeStruct((m, (n_tiles - n_f32) * tn), BF16),
                   jax.ShapeDtypeStruct((m, ng), F32)],
        scratch_shapes=[pltpu.VMEM((tm, d), BF16)],
        compiler_params=_cparams(("arbitrary", "arbitrary")),
    )(x2d, mod3, mod3, w_main, w_lr, w_g, b_g, cos_t, sin_t)


def _gla_chunk(q, k, v, g, st_ref, tri, causal, *, want_out, scale):
    bcum = jnp.dot(tri, g, preferred_element_type=F32, precision=lax.Precision.HIGHEST)
    btot = jnp.sum(g, axis=0, keepdims=True)
    st = st_ref[...]
    vb = v.astype(BF16)
    k_out = (k * jnp.exp(btot - bcum)).astype(BF16)
    out = None
    if want_out:
        q_in = ((q * scale) * jnp.exp(bcum)).astype(BF16)
        k_in = (k * jnp.exp(-bcum)).astype(BF16)
        att = lax.dot_general(q_in, k_in, _NT, preferred_element_type=F32)
        att = jnp.where(causal, att, 0.0).astype(BF16)
        out = (jnp.dot(att, vb, preferred_element_type=F32)
               + lax.dot_general(q_in, st.astype(BF16), _NT, preferred_element_type=F32))
    upd = lax.dot_general(vb, k_out, _TN, preferred_element_type=F32)
    st_ref[...] = st * jnp.exp(btot) + upd
    return out


def _gla_kernel(q_ref, k_ref, v_ref, r_ref, gf_ref, gb_ref, kc_ref, vc_ref, gfc_ref, gbc_ref, ng_ref,
                o_ref, of_ref, obk_ref, sf_ref, sb_ref, *, n_lat, n_ctx, dk):
    c = GLA_CHUNK
    row = lax.broadcasted_iota(I32, (c, c), 0)
    col = lax.broadcasted_iota(I32, (c, c), 1)
    lower = row >= col
    upper = row <= col
    tri_f = lower.astype(F32)
    tri_b = upper.astype(F32)
    scale = dk ** -0.5
    sf_ref[...] = jnp.zeros_like(sf_ref)
    sb_ref[...] = jnp.zeros_like(sb_ref)
    nc_ctx = n_ctx // c
    nc_lat = n_lat // c

    def ctx_body(i, carry):
        rf = pl.multiple_of(i * c, c)
        rb = pl.multiple_of((nc_ctx - 1 - i) * c, c)
        _gla_chunk(None, kc_ref[pl.ds(rf, c), :], vc_ref[pl.ds(rf, c), :], gfc_ref[pl.ds(rf, c), :],
                   sf_ref, tri_f, lower, want_out=False, scale=scale)
        _gla_chunk(None, kc_ref[pl.ds(rb, c), :], vc_ref[pl.ds(rb, c), :], gbc_ref[pl.ds(rb, c), :],
                   sb_ref, tri_b, upper, want_out=False, scale=scale)
        return carry

    lax.fori_loop(0, nc_ctx, ctx_body, 0)

    def lat_body(i, carry):
        rf = pl.multiple_of(i * c, c)
        rb = pl.multiple_of((nc_lat - 1 - i) * c, c)
        of_ref[pl.ds(rf, c), :] = _gla_chunk(
            q_ref[pl.ds(rf, c), :], k_ref[pl.ds(rf, c), :], v_ref[pl.ds(rf, c), :], gf_ref[pl.ds(rf, c), :],
            sf_ref, tri_f, lower, want_out=True, scale=scale)
        obk_ref[pl.ds(rb, c), :] = _gla_chunk(
            q_ref[pl.ds(rb, c), :], k_ref[pl.ds(rb, c), :], v_ref[pl.ds(rb, c), :], gb_ref[pl.ds(rb, c), :],
            sb_ref, tri_b, upper, want_out=True, scale=scale)
        return carry

    lax.fori_loop(0, nc_lat, lat_body, 0)

    tb = min(512, n_lat)

    def fin_body(i, carry):
        r0 = pl.multiple_of(i * tb, tb)
        o = of_ref[pl.ds(r0, tb), :] + obk_ref[pl.ds(r0, tb), :]
        ms = jnp.mean(o * o, axis=-1, keepdims=True)
        o = o * lax.rsqrt(ms + RMS_EPS) * ng_ref[...]
        o_ref[pl.ds(r0, tb), :] = (o * _silu(r_ref[pl.ds(r0, tb), :])).astype(o_ref.dtype)
        return carry

    lax.fori_loop(0, n_lat // tb, fin_body, 0)


def _gla(pf_l, lg_l, pf_c, lg_c, norm_g, *, batch, n_lat, n_ctx):
    h = GLA_HEADS
    dk = pf_l.shape[1] // 6 // h
    dv = 2 * dk
    kern = functools.partial(_gla_kernel, n_lat=n_lat, n_ctx=n_ctx, dk=dk)
    lat = lambda width, off: pl.BlockSpec((n_lat, width), lambda b, i: (b, off + i))
    ctx = lambda width, off: pl.BlockSpec((n_ctx, width), lambda b, i: (b, off + i))
    return pl.pallas_call(
        kern,
        grid=(batch, h),
        in_specs=[lat(dk, 0), lat(dk, h), lat(dv, h), lat(dv, 2 * h),
                  lat(dk, 0), lat(dk, h),
                  ctx(dk, h), ctx(dv, h), ctx(dk, 0), ctx(dk, h),
                  pl.BlockSpec((1, dv), lambda b, i: (0, 0))],
        out_specs=pl.BlockSpec((n_lat, dv), lambda b, i: (b, i)),
        out_shape=jax.ShapeDtypeStruct((batch * n_lat, h * dv), BF16),
        scratch_shapes=[pltpu.VMEM((n_lat, dv), F32), pltpu.VMEM((n_lat, dv), F32),
                        pltpu.VMEM((dv, dk), F32), pltpu.VMEM((dv, dk), F32)],
        compiler_params=_cparams(("arbitrary", "arbitrary")),
    )(pf_l, pf_l, pf_l, pf_l, lg_l, lg_l, pf_c, pf_c, lg_c, lg_c, norm_g.reshape(1, dv))


def _diff_kernel(q_ref, kl_ref, vl_ref, kc_ref, vc_ref, lam_ref, ng_ref, o_ref):
    tq = q_ref.shape[0]
    q = q_ref[...]
    lane = lax.broadcasted_iota(I32, q.shape, 1)
    zero = jnp.zeros_like(q)
    qq = jnp.concatenate([jnp.where(lane < DIFF_DH, q, zero), jnp.where(lane >= DIFF_DH, q, zero)], axis=0)
    s_c = lax.dot_general(qq, kc_ref[...], _NT, preferred_element_type=F32)
    s_l = lax.dot_general(qq, kl_ref[...], _NT, preferred_element_type=F32)
    mx = jnp.maximum(jnp.max(s_c, axis=-1, keepdims=True), jnp.max(s_l, axis=-1, keepdims=True))
    e_c = jnp.exp(s_c - mx)
    e_l = jnp.exp(s_l - mx)
    inv = 1.0 / (jnp.sum(e_c, axis=-1, keepdims=True) + jnp.sum(e_l, axis=-1, keepdims=True))
    dl = lam_ref[...]
    lam = (jnp.exp(jnp.sum(dl[0:1] * dl[1:2], axis=-1, keepdims=True))
           - jnp.exp(jnp.sum(dl[2:3] * dl[3:4], axis=-1, keepdims=True)) + LAM_INIT)
    w1 = inv[:tq]
    w2 = inv[tq:] * lam
    a_c = (e_c[:tq] * w1 - e_c[tq:] * w2).astype(BF16)
    a_l = (e_l[:tq] * w1 - e_l[tq:] * w2).astype(BF16)
    o = (jnp.dot(a_c, vc_ref[...], preferred_element_type=F32)
         + jnp.dot(a_l, vl_ref[...], preferred_element_type=F32))
    ms = jnp.mean(o * o, axis=-1, keepdims=True)
    o_ref[...] = (o * lax.rsqrt(ms + RMS_EPS) * ng_ref[...] * (1.0 - LAM_INIT)).astype(o_ref.dtype)


def _diff(pb_l, pb_c, diff_lambda, norm_g, *, batch, n_lat, n_ctx):
    h = DIFF_HEADS
    dv = 2 * DIFF_DH
    tq = min(256, n_lat)
    nq = n_lat // tq
    return pl.pallas_call(
        _diff_kernel,
        grid=(batch, h, nq),
        in_specs=[pl.BlockSpec((tq, dv), lambda b, i, j: (b * nq + j, i)),
                  pl.BlockSpec((n_lat, dv), lambda b, i, j: (b, h + i)),
                  pl.BlockSpec((n_lat, dv), lambda b, i, j: (b, 2 * h + i)),
                  pl.BlockSpec((n_ctx, dv), lambda b, i, j: (b, h + i)),
                  pl.BlockSpec((n_ctx, dv), lambda b, i, j: (b, 2 * h + i)),
                  pl.BlockSpec(diff_lambda.shape, lambda b, i, j: (0, 0)),
                  pl.BlockSpec((1, dv), lambda b, i, j: (0, 0))],
        out_specs=pl.BlockSpec((tq, dv), lambda b, i, j: (b * nq + j, i)),
        out_shape=jax.ShapeDtypeStruct((batch * n_lat, h * dv), BF16),
        compiler_params=_cparams(("arbitrary", "arbitrary", "arbitrary")),
    )(pb_l, pb_l, pb_l, pb_c, pb_c, diff_lambda, norm_g.reshape(1, dv))


def _layer_norm(y, g, b):
    mu = jnp.mean(y, axis=-1, keepdims=True)
    yc = y - mu
    var = jnp.mean(yc * yc, axis=-1, keepdims=True)
    return yc * lax.rsqrt(var + LN_EPS) * g + b


def _out_kernel(ga_ref, df_ref, x_ref, wo_ref, g1_ref, sh2_ref, sc2_ref, lg_ref, lb_ref, wr_ref,
                x1_ref, h_ref, aff_ref):
    half = ga_ref.shape[1]
    o = (jnp.dot(ga_ref[...], wo_ref[:half, :], preferred_element_type=F32)
         + jnp.dot(df_ref[...], wo_ref[half:, :], preferred_element_type=F32))
    x1 = _layer_norm(ALPHA * x_ref[...] + g1_ref[...] * o, lg_ref[...], lb_ref[...])
    x1_ref[...] = x1
    hmod = x1 * (1.0 + sc2_ref[...]) + sh2_ref[...]
    h_ref[...] = hmod
    logits = lax.dot_general(wr_ref[...], hmod, _NT, preferred_element_type=F32,
                             precision=lax.Precision.HIGHEST)
    e = jnp.exp(logits - jnp.max(logits, axis=0, keepdims=True))
    aff_ref[...] = e / jnp.sum(e, axis=0, keepdims=True)


def _out(gla_o, diff_o, x2d, w_o, mod3, ln_g, ln_b, w_router_t, *, batch, n_lat):
    m, d = x2d.shape
    half = gla_o.shape[1]
    n_e = w_router_t.shape[0]
    tm = min(256, n_lat)
    tpb = n_lat // tm
    modspec = lambda k: pl.BlockSpec((None, 1, d), lambda i: (i // tpb, 0, k))
    vec = pl.BlockSpec((1, d), lambda i: (0, 0))
    return pl.pallas_call(
        _out_kernel,
        grid=(m // tm,),
        in_specs=[pl.BlockSpec((tm, half), lambda i: (i, 0)),
                  pl.BlockSpec((tm, half), lambda i: (i, 0)),
                  pl.BlockSpec((tm, d), lambda i: (i, 0)),
                  pl.BlockSpec((2 * half, d), lambda i: (0, 0)),
                  modspec(2), modspec(3), modspec(4), vec, vec,
                  pl.BlockSpec((n_e, d), lambda i: (0, 0))],
        out_specs=[pl.BlockSpec((tm, d), lambda i: (i, 0)),
                   pl.BlockSpec((tm, d), lambda i: (i, 0)),
                   pl.BlockSpec((None, n_e, tm), lambda i: (i // tpb, 0, i % tpb))],
        out_shape=[jax.ShapeDtypeStruct((m, d), F32),
                   jax.ShapeDtypeStruct((m, d), F32),
                   jax.ShapeDtypeStruct((batch, n_e, n_lat), F32)],
        compiler_params=_cparams(("arbitrary",)),
    )(gla_o, diff_o, x2d, w_o, mod3, mod3, mod3, ln_g.reshape(1, d), ln_b.reshape(1, d), w_router_t)


def _prefix_count(mask_f32, strict_upper):
    rows, n = mask_f32.shape
    carry = jnp.zeros((rows, 1), F32)
    pieces = []
    for j in range(n // LANES):
        blk = mask_f32[:, j * LANES:(j + 1) * LANES]
        within = jnp.dot(blk.astype(BF16), strict_upper, preferred_element_type=F32)
        pieces.append(within + carry)
        carry = carry + jnp.sum(blk, axis=-1, keepdims=True)
    return jnp.concatenate(pieces, axis=-1)


def _route_kernel(aff_ref, idx_ref, gate_ref, pos_ref, *, cap):
    b = pl.program_id(0)
    v = aff_ref[...]
    n_e, n = v.shape
    bits = pltpu.bitcast(v, I32)

    def search(i, t):
        cand = t | (1 << (30 - i))
        cnt = jnp.sum((bits >= cand).astype(I32), axis=-1, keepdims=True)
        return jnp.where(cnt >= cap, cand, t)

    thr = lax.fori_loop(0, 31, search, jnp.zeros((n_e, 1), I32))
    gt = bits > thr
    eq = bits == thr
    need = (cap - jnp.sum(gt.astype(I32), axis=-1, keepdims=True)).astype(F32)
    r = lax.broadcasted_iota(I32, (LANES, LANES), 0)
    cidx = lax.broadcasted_iota(I32, (LANES, LANES), 1)
    strict_upper = (r < cidx).astype(BF16)
    eq_rank = _prefix_count(eq.astype(F32), strict_upper)
    sel = jnp.logical_or(gt, jnp.logical_and(eq, eq_rank < need))
    pos = _prefix_count(sel.astype(F32), strict_upper)
    pos_ref[...] = jnp.where(sel, pos, -1.0).astype(I32)

    tok = lax.broadcasted_iota(I32, (8, n), 1)
    srow = lax.broadcasted_iota(I32, (8, n), 0)
    tok_hi = (tok >> 6).astype(F32)
    tok_lo = (tok & 63).astype(F32)
    slot = lax.broadcasted_iota(I32, (cap, n), 0)

    def compact(e, carry):
        ve = aff_ref[pl.ds(e, 1), :]
        v_hi = ve.astype(BF16).astype(F32)
        r1 = ve - v_hi
        v_mid = r1.astype(BF16).astype(F32)
        v_lo = r1 - v_mid
        lhs = jnp.where(srow == 0, tok_hi,
              jnp.where(srow == 1, tok_lo,
              jnp.where(srow == 2, v_hi,
              jnp.where(srow == 3, v_mid,
              jnp.where(srow == 4, v_lo, 0.0))))).astype(BF16)
        onehot = (slot == pos_ref[pl.ds(e, 1), :]).astype(F32).astype(BF16)
        res = lax.dot_general(lhs, onehot, _NT, preferred_element_type=F32)
        idx_ref[pl.ds(e, 1), :] = (res[0:1] * 64.0 + res[1:2]).astype(I32) + b * n
        gate_ref[pl.ds(e, 1), :] = (res[2:3] + res[3:4]) + res[4:5]
        return carry

    lax.fori_loop(0, n_e, compact, 0)


def _route(aff_t, *, cap):
    batch, n_e, n = aff_t.shape
    kern = functools.partial(_route_kernel, cap=cap)
    return pl.pallas_call(
        kern,
        grid=(batch,),
        in_specs=[pl.BlockSpec((None, n_e, n), lambda b: (b, 0, 0))],
        out_specs=[pl.BlockSpec((None, n_e, cap), lambda b: (b, 0, 0)),
                   pl.BlockSpec((None, n_e, cap), lambda b: (b, 0, 0))],
        out_shape=[jax.ShapeDtypeStruct((batch, n_e, cap), I32),
                   jax.ShapeDtypeStruct((batch, n_e, cap), F32)],
        scratch_shapes=[pltpu.VMEM((n_e, n), I32)],
        compiler_params=_cparams(("arbitrary",)),
    )(aff_t)


def _moe_kernel(idx_ref, gate_ref, h_hbm, w1_ref, w3_ref, w2_ref, facc_in, facc_hbm,
                xs_ref, acc_ref, buf_ref, sem, *, rows, chunk):
    del facc_in
    e = pl.program_id(0)
    f = pl.program_id(1)
    nf = pl.num_programs(1)
    base = e * rows
    n_chunks = rows // chunk

    def row_copy(src_hbm, k, j):
        tok = idx_ref[base + k * chunk + j]
        return pltpu.make_async_copy(src_hbm.at[pl.ds(tok, 1), :], buf_ref.at[pl.ds(j, 1), :], sem)

    def row_store(k, j):
        tok = idx_ref[base + k * chunk + j]
        return pltpu.make_async_copy(buf_ref.at[pl.ds(j, 1), :], facc_hbm.at[pl.ds(tok, 1), :], sem)

    def for_rows(fn):
        def body(j, carry):
            fn(j)
            return carry
        lax.fori_loop(0, chunk, body, 0)

    @pl.when(f == 0)
    def _():
        for k in range(n_chunks):
            for_rows(lambda j: row_copy(h_hbm, k, j).start())
            for_rows(lambda j: row_copy(h_hbm, k, j).wait())
            xs_ref[k * chunk:(k + 1) * chunk, :] = buf_ref[...].astype(BF16)

    xs = xs_ref[...]
    a = jnp.dot(xs, w1_ref[...].astype(BF16), preferred_element_type=F32)
    g = jnp.dot(xs, w3_ref[...].astype(BF16), preferred_element_type=F32)
    hid = (_silu(a) * g).astype(BF16)
    d = acc_ref.shape[1]
    tc = min(512, d)
    for j in range(d // tc):
        y = jnp.dot(hid, w2_ref[:, j * tc:(j + 1) * tc].astype(BF16), preferred_element_type=F32)

        @pl.when(f == 0)
        def _():
            acc_ref[:, j * tc:(j + 1) * tc] = y

        @pl.when(f > 0)
        def _():
            acc_ref[:, j * tc:(j + 1) * tc] += y

    @pl.when(f == nf - 1)
    def _():
        for k in range(n_chunks):
            for_rows(lambda j: row_copy(facc_hbm, k, j).start())
            for_rows(lambda j: row_copy(facc_hbm, k, j).wait())
            sl = slice(k * chunk, (k + 1) * chunk)
            buf_ref[...] += acc_ref[sl, :] * gate_ref[sl, :]
            for_rows(lambda j: row_store(k, j).start())
            for_rows(lambda j: row_store(k, j).wait())


def _moe(idx_flat, gate_col, h2d, w1, w3, w2, facc0, *, rows, chunk):
    m, d = h2d.shape
    n_e, _, dff = w1.shape
    tf = min(256, dff)
    kern = functools.partial(_moe_kernel, rows=rows, chunk=chunk)
    grid_spec = pltpu.PrefetchScalarGridSpec(
        num_scalar_prefetch=1,
        grid=(n_e, dff // tf),
        in_specs=[pl.BlockSpec((None, rows, 1), lambda e, f, idx: (e, 0, 0)),
                  pl.BlockSpec(memory_space=pl.ANY),
                  pl.BlockSpec((None, d, tf), lambda e, f, idx: (e, 0, f)),
                  pl.BlockSpec((None, d, tf), lambda e, f, idx: (e, 0, f)),
                  pl.BlockSpec((None, tf, d), lambda e, f, idx: (e, f, 0)),
                  pl.BlockSpec(memory_space=pl.ANY)],
        out_specs=pl.BlockSpec(memory_space=pl.ANY),
        scratch_shapes=[pltpu.VMEM((rows, d), BF16), pltpu.VMEM((rows, d), F32),
                        pltpu.VMEM((chunk, d), F32), pltpu.SemaphoreType.DMA(())],
    )
    return pl.pallas_call(
        kern,
        grid_spec=grid_spec,
        out_shape=jax.ShapeDtypeStruct((m, d), F32),
        input_output_aliases={6: 0},
        compiler_params=_cparams(("arbitrary", "arbitrary")),
    )(idx_flat, gate_col, h2d, w1, w3, w2, facc0)


def _final_kernel(x1_ref, f_ref, g2_ref, lg_ref, lb_ref, o_ref):
    o_ref[...] = _layer_norm(ALPHA * x1_ref[...] + g2_ref[...] * f_ref[...], lg_ref[...], lb_ref[...])


def _final(x1, facc, mod3, ln_g, ln_b, *, n_lat):
    m, d = x1.shape
    tm = min(512, n_lat)
    tpb = n_lat // tm
    vec = pl.BlockSpec((1, d), lambda i: (0, 0))
    return pl.pallas_call(
        _final_kernel,
        grid=(m // tm,),
        in_specs=[pl.BlockSpec((tm, d), lambda i: (i, 0)),
                  pl.BlockSpec((tm, d), lambda i: (i, 0)),
                  pl.BlockSpec((None, 1, d), lambda i: (i // tpb, 0, 5)),
                  vec, vec],
        out_specs=pl.BlockSpec((tm, d), lambda i: (i, 0)),
        out_shape=jax.ShapeDtypeStruct((m, d), F32),
        compiler_params=_cparams(("arbitrary",)),
    )(x1, facc, mod3, ln_g.reshape(1, d), ln_b.reshape(1, d))


def _rope_tables(n_lat):
    rows = n_lat // GRID_W
    row = jnp.repeat(jnp.arange(rows, dtype=F32), GRID_W)
    col = jnp.tile(jnp.arange(GRID_W, dtype=F32), rows)
    n_freq = DIFF_DH // 4
    inv = ROPE_BASE ** (-jnp.arange(n_freq, dtype=F32) / n_freq)
    ang = jnp.concatenate([row[:, None] * inv, col[:, None] * inv], axis=-1)
    cos, sin = jnp.cos(ang), jnp.sin(ang)
    reps = LANES // DIFF_DH
    cos_t = jnp.tile(jnp.concatenate([cos, cos], axis=-1), (1, reps))
    sin_t = jnp.tile(jnp.concatenate([-sin, sin], axis=-1), (1, reps))
    return cos_t, sin_t


def kernel(x, c, ctx, c_ctx, w_ada, b_ada, w_in, w_gate2, b_gate, gla_norm_g, diff_lambda, diff_norm_g, w_o,
           ln1_g, ln1_b, w_router, w1, w3, w2, ln2_g, ln2_b):
    batch, n_lat, d = x.shape
    n_ctx = ctx.shape[1]
    assert w_ada.shape[0] == DEPTH == 1 and batch < ADA_ROWS
    n_e = w_router.shape[-1]
    cap = EC_CAPACITY_FACTOR * n_lat // n_e
    l = 0

    c_rows = jnp.concatenate([c, c_ctx[None, :], jnp.zeros((ADA_ROWS - batch - 1, d), F32)], axis=0)
    mod3 = _ada(c_rows, w_ada[l], b_ada[l]).reshape(ADA_ROWS, 1, 6 * d)

    gqkvr = 2 * (d // 4) + 2 * (d // 2)
    r2 = 2 * GLA_RANK
    w_main = jnp.concatenate([w_in[l][:, :gqkvr], w_in[l][:, gqkvr + r2:]], axis=1).astype(BF16)
    w_lr = w_in[l][:, gqkvr:gqkvr + r2].astype(BF16)
    hk = w_gate2.shape[-1]
    zeros = jnp.zeros((GLA_RANK, hk), F32)
    w_g = jnp.concatenate([jnp.concatenate([w_gate2[l, 0], zeros], axis=1),
                           jnp.concatenate([zeros, w_gate2[l, 1]], axis=1)], axis=0).astype(BF16)
    b_g = b_gate[l].reshape(1, 2 * hk)
    cos_t, sin_t = _rope_tables(n_lat)

    tm_l = min(1024, n_lat)
    tpb = n_lat // tm_l
    x2d = x.reshape(batch * n_lat, d)
    pf_l, pb_l, lg_l = _proj(x2d, mod3, lambda i: i // tpb, w_main, w_lr, w_g, b_g, cos_t, sin_t,
                             rope=True, tm=tm_l, n_tab_tiles=tpb)
    ctx2d = ctx.reshape(batch * n_ctx, d)
    tm_c = min(1024, batch * n_ctx)
    pf_c, pb_c, lg_c = _proj(ctx2d, mod3, lambda i: batch, w_main, w_lr, w_g, b_g, cos_t, sin_t,
                             rope=False, tm=tm_c, n_tab_tiles=1)

    gla_o = _gla(pf_l, lg_l, pf_c, lg_c, gla_norm_g[l], batch=batch, n_lat=n_lat, n_ctx=n_ctx)
    diff_o = _diff(pb_l, pb_c, diff_lambda[l], diff_norm_g[l], batch=batch, n_lat=n_lat, n_ctx=n_ctx)

    x1, hmod, aff_t = _out(gla_o, diff_o, x2d, w_o[l].astype(BF16), mod3, ln1_g[l], ln1_b[l],
                           w_router[l].T, batch=batch, n_lat=n_lat)

    idx, gate = _route(aff_t, cap=cap)
    rows = batch * cap
    idx_flat = idx.transpose(1, 0, 2).reshape(n_e * rows)
    gate_col = gate.transpose(1, 0, 2).reshape(n_e, rows, 1)
    facc = _moe(idx_flat, gate_col, hmod, w1[l], w3[l], w2[l], jnp.zeros_like(hmod), rows=rows, chunk=cap)

    out = _final(x1, facc, mod3, ln2_g[l], ln2_b[l], n_lat=n_lat)
    return out.reshape(batch, n_lat, d)
```

```python
import functools
import math

import jax
import jax.numpy as jnp
import numpy as np
from jax import lax
from jax.experimental import pallas as pl
from jax.experimental.pallas import tpu as pltpu

F32 = jnp.float32
BF16 = jnp.bfloat16
I32 = jnp.int32

GRID_W = 64
GLA_HEADS = 4
GLA_RANK = 16
GLA_TAU = 16.0
GLA_CHUNK = 64
DIFF_HEADS = 8
DIFF_DH = 64
ROPE_BASE = 10000.0
EC_CAPACITY_FACTOR = 2
LN_EPS = 1e-5
RMS_EPS = 1e-6
DEPTH = 1
ALPHA = (2.0 * DEPTH) ** 0.25
LAM_INIT = 0.8 - 0.6 * math.exp(-0.3 * 0)

LANES = 128
ADA_ROWS = 8
VMEM_LIMIT = 56 * 1024 * 1024

_NT = (((1,), (1,)), ((), ()))
_TN = (((0,), (0,)), ((), ()))


def _cparams(sem):
    return pltpu.CompilerParams(dimension_semantics=sem, vmem_limit_bytes=VMEM_LIMIT)


def _silu(a):
    return a * (1.0 / (1.0 + jnp.exp(-a)))


def _log_sigmoid(z):
    return jnp.minimum(z, 0.0) - jnp.log1p(jnp.exp(-jnp.abs(z)))


def _ada_kernel(c_ref, w_ref, b_ref, o_ref):
    s = _silu(c_ref[...]).astype(BF16)
    o_ref[...] = jnp.dot(s, w_ref[...].astype(BF16), preferred_element_type=F32) + b_ref[...]


def _ada(c_rows, w_ada, b_ada):
    d, n_out = w_ada.shape
    tn = min(1024, n_out)
    return pl.pallas_call(
        _ada_kernel,
        grid=(n_out // tn,),
        in_specs=[pl.BlockSpec((ADA_ROWS, d), lambda j: (0, 0)),
                  pl.BlockSpec((d, tn), lambda j: (0, j)),
                  pl.BlockSpec((1, tn), lambda j: (0, j))],
        out_specs=pl.BlockSpec((ADA_ROWS, tn), lambda j: (0, j)),
        out_shape=jax.ShapeDtypeStruct((ADA_ROWS, n_out), F32),
        compiler_params=_cparams(("arbitrary",)),
    )(c_rows, w_ada, b_ada.reshape(1, n_out))


def _proj_kernel(x_ref, sh_ref, sc_ref, w_ref, wlr_ref, wg_ref, bg_ref, cos_ref, sin_ref,
                 of_ref, ob_ref, lg_ref, u_ref, *, rope, n_f32, n_q, n_rope):
    n = pl.program_id(1)

    @pl.when(n == 0)
    def _():
        u = (x_ref[...] * (1.0 + sc_ref[...]) + sh_ref[...]).astype(BF16)
        u_ref[...] = u
        lr = jnp.dot(u, wlr_ref[...], preferred_element_type=F32)
        z = jnp.dot(lr.astype(BF16), wg_ref[...], preferred_element_type=F32) + bg_ref[...]
        lg_ref[...] = _log_sigmoid(z) * (1.0 / GLA_TAU)

    acc = jnp.dot(u_ref[...], w_ref[...], preferred_element_type=F32)

    @pl.when(n < n_f32)
    def _():
        of_ref[...] = acc

    @pl.when(jnp.logical_and(n >= n_f32, n < n_f32 + n_rope))
    def _():
        scale = jnp.where(n < n_f32 + n_q, DIFF_DH ** -0.5, 1.0).astype(F32)
        if rope:
            lane = lax.broadcasted_iota(I32, (acc.shape[0], LANES), 1)
            first = (lane % DIFF_DH) < (DIFF_DH // 2)
            cs = cos_ref[...]
            sn = sin_ref[...]
            for j in range(acc.shape[1] // LANES):
                a = acc[:, j * LANES:(j + 1) * LANES]
                partner = jnp.where(first, pltpu.roll(a, LANES - DIFF_DH // 2, 1), pltpu.roll(a, DIFF_DH // 2, 1))
                ob_ref[:, j * LANES:(j + 1) * LANES] = ((a * cs + partner * sn) * scale).astype(BF16)
        else:
            ob_ref[...] = (acc * scale).astype(BF16)

    @pl.when(n >= n_f32 + n_rope)
    def _():
        ob_ref[...] = acc.astype(BF16)


def _proj(x2d, mod3, row_of_tile, w_main, w_lr, w_g, b_g, cos_t, sin_t, *, rope, tm, n_tab_tiles):
    m, d = x2d.shape
    n_main = w_main.shape[1]
    tn = 512
    n_tiles = n_main // tn
    n_f32 = n_tiles // 2
    n_q = n_tiles // 6
    n_rope = n_tiles // 3
    r2 = w_lr.shape[1]
    ng = w_g.shape[1]
    kern = functools.partial(_proj_kernel, rope=rope, n_f32=n_f32, n_q=n_q, n_rope=n_rope)
    return pl.pallas_call(
        kern,
        grid=(m // tm, n_tiles),
        in_specs=[pl.BlockSpec((tm, d), lambda i, j: (i, 0)),
                  pl.BlockSpec((None, 1, d), lambda i, j: (row_of_tile(i), 0, 0)),
                  pl.BlockSpec((None, 1, d), lambda i, j: (row_of_tile(i), 0, 1)),
                  pl.BlockSpec((d, tn), lambda i, j: (0, j)),
                  pl.BlockSpec((d, r2), lambda i, j: (0, 0)),
                  pl.BlockSpec((r2, ng), lambda i, j: (0, 0)),
                  pl.BlockSpec((1, ng), lambda i, j: (0, 0)),
                  pl.BlockSpec((tm, LANES), lambda i, j: (i % n_tab_tiles, 0)),
                  pl.BlockSpec((tm, LANES), lambda i, j: (i % n_tab_tiles, 0))],
        out_specs=[pl.BlockSpec((tm, tn), lambda i, j: (i, jnp.minimum(j, n_f32 - 1))),
                   pl.BlockSpec((tm, tn), lambda i, j: (i, jnp.maximum(j - n_f32, 0))),
                   pl.BlockSpec((tm, ng), lambda i, j: (i, 0))],
        out_shape=[jax.ShapeDtypeStruct((m, n_f32 * tn), F32),
                   jax.ShapeDtypeStruct((m, (n_tiles - n_f32) * tn), BF16),
                   jax.ShapeDtypeStruct((m, ng), F32)],
        scratch_shapes=[pltpu.VMEM((tm, d), BF16)],
        compiler_params=_cparams(("arbitrary", "arbitrary")),
    )(x2d, mod3, mod3, w_main, w_lr, w_g, b_g, cos_t, sin_t)


def _gla_chunk(q, k, v, g, st_ref, tri, causal, *, want_out, scale):
    bcum = jnp.dot(tri, g, preferred_element_type=F32, precision=lax.Precision.HIGHEST)
    btot = jnp.sum(g, axis=0, keepdims=True)
    st = st_ref[...]
    vb = v.astype(BF16)
    k_out = (k * jnp.exp(btot - bcum)).astype(BF16)
    out = None
    if want_out:
        q_in = ((q * scale) * jnp.exp(bcum)).astype(BF16)
        k_in = (k * jnp.exp(-bcum)).astype(BF16)
        att = lax.dot_general(q_in, k_in, _NT, preferred_element_type=F32)
        att = jnp.where(causal, att, 0.0).astype(BF16)
        out = (jnp.dot(att, vb, preferred_element_type=F32)
               + lax.dot_general(q_in, st.astype(BF16), _NT, preferred_element_type=F32))
    upd = lax.dot_general(vb, k_out, _TN, preferred_element_type=F32)
    st_ref[...] = st * jnp.exp(btot) + upd
    return out


def _gla_kernel(q_ref, k_ref, v_ref, r_ref, gf_ref, gb_ref, kc_ref, vc_ref, gfc_ref, gbc_ref, ng_ref,
                o_ref, of_ref, obk_ref, sf_ref, sb_ref, *, n_lat, n_ctx, dk):
    c = GLA_CHUNK
    row = lax.broadcasted_iota(I32, (c, c), 0)
    col = lax.broadcasted_iota(I32, (c, c), 1)
    lower = row >= col
    upper = row <= col
    tri_f = lower.astype(F32)
    tri_b = upper.astype(F32)
    scale = dk ** -0.5
    sf_ref[...] = jnp.zeros_like(sf_ref)
    sb_ref[...] = jnp.zeros_like(sb_ref)
    nc_ctx = n_ctx // c
    nc_lat = n_lat // c

    def ctx_body(i, carry):
        rf = pl.multiple_of(i * c, c)
        rb = pl.multiple_of((nc_ctx - 1 - i) * c, c)
        _gla_chunk(None, kc_ref[pl.ds(rf, c), :], vc_ref[pl.ds(rf, c), :], gfc_ref[pl.ds(rf, c), :],
                   sf_ref, tri_f, lower, want_out=False, scale=scale)
        _gla_chunk(None, kc_ref[pl.ds(rb, c), :], vc_ref[pl.ds(rb, c), :], gbc_ref[pl.ds(rb, c), :],
                   sb_ref, tri_b, upper, want_out=False, scale=scale)
        return carry

    lax.fori_loop(0, nc_ctx, ctx_body, 0)

    def lat_body(i, carry):
        rf = pl.multiple_of(i * c, c)
        rb = pl.multiple_of((nc_lat - 1 - i) * c, c)
        of_ref[pl.ds(rf, c), :] = _gla_chunk(
            q_ref[pl.ds(rf, c), :], k_ref[pl.ds(rf, c), :], v_ref[pl.ds(rf, c), :], gf_ref[pl.ds(rf, c), :],
            sf_ref, tri_f, lower, want_out=True, scale=scale)
        obk_ref[pl.ds(rb, c), :] = _gla_chunk(
            q_ref[pl.ds(rb, c), :], k_ref[pl.ds(rb, c), :], v_ref[pl.ds(rb, c), :], gb_ref[pl.ds(rb, c), :],
            sb_ref, tri_b, upper, want_out=True, scale=scale)
        return carry

    lax.fori_loop(0, nc_lat, lat_body, 0)

    tb = min(512, n_lat)

    def fin_body(i, carry):
        r0 = pl.multiple_of(i * tb, tb)
        o = of_ref[pl.ds(r0, tb), :] + obk_ref[pl.ds(r0, tb), :]
        ms = jnp.mean(o * o, axis=-1, keepdims=True)
        o = o * lax.rsqrt(ms + RMS_EPS) * ng_ref[...]
        o_ref[pl.ds(r0, tb), :] = (o * _silu(r_ref[pl.ds(r0, tb), :])).astype(o_ref.dtype)
        return carry

    lax.fori_loop(0, n_lat // tb, fin_body, 0)


def _gla(pf_l, lg_l, pf_c, lg_c, norm_g, *, batch, n_lat, n_ctx):
    h = GLA_HEADS
    dk = pf_l.shape[1] // 6 // h
    dv = 2 * dk
    kern = functools.partial(_gla_kernel, n_lat=n_lat, n_ctx=n_ctx, dk=dk)
    lat = lambda width, off: pl.BlockSpec((n_lat, width), lambda b, i: (b, off + i))
    ctx = lambda width, off: pl.BlockSpec((n_ctx, width), lambda b, i: (b, off + i))
    return pl.pallas_call(
        kern,
        grid=(batch, h),
        in_specs=[lat(dk, 0), lat(dk, h), lat(dv, h), lat(dv, 2 * h),
                  lat(dk, 0), lat(dk, h),
                  ctx(dk, h), ctx(dv, h), ctx(dk, 0), ctx(dk, h),
                  pl.BlockSpec((1, dv), lambda b, i: (0, 0))],
        out_specs=pl.BlockSpec((n_lat, dv), lambda b, i: (b, i)),
        out_shape=jax.ShapeDtypeStruct((batch * n_lat, h * dv), BF16),
        scratch_shapes=[pltpu.VMEM((n_lat, dv), F32), pltpu.VMEM((n_lat, dv), F32),
                        pltpu.VMEM((dv, dk), F32), pltpu.VMEM((dv, dk), F32)],
        compiler_params=_cparams(("arbitrary", "arbitrary")),
    )(pf_l, pf_l, pf_l, pf_l, lg_l, lg_l, pf_c, pf_c, lg_c, lg_c, norm_g.reshape(1, dv))


def _diff_kernel(q_ref, kl_ref, vl_ref, kc_ref, vc_ref, lam_ref, ng_ref, o_ref):
    tq = q_ref.shape[0]
    q = q_ref[...]
    lane = lax.broadcasted_iota(I32, q.shape, 1)
    zero = jnp.zeros_like(q)
    qq = jnp.concatenate([jnp.where(lane < DIFF_DH, q, zero), jnp.where(lane >= DIFF_DH, q, zero)], axis=0)
    s_c = lax.dot_general(qq, kc_ref[...], _NT, preferred_element_type=F32)
    s_l = lax.dot_general(qq, kl_ref[...], _NT, preferred_element_type=F32)
    mx = jnp.maximum(jnp.max(s_c, axis=-1, keepdims=True), jnp.max(s_l, axis=-1, keepdims=True))
    e_c = jnp.exp(s_c - mx)
    e_l = jnp.exp(s_l - mx)
    inv = 1.0 / (jnp.sum(e_c, axis=-1, keepdims=True) + jnp.sum(e_l, axis=-1, keepdims=True))
    dl = lam_ref[...]
    lam = (jnp.exp(jnp.sum(dl[0:1] * dl[1:2], axis=-1, keepdims=True))
           - jnp.exp(jnp.sum(dl[2:3] * dl[3:4], axis=-1, keepdims=True)) + LAM_INIT)
    w1 = inv[:tq]
    w2 = inv[tq:] * lam
    a_c = (e_c[:tq] * w1 - e_c[tq:] * w2).astype(BF16)
    a_l = (e_l[:tq] * w1 - e_l[tq:] * w2).astype(BF16)
    o = (jnp.dot(a_c, vc_ref[...], preferred_element_type=F32)
         + jnp.dot(a_l, vl_ref[...], preferred_element_type=F32))
    ms = jnp.mean(o * o, axis=-1, keepdims=True)
    o_ref[...] = (o * lax.rsqrt(ms + RMS_EPS) * ng_ref[...] * (1.0 - LAM_INIT)).astype(o_ref.dtype)


def _diff(pb_l, pb_c, diff_lambda, norm_g, *, batch, n_lat, n_ctx):
    h = DIFF_HEADS
    dv = 2 * DIFF_DH
    tq = min(256, n_lat)
    nq = n_lat // tq
    return pl.pallas_call(
        _diff_kernel,
        grid=(batch, h, nq),
        in_specs=[pl.BlockSpec((tq, dv), lambda b, i, j: (b * nq + j, i)),
                  pl.BlockSpec((n_lat, dv), lambda b, i, j: (b, h + i)),
                  pl.BlockSpec((n_lat, dv), lambda b, i, j: (b, 2 * h + i)),
                  pl.BlockSpec((n_ctx, dv), lambda b, i, j: (b, h + i)),
                  pl.BlockSpec((n_ctx, dv), lambda b, i, j: (b, 2 * h + i)),
                  pl.BlockSpec(diff_lambda.shape, lambda b, i, j: (0, 0)),
                  pl.BlockSpec((1, dv), lambda b, i, j: (0, 0))],
        out_specs=pl.BlockSpec((tq, dv), lambda b, i, j: (b * nq + j, i)),
        out_shape=jax.ShapeDtypeStruct((batch * n_lat, h * dv), BF16),
        compiler_params=_cparams(("arbitrary", "arbitrary", "arbitrary")),
    )(pb_l, pb_l, pb_l, pb_c, pb_c, diff_lambda, norm_g.reshape(1, dv))


def _layer_norm(y, g, b):
    mu = jnp.mean(y, axis=-1, keepdims=True)
    yc = y - mu
    var = jnp.mean(yc * yc, axis=-1, keepdims=True)
    return yc * lax.rsqrt(var + LN_EPS) * g + b


def _out_kernel(ga_ref, df_ref, x_ref, wo_ref, g1_ref, sh2_ref, sc2_ref, lg_ref, lb_ref, wr_ref,
                x1_ref, h_ref, aff_ref):
    half = ga_ref.shape[1]
    o = (jnp.dot(ga_ref[...], wo_ref[:half, :], preferred_element_type=F32)
         + jnp.dot(df_ref[...], wo_ref[half:, :], preferred_element_type=F32))
    x1 = _layer_norm(ALPHA * x_ref[...] + g1_ref[...] * o, lg_ref[...], lb_ref[...])
    x1_ref[...] = x1
    hmod = x1 * (1.0 + sc2_ref[...]) + sh2_ref[...]
    hbits = pltpu.bitcast(hmod.astype(BF16).astype(F32), I32)
    hw = h_ref.shape[1]
    h_ref[...] = (hbits[:, hw:] & jnp.int32(-65536)) | lax.shift_right_logical(hbits[:, :hw], 16)
    logits = lax.dot_general(wr_ref[...], hmod, _NT, preferred_element_type=F32,
                             precision=lax.Precision.HIGHEST)
    e = jnp.exp(logits - jnp.max(logits, axis=0, keepdims=True))
    aff_ref[...] = e / jnp.sum(e, axis=0, keepdims=True)


def _out(gla_o, diff_o, x2d, w_o, mod3, ln_g, ln_b, w_router_t, *, batch, n_lat):
    m, d = x2d.shape
    half = gla_o.shape[1]
    n_e = w_router_t.shape[0]
    tm = min(256, n_lat)
    tpb = n_lat // tm
    modspec = lambda k: pl.BlockSpec((None, 1, d), lambda i: (i // tpb, 0, k))
    vec = pl.BlockSpec((1, d), lambda i: (0, 0))
    return pl.pallas_call(
        _out_kernel,
        grid=(m // tm,),
        in_specs=[pl.BlockSpec((tm, half), lambda i: (i, 0)),
                  pl.BlockSpec((tm, half), lambda i: (i, 0)),
                  pl.BlockSpec((tm, d), lambda i: (i, 0)),
                  pl.BlockSpec((2 * half, d), lambda i: (0, 0)),
                  modspec(2), modspec(3), modspec(4), vec, vec,
                  pl.BlockSpec((n_e, d), lambda i: (0, 0))],
        out_specs=[pl.BlockSpec((tm, d), lambda i: (i, 0)),
                   pl.BlockSpec((tm, d // 2), lambda i: (i, 0)),
                   pl.BlockSpec((None, n_e, tm), lambda i: (i // tpb, 0, i % tpb))],
        out_shape=[jax.ShapeDtypeStruct((m, d), F32),
                   jax.ShapeDtypeStruct((m, d // 2), I32),
                   jax.ShapeDtypeStruct((batch, n_e, n_lat), F32)],
        compiler_params=_cparams(("arbitrary",)),
    )(gla_o, diff_o, x2d, w_o, mod3, mod3, mod3, ln_g.reshape(1, d), ln_b.reshape(1, d), w_router_t)


def _prefix_count(mask_f32, strict_upper):
    rows, n = mask_f32.shape
    carry = jnp.zeros((rows, 1), F32)
    pieces = []
    for j in range(n // LANES):
        blk = mask_f32[:, j * LANES:(j + 1) * LANES]
        within = jnp.dot(blk.astype(BF16), strict_upper, preferred_element_type=F32)
        pieces.append(within + carry)
        carry = carry + jnp.sum(blk, axis=-1, keepdims=True)
    return jnp.concatenate(pieces, axis=-1)


def _route_kernel(aff_ref, idx_ref, gate_ref, pos_ref, *, cap):
    b = pl.program_id(0)
    v = aff_ref[...]
    n_e, n = v.shape
    bits = pltpu.bitcast(v, I32)

    def search(i, t):
        cand = t | (1 << (30 - i))
        cnt = jnp.sum((bits >= cand).astype(I32), axis=-1, keepdims=True)
        return jnp.where(cnt >= cap, cand, t)

    thr = lax.fori_loop(0, 31, search, jnp.zeros((n_e, 1), I32))
    gt = bits > thr
    eq = bits == thr
    need = (cap - jnp.sum(gt.astype(I32), axis=-1, keepdims=True)).astype(F32)
    r = lax.broadcasted_iota(I32, (LANES, LANES), 0)
    cidx = lax.broadcasted_iota(I32, (LANES, LANES), 1)
    strict_upper = (r < cidx).astype(BF16)
    eq_rank = _prefix_count(eq.astype(F32), strict_upper)
    sel = jnp.logical_or(gt, jnp.logical_and(eq, eq_rank < need))
    pos = _prefix_count(sel.astype(F32), strict_upper)
    pos_ref[...] = jnp.where(sel, pos, -1.0).astype(I32)

    tok = lax.broadcasted_iota(I32, (8, n), 1)
    srow = lax.broadcasted_iota(I32, (8, n), 0)
    tok_hi = (tok >> 6).astype(F32)
    tok_lo = (tok & 63).astype(F32)
    slot = lax.broadcasted_iota(I32, (cap, n), 0)

    def compact(e, carry):
        ve = aff_ref[pl.ds(e, 1), :]
        v_hi = ve.astype(BF16).astype(F32)
        r1 = ve - v_hi
        v_mid = r1.astype(BF16).astype(F32)
        v_lo = r1 - v_mid
        lhs = jnp.where(srow == 0, tok_hi,
              jnp.where(srow == 1, tok_lo,
              jnp.where(srow == 2, v_hi,
              jnp.where(srow == 3, v_mid,
              jnp.where(srow == 4, v_lo, 0.0))))).astype(BF16)
        onehot = (slot == pos_ref[pl.ds(e, 1), :]).astype(F32).astype(BF16)
        res = lax.dot_general(lhs, onehot, _NT, preferred_element_type=F32)
        idx_ref[pl.ds(e, 1), :] = (res[0:1] * 64.0 + res[1:2]).astype(I32) + b * n
        gate_ref[pl.ds(e, 1), :] = (res[2:3] + res[3:4]) + res[4:5]
        return carry

    lax.fori_loop(0, n_e, compact, 0)


def _route(aff_t, *, cap):
    batch, n_e, n = aff_t.shape
    kern = functools.partial(_route_kernel, cap=cap)
    return pl.pallas_call(
        kern,
        grid=(batch,),
        in_specs=[pl.BlockSpec((None, n_e, n), lambda b: (b, 0, 0))],
        out_specs=[pl.BlockSpec((None, n_e, cap), lambda b: (b, 0, 0)),
                   pl.BlockSpec((None, n_e, cap), lambda b: (b, 0, 0))],
        out_shape=[jax.ShapeDtypeStruct((batch, n_e, cap), I32),
                   jax.ShapeDtypeStruct((batch, n_e, cap), F32)],
        scratch_shapes=[pltpu.VMEM((n_e, n), I32)],
        compiler_params=_cparams(("arbitrary",)),
    )(aff_t)


DMA_UNROLL = 8


def _moe_kernel(idx_ref, gate_ref, h_hbm, w1_ref, w3_ref, w2_ref, facc_in, facc_hbm,
                stage_ref, acc_ref, gsem, rsem, wsem, *, rows):
    del facc_in
    e = pl.program_id(0)
    f = pl.program_id(1)
    n_e = pl.num_programs(0)
    nf = pl.num_programs(1)
    slot = e % 2

    def issue_rows(make):
        def body(i, carry):
            for u in range(DMA_UNROLL):
                make(i * DMA_UNROLL + u).start()
            return carry
        lax.fori_loop(0, rows // DMA_UNROLL, body, 0)

    def x_row(ex, s):
        return lambda k: pltpu.make_async_copy(
            h_hbm.at[pl.ds(idx_ref[ex * rows + k], 1), :], stage_ref.at[s, pl.ds(k, 1), :], gsem.at[s])

    def acc_row_in(k):
        return pltpu.make_async_copy(facc_hbm.at[pl.ds(idx_ref[e * rows + k], 1), :], acc_ref.at[pl.ds(k, 1), :], rsem)

    def acc_row_out(k):
        return pltpu.make_async_copy(acc_ref.at[pl.ds(k, 1), :], facc_hbm.at[pl.ds(idx_ref[e * rows + k], 1), :], wsem)

    def wait_x(s):
        pltpu.make_async_copy(h_hbm.at[pl.ds(0, rows), :], stage_ref.at[s], gsem.at[s]).wait()

    def wait_acc_in():
        pltpu.make_async_copy(facc_hbm.at[pl.ds(0, rows), :], acc_ref, rsem).wait()

    def wait_acc_out():
        pltpu.make_async_copy(acc_ref, facc_hbm.at[pl.ds(0, rows), :], wsem).wait()

    @pl.when(f == 0)
    def _():
        @pl.when(e > 0)
        def _():
            wait_acc_out()
        issue_rows(acc_row_in)

        @pl.when(e == 0)
        def _():
            issue_rows(x_row(0, 0))
        wait_x(slot)

        @pl.when(e + 1 < n_e)
        def _():
            issue_rows(x_row(e + 1, 1 - slot))
        wait_acc_in()

    u = stage_ref[slot]
    half = u.shape[1]
    x_lo = pltpu.bitcast(u << 16, F32).astype(BF16)
    x_hi = pltpu.bitcast(u & jnp.int32(-65536), F32).astype(BF16)
    tf = w1_ref.shape[1]
    w13 = jnp.concatenate([w1_ref[...], w3_ref[...]], axis=1).astype(BF16)
    ag = (jnp.dot(x_lo, w13[:half], preferred_element_type=F32)
          + jnp.dot(x_hi, w13[half:], preferred_element_type=F32))
    hid = (_silu(ag[:, :tf]) * ag[:, tf:]).astype(BF16)
    gate = gate_ref[...]
    d = acc_ref.shape[1]
    tc = min(512, d)
    for j in range(d // tc):
        y = jnp.dot(hid, w2_ref[:, j * tc:(j + 1) * tc].astype(BF16), preferred_element_type=F32)
        acc_ref[:, j * tc:(j + 1) * tc] += gate * y

    @pl.when(f == nf - 1)
    def _():
        issue_rows(acc_row_out)

        @pl.when(e == n_e - 1)
        def _():
            wait_acc_out()


def _moe(idx_flat, gate_col, h_packed, w1, w3, w2, facc0, *, rows):
    m, d = facc0.shape
    n_e, _, dff = w1.shape
    tf = min(256, dff)
    kern = functools.partial(_moe_kernel, rows=rows)
    grid_spec = pltpu.PrefetchScalarGridSpec(
        num_scalar_prefetch=1,
        grid=(n_e, dff // tf),
        in_specs=[pl.BlockSpec((None, rows, 1), lambda e, f, idx: (e, 0, 0)),
                  pl.BlockSpec(memory_space=pl.ANY),
                  pl.BlockSpec((None, d, tf), lambda e, f, idx: (e, 0, f)),
                  pl.BlockSpec((None, d, tf), lambda e, f, idx: (e, 0, f)),
                  pl.BlockSpec((None, tf, d), lambda e, f, idx: (e, f, 0)),
                  pl.BlockSpec(memory_space=pl.ANY)],
        out_specs=pl.BlockSpec(memory_space=pl.ANY),
        scratch_shapes=[pltpu.VMEM((2, rows, d // 2), I32), pltpu.VMEM((rows, d), F32),
                        pltpu.SemaphoreType.DMA((2,)), pltpu.SemaphoreType.DMA(()), pltpu.SemaphoreType.DMA(())],
    )
    return pl.pallas_call(
        kern,
        grid_spec=grid_spec,
        out_shape=jax.ShapeDtypeStruct((m, d), F32),
        input_output_aliases={6: 0},
        compiler_params=_cparams(("arbitrary", "arbitrary")),
    )(idx_flat, gate_col, h_packed, w1, w3, w2, facc0)


def _final_kernel(x1_ref, f_ref, g2_ref, lg_ref, lb_ref, o_ref):
    o_ref[...] = _layer_norm(ALPHA * x1_ref[...] + g2_ref[...] * f_ref[...], lg_ref[...], lb_ref[...])


def _final(x1, facc, mod3, ln_g, ln_b, *, n_lat):
    m, d = x1.shape
    tm = min(512, n_lat)
    tpb = n_lat // tm
    vec = pl.BlockSpec((1, d), lambda i: (0, 0))
    return pl.pallas_call(
        _final_kernel,
        grid=(m // tm,),
        in_specs=[pl.BlockSpec((tm, d), lambda i: (i, 0)),
                  pl.BlockSpec((tm, d), lambda i: (i, 0)),
                  pl.BlockSpec((None, 1, d), lambda i: (i // tpb, 0, 5)),
                  vec, vec],
        out_specs=pl.BlockSpec((tm, d), lambda i: (i, 0)),
        out_shape=jax.ShapeDtypeStruct((m, d), F32),
        compiler_params=_cparams(("arbitrary",)),
    )(x1, facc, mod3, ln_g.reshape(1, d), ln_b.reshape(1, d))


def _rope_tables(n_lat):
    rows = n_lat // GRID_W
    row = jnp.repeat(jnp.arange(rows, dtype=F32), GRID_W)
    col = jnp.tile(jnp.arange(GRID_W, dtype=F32), rows)
    n_freq = DIFF_DH // 4
    inv = ROPE_BASE ** (-jnp.arange(n_freq, dtype=F32) / n_freq)
    ang = jnp.concatenate([row[:, None] * inv, col[:, None] * inv], axis=-1)
    cos, sin = jnp.cos(ang), jnp.sin(ang)
    reps = LANES // DIFF_DH
    cos_t = jnp.tile(jnp.concatenate([cos, cos], axis=-1), (1, reps))
    sin_t = jnp.tile(jnp.concatenate([-sin, sin], axis=-1), (1, reps))
    return cos_t, sin_t


def kernel(x, c, ctx, c_ctx, w_ada, b_ada, w_in, w_gate2, b_gate, gla_norm_g, diff_lambda, diff_norm_g, w_o,
           ln1_g, ln1_b, w_router, w1, w3, w2, ln2_g, ln2_b):
    batch, n_lat, d = x.shape
    n_ctx = ctx.shape[1]
    assert w_ada.shape[0] == DEPTH == 1 and batch < ADA_ROWS
    n_e = w_router.shape[-1]
    cap = EC_CAPACITY_FACTOR * n_lat // n_e
    l = 0

    c_rows = jnp.concatenate([c, c_ctx[None, :], jnp.zeros((ADA_ROWS - batch - 1, d), F32)], axis=0)
    mod3 = _ada(c_rows, w_ada[l], b_ada[l]).reshape(ADA_ROWS, 1, 6 * d)

    gqkvr = 2 * (d // 4) + 2 * (d // 2)
    r2 = 2 * GLA_RANK
    w_main = jnp.concatenate([w_in[l][:, :gqkvr], w_in[l][:, gqkvr + r2:]], axis=1).astype(BF16)
    w_lr = w_in[l][:, gqkvr:gqkvr + r2].astype(BF16)
    hk = w_gate2.shape[-1]
    zeros = jnp.zeros((GLA_RANK, hk), F32)
    w_g = jnp.concatenate([jnp.concatenate([w_gate2[l, 0], zeros], axis=1),
                           jnp.concatenate([zeros, w_gate2[l, 1]], axis=1)], axis=0).astype(BF16)
    b_g = b_gate[l].reshape(1, 2 * hk)
    cos_t, sin_t = _rope_tables(n_lat)

    tm_l = min(1024, n_lat)
    tpb = n_lat // tm_l
    x2d = x.reshape(batch * n_lat, d)
    pf_l, pb_l, lg_l = _proj(x2d, mod3, lambda i: i // tpb, w_main, w_lr, w_g, b_g, cos_t, sin_t,
                             rope=True, tm=tm_l, n_tab_tiles=tpb)
    ctx2d = ctx.reshape(batch * n_ctx, d)
    tm_c = min(1024, batch * n_ctx)
    pf_c, pb_c, lg_c = _proj(ctx2d, mod3, lambda i: batch, w_main, w_lr, w_g, b_g, cos_t, sin_t,
                             rope=False, tm=tm_c, n_tab_tiles=1)

    gla_o = _gla(pf_l, lg_l, pf_c, lg_c, gla_norm_g[l], batch=batch, n_lat=n_lat, n_ctx=n_ctx)
    diff_o = _diff(pb_l, pb_c, diff_lambda[l], diff_norm_g[l], batch=batch, n_lat=n_lat, n_ctx=n_ctx)

    x1, h_packed, aff_t = _out(gla_o, diff_o, x2d, w_o[l].astype(BF16), mod3, ln1_g[l], ln1_b[l],
                           w_router[l].T, batch=batch, n_lat=n_lat)

    idx, gate = _route(aff_t, cap=cap)
    rows = batch * cap
    idx_flat = idx.transpose(1, 0, 2).reshape(n_e * rows)
    gate_col = gate.transpose(1, 0, 2).reshape(n_e, rows, 1)
    facc = _moe(idx_flat, gate_col, h_packed, w1[l], w3[l], w2[l], jnp.zeros_like(x1), rows=rows)

    out = _final(x1, facc, mod3, ln2_g[l], ln2_b[l], n_lat=n_lat)
    return out.reshape(batch, n_lat, d)
```

```python
import functools
import math

import jax
import jax.numpy as jnp
import numpy as np
from jax import lax
from jax.experimental import pallas as pl
from jax.experimental.pallas import tpu as pltpu

F32 = jnp.float32
BF16 = jnp.bfloat16
I32 = jnp.int32

GRID_W = 64
GLA_HEADS = 4
GLA_RANK = 16
GLA_TAU = 16.0
GLA_CHUNK = 64
DIFF_HEADS = 8
DIFF_DH = 64
ROPE_BASE = 10000.0
EC_CAPACITY_FACTOR = 2
LN_EPS = 1e-5
RMS_EPS = 1e-6
DEPTH = 1
ALPHA = (2.0 * DEPTH) ** 0.25
LAM_INIT = 0.8 - 0.6 * math.exp(-0.3 * 0)

LANES = 128
ADA_ROWS = 8
VMEM_LIMIT = 56 * 1024 * 1024

_NT = (((1,), (1,)), ((), ()))
_TN = (((0,), (0,)), ((), ()))


def _cparams(sem):
    return pltpu.CompilerParams(dimension_semantics=sem, vmem_limit_bytes=VMEM_LIMIT)


def _silu(a):
    return a * (1.0 / (1.0 + jnp.exp(-a)))


def _log_sigmoid(z):
    return jnp.minimum(z, 0.0) - jnp.log1p(jnp.exp(-jnp.abs(z)))


def _ada_kernel(c_ref, w_ref, b_ref, o_ref):
    s = _silu(c_ref[...]).astype(BF16)
    o_ref[...] = jnp.dot(s, w_ref[...].astype(BF16), preferred_element_type=F32) + b_ref[...]


def _ada(c_rows, w_ada, b_ada):
    d, n_out = w_ada.shape
    tn = min(1024, n_out)
    return pl.pallas_call(
        _ada_kernel,
        grid=(n_out // tn,),
        in_specs=[pl.BlockSpec((ADA_ROWS, d), lambda j: (0, 0)),
                  pl.BlockSpec((d, tn), lambda j: (0, j)),
                  pl.BlockSpec((1, tn), lambda j: (0, j))],
        out_specs=pl.BlockSpec((ADA_ROWS, tn), lambda j: (0, j)),
        out_shape=jax.ShapeDtypeStruct((ADA_ROWS, n_out), F32),
        compiler_params=_cparams(("arbitrary",)),
    )(c_rows, w_ada, b_ada.reshape(1, n_out))


def _proj_kernel(x_ref, sh_ref, sc_ref, w_ref, wlr_ref, wg_ref, bg_ref, cos_ref, sin_ref,
                 of_ref, ob_ref, lg_ref, u_ref, *, rope, n_f32, n_q, n_rope):
    n = pl.program_id(1)

    @pl.when(n == 0)
    def _():
        u = (x_ref[...] * (1.0 + sc_ref[...]) + sh_ref[...]).astype(BF16)
        u_ref[...] = u
        lr = jnp.dot(u, wlr_ref[...], preferred_element_type=F32)
        z = jnp.dot(lr.astype(BF16), wg_ref[...], preferred_element_type=F32) + bg_ref[...]
        lg_ref[...] = _log_sigmoid(z) * (1.0 / GLA_TAU)

    acc = jnp.dot(u_ref[...], w_ref[...], preferred_element_type=F32)

    @pl.when(n < n_f32)
    def _():
        of_ref[...] = acc

    @pl.when(jnp.logical_and(n >= n_f32, n < n_f32 + n_rope))
    def _():
        scale = jnp.where(n < n_f32 + n_q, DIFF_DH ** -0.5, 1.0).astype(F32)
        if rope:
            lane = lax.broadcasted_iota(I32, (acc.shape[0], LANES), 1)
            first = (lane % DIFF_DH) < (DIFF_DH // 2)
            cs = cos_ref[...]
            sn = sin_ref[...]
            for j in range(acc.shape[1] // LANES):
                a = acc[:, j * LANES:(j + 1) * LANES]
                partner = jnp.where(first, pltpu.roll(a, LANES - DIFF_DH // 2, 1), pltpu.roll(a, DIFF_DH // 2, 1))
                ob_ref[:, j * LANES:(j + 1) * LANES] = ((a * cs + partner * sn) * scale).astype(BF16)
        else:
            ob_ref[...] = (acc * scale).astype(BF16)

    @pl.when(n >= n_f32 + n_rope)
    def _():
        ob_ref[...] = acc.astype(BF16)


def _proj(x2d, mod3, row_of_tile, w_main, w_lr, w_g, b_g, cos_t, sin_t, *, rope, tm, n_tab_tiles):
    m, d = x2d.shape
    n_main = w_main.shape[1]
    tn = 512
    n_tiles = n_main // tn
    n_f32 = n_tiles // 2
    n_q = n_tiles // 6
    n_rope = n_tiles // 3
    r2 = w_lr.shape[1]
    ng = w_g.shape[1]
    kern = functools.partial(_proj_kernel, rope=rope, n_f32=n_f32, n_q=n_q, n_rope=n_rope)
    return pl.pallas_call(
        kern,
        grid=(m // tm, n_tiles),
        in_specs=[pl.BlockSpec((tm, d), lambda i, j: (i, 0)),
                  pl.BlockSpec((None, 1, d), lambda i, j: (row_of_tile(i), 0, 0)),
                  pl.BlockSpec((None, 1, d), lambda i, j: (row_of_tile(i), 0, 1)),
                  pl.BlockSpec((d, tn), lambda i, j: (0, j)),
                  pl.BlockSpec((d, r2), lambda i, j: (0, 0)),
                  pl.BlockSpec((r2, ng), lambda i, j: (0, 0)),
                  pl.BlockSpec((1, ng), lambda i, j: (0, 0)),
                  pl.BlockSpec((tm, LANES), lambda i, j: (i % n_tab_tiles, 0)),
                  pl.BlockSpec((tm, LANES), lambda i, j: (i % n_tab_tiles, 0))],
        out_specs=[pl.BlockSpec((tm, tn), lambda i, j: (i, jnp.minimum(j, n_f32 - 1))),
                   pl.BlockSpec((tm, tn), lambda i, j: (i, jnp.maximum(j - n_f32, 0))),
                   pl.BlockSpec((tm, ng), lambda i, j: (i, 0))],
        out_shape=[jax.ShapeDtypeStruct((m, n_f32 * tn), F32),
                   jax.ShapeDtypeStruct((m, (n_tiles - n_f32) * tn), BF16),
                   jax.ShapeDtypeStruct((m, ng), F32)],
        scratch_shapes=[pltpu.VMEM((tm, d), BF16)],
        compiler_params=_cparams(("arbitrary", "arbitrary")),
    )(x2d, mod3, mod3, w_main, w_lr, w_g, b_g, cos_t, sin_t)


def _gla_chunk(q, k, v, g, st_ref, tri, causal, *, want_out, scale):
    bcum = jnp.dot(tri, g, preferred_element_type=F32, precision=lax.Precision.HIGHEST)
    btot = jnp.sum(g, axis=0, keepdims=True)
    st = st_ref[...]
    vb = v.astype(BF16)
    k_out = (k * jnp.exp(btot - bcum)).astype(BF16)
    out = None
    if want_out:
        q_in = ((q * scale) * jnp.exp(bcum)).astype(BF16)
        k_in = (k * jnp.exp(-bcum)).astype(BF16)
        att = lax.dot_general(q_in, k_in, _NT, preferred_element_type=F32)
        att = jnp.where(causal, att, 0.0).astype(BF16)
        out = (jnp.dot(att, vb, preferred_element_type=F32)
               + lax.dot_general(q_in, st.astype(BF16), _NT, preferred_element_type=F32))
    upd = lax.dot_general(vb, k_out, _TN, preferred_element_type=F32)
    st_ref[...] = st * jnp.exp(btot) + upd
    return out


def _gla_kernel(q_ref, k_ref, v_ref, r_ref, gf_ref, gb_ref, kc_ref, vc_ref, gfc_ref, gbc_ref, ng_ref,
                o_ref, of_ref, obk_ref, sf_ref, sb_ref, *, n_lat, n_ctx, dk):
    c = GLA_CHUNK
    row = lax.broadcasted_iota(I32, (c, c), 0)
    col = lax.broadcasted_iota(I32, (c, c), 1)
    lower = row >= col
    upper = row <= col
    tri_f = lower.astype(F32)
    tri_b = upper.astype(F32)
    scale = dk ** -0.5
    sf_ref[...] = jnp.zeros_like(sf_ref)
    sb_ref[...] = jnp.zeros_like(sb_ref)
    nc_ctx = n_ctx // c
    nc_lat = n_lat // c

    def ctx_body(i, carry):
        rf = pl.multiple_of(i * c, c)
        rb = pl.multiple_of((nc_ctx - 1 - i) * c, c)
        _gla_chunk(None, kc_ref[pl.ds(rf, c), :], vc_ref[pl.ds(rf, c), :], gfc_ref[pl.ds(rf, c), :],
                   sf_ref, tri_f, lower, want_out=False, scale=scale)
        _gla_chunk(None, kc_ref[pl.ds(rb, c), :], vc_ref[pl.ds(rb, c), :], gbc_ref[pl.ds(rb, c), :],
                   sb_ref, tri_b, upper, want_out=False, scale=scale)
        return carry

    lax.fori_loop(0, nc_ctx, ctx_body, 0)

    def lat_body(i, carry):
        rf = pl.multiple_of(i * c, c)
        rb = pl.multiple_of((nc_lat - 1 - i) * c, c)
        of_ref[pl.ds(rf, c), :] = _gla_chunk(
            q_ref[pl.ds(rf, c), :], k_ref[pl.ds(rf, c), :], v_ref[pl.ds(rf, c), :], gf_ref[pl.ds(rf, c), :],
            sf_ref, tri_f, lower, want_out=True, scale=scale)
        obk_ref[pl.ds(rb, c), :] = _gla_chunk(
            q_ref[pl.ds(rb, c), :], k_ref[pl.ds(rb, c), :], v_ref[pl.ds(rb, c), :], gb_ref[pl.ds(rb, c), :],
            sb_ref, tri_b, upper, want_out=True, scale=scale)
        return carry

    lax.fori_loop(0, nc_lat, lat_body, 0)

    tb = min(512, n_lat)

    def fin_body(i, carry):
        r0 = pl.multiple_of(i * tb, tb)
        o = of_ref[pl.ds(r0, tb), :] + obk_ref[pl.ds(r0, tb), :]
        ms = jnp.mean(o * o, axis=-1, keepdims=True)
        o = o * lax.rsqrt(ms + RMS_EPS) * ng_ref[...]
        o_ref[pl.ds(r0, tb), :] = (o * _silu(r_ref[pl.ds(r0, tb), :])).astype(o_ref.dtype)
        return carry

    lax.fori_loop(0, n_lat // tb, fin_body, 0)


def _gla(pf_l, lg_l, pf_c, lg_c, norm_g, *, batch, n_lat, n_ctx):
    h = GLA_HEADS
    dk = pf_l.shape[1] // 6 // h
    dv = 2 * dk
    kern = functools.partial(_gla_kernel, n_lat=n_lat, n_ctx=n_ctx, dk=dk)
    lat = lambda width, off: pl.BlockSpec((n_lat, width), lambda b, i: (b, off + i))
    ctx = lambda width, off: pl.BlockSpec((n_ctx, width), lambda b, i: (b, off + i))
    return pl.pallas_call(
        kern,
        grid=(batch, h),
        in_specs=[lat(dk, 0), lat(dk, h), lat(dv, h), lat(dv, 2 * h),
                  lat(dk, 0), lat(dk, h),
                  ctx(dk, h), ctx(dv, h), ctx(dk, 0), ctx(dk, h),
                  pl.BlockSpec((1, dv), lambda b, i: (0, 0))],
        out_specs=pl.BlockSpec((n_lat, dv), lambda b, i: (b, i)),
        out_shape=jax.ShapeDtypeStruct((batch * n_lat, h * dv), BF16),
        scratch_shapes=[pltpu.VMEM((n_lat, dv), F32), pltpu.VMEM((n_lat, dv), F32),
                        pltpu.VMEM((dv, dk), F32), pltpu.VMEM((dv, dk), F32)],
        compiler_params=_cparams(("arbitrary", "arbitrary")),
    )(pf_l, pf_l, pf_l, pf_l, lg_l, lg_l, pf_c, pf_c, lg_c, lg_c, norm_g.reshape(1, dv))


DIFF_KEY_CHUNK = 512


def _diff_step(q_ref, kc_ref, kl_ref, vc_ref, vl_ref, lam_ref, ng_ref, o_ref, s_w, m_w, s_r, m_r):
    tq = o_ref.shape[0]
    dv = o_ref.shape[1]
    q = q_ref[...]
    lane = lax.broadcasted_iota(I32, q.shape, 1)
    zero = jnp.zeros_like(q)
    qq = jnp.concatenate([jnp.where(lane < DIFF_DH, q, zero), jnp.where(lane >= DIFF_DH, q, zero)], axis=0)
    mx = jnp.max(m_r[...], axis=-1, keepdims=True)
    m_new = jnp.full((2 * tq, LANES), -jnp.inf, F32)
    acc = jnp.zeros((2 * tq, 2 * dv), F32)
    col = 0
    for k_ref, v_ref in ((kc_ref, vc_ref), (kl_ref, vl_ref)):
        n_keys = k_ref.shape[0]
        w = min(DIFF_KEY_CHUNK, n_keys)
        ones_col = (lax.broadcasted_iota(I32, (w, dv), 1) == 0).astype(BF16)
        for r0 in range(0, n_keys, w):
            s_new = lax.dot_general(qq, k_ref[r0:r0 + w, :], _NT, preferred_element_type=F32)
            s_w[:, col:col + w] = s_new
            for t in range(w // LANES):
                m_new = jnp.maximum(m_new, s_new[:, t * LANES:(t + 1) * LANES])
            e = jnp.exp(s_r[:, col:col + w] - mx).astype(BF16)
            v_ext = jnp.concatenate([v_ref[r0:r0 + w, :], ones_col], axis=1)
            acc = acc + jnp.dot(e, v_ext, preferred_element_type=F32)
            col += w
    m_w[...] = m_new
    dl = lam_ref[...]
    lam = (jnp.exp(jnp.sum(dl[0:1] * dl[1:2], axis=-1, keepdims=True))
           - jnp.exp(jnp.sum(dl[2:3] * dl[3:4], axis=-1, keepdims=True)) + LAM_INIT)
    inv = 1.0 / acc[:, dv:dv + 1]
    o = acc[:tq, :dv] * inv[:tq] - acc[tq:, :dv] * (inv[tq:] * lam)
    ms = jnp.mean(o * o, axis=-1, keepdims=True)
    o_ref[...] = (o * lax.rsqrt(ms + RMS_EPS) * ng_ref[...] * (1.0 - LAM_INIT)).astype(o_ref.dtype)


def _diff_kernel(q_ref, kl_ref, kc_ref, vl_ref, vc_ref, lam_ref, ng_ref, o_ref, sa_ref, sb_ref, ma_ref, mb_ref):
    j = pl.program_id(1)

    @pl.when(j == 0)
    def _():
        sb_ref[...] = jnp.zeros_like(sb_ref)
        mb_ref[...] = jnp.zeros_like(mb_ref)

    args = (q_ref, kc_ref, kl_ref, vc_ref, vl_ref, lam_ref, ng_ref, o_ref)

    @pl.when(j % 2 == 0)
    def _():
        _diff_step(*args, sa_ref, ma_ref, sb_ref, mb_ref)

    @pl.when(j % 2 == 1)
    def _():
        _diff_step(*args, sb_ref, mb_ref, sa_ref, ma_ref)


def _diff(pb_l, pb_c, diff_lambda, norm_g, *, batch, n_lat, n_ctx):
    h = DIFF_HEADS
    dv = 2 * DIFF_DH
    tq = min(256, n_lat)
    nq = n_lat // tq
    nblk = h * nq
    cur = lambda j: jnp.minimum(j, nblk - 1)
    prev = lambda j: jnp.maximum(j - 1, 0)
    return pl.pallas_call(
        _diff_kernel,
        grid=(batch, nblk + 1),
        in_specs=[pl.BlockSpec((tq, dv), lambda b, j: (b * nq + cur(j) % nq, cur(j) // nq)),
                  pl.BlockSpec((n_lat, dv), lambda b, j: (b, h + cur(j) // nq)),
                  pl.BlockSpec((n_ctx, dv), lambda b, j: (b, h + cur(j) // nq)),
                  pl.BlockSpec((n_lat, dv), lambda b, j: (b, 2 * h + prev(j) // nq)),
                  pl.BlockSpec((n_ctx, dv), lambda b, j: (b, 2 * h + prev(j) // nq)),
                  pl.BlockSpec(diff_lambda.shape, lambda b, j: (0, 0)),
                  pl.BlockSpec((1, dv), lambda b, j: (0, 0))],
        out_specs=pl.BlockSpec((tq, dv), lambda b, j: (b * nq + prev(j) % nq, prev(j) // nq)),
        out_shape=jax.ShapeDtypeStruct((batch * n_lat, h * dv), BF16),
        scratch_shapes=[pltpu.VMEM((2 * tq, n_ctx + n_lat), F32), pltpu.VMEM((2 * tq, n_ctx + n_lat), F32),
                        pltpu.VMEM((2 * tq, LANES), F32), pltpu.VMEM((2 * tq, LANES), F32)],
        compiler_params=_cparams(("arbitrary", "arbitrary")),
    )(pb_l, pb_l, pb_c, pb_l, pb_c, diff_lambda, norm_g.reshape(1, dv))


def _layer_norm(y, g, b):
    mu = jnp.mean(y, axis=-1, keepdims=True)
    yc = y - mu
    var = jnp.mean(yc * yc, axis=-1, keepdims=True)
    return yc * lax.rsqrt(var + LN_EPS) * g + b


def _out_kernel(ga_ref, df_ref, x_ref, wo_ref, g1_ref, sh2_ref, sc2_ref, lg_ref, lb_ref, wr_ref,
                x1_ref, h_ref, aff_ref):
    half = ga_ref.shape[1]
    o = (jnp.dot(ga_ref[...], wo_ref[:half, :], preferred_element_type=F32)
         + jnp.dot(df_ref[...], wo_ref[half:, :], preferred_element_type=F32))
    x1 = _layer_norm(ALPHA * x_ref[...] + g1_ref[...] * o, lg_ref[...], lb_ref[...])
    x1_ref[...] = x1
    hmod = x1 * (1.0 + sc2_ref[...]) + sh2_ref[...]
    hbits = pltpu.bitcast(hmod.astype(BF16).astype(F32), I32)
    hw = h_ref.shape[1]
    h_ref[...] = (hbits[:, hw:] & jnp.int32(-65536)) | lax.shift_right_logical(hbits[:, :hw], 16)
    logits = lax.dot_general(wr_ref[...], hmod, _NT, preferred_element_type=F32,
                             precision=lax.Precision.HIGHEST)
    e = jnp.exp(logits - jnp.max(logits, axis=0, keepdims=True))
    aff_ref[...] = e / jnp.sum(e, axis=0, keepdims=True)


def _out(gla_o, diff_o, x2d, w_o, mod3, ln_g, ln_b, w_router_t, *, batch, n_lat):
    m, d = x2d.shape
    half = gla_o.shape[1]
    n_e = w_router_t.shape[0]
    tm = min(256, n_lat)
    tpb = n_lat // tm
    modspec = lambda k: pl.BlockSpec((None, 1, d), lambda i: (i // tpb, 0, k))
    vec = pl.BlockSpec((1, d), lambda i: (0, 0))
    return pl.pallas_call(
        _out_kernel,
        grid=(m // tm,),
        in_specs=[pl.BlockSpec((tm, half), lambda i: (i, 0)),
                  pl.BlockSpec((tm, half), lambda i: (i, 0)),
                  pl.BlockSpec((tm, d), lambda i: (i, 0)),
                  pl.BlockSpec((2 * half, d), lambda i: (0, 0)),
                  modspec(2), modspec(3), modspec(4), vec, vec,
                  pl.BlockSpec((n_e, d), lambda i: (0, 0))],
        out_specs=[pl.BlockSpec((tm, d), lambda i: (i, 0)),
                   pl.BlockSpec((tm, d // 2), lambda i: (i, 0)),
                   pl.BlockSpec((None, n_e, tm), lambda i: (i // tpb, 0, i % tpb))],
        out_shape=[jax.ShapeDtypeStruct((m, d), F32),
                   jax.ShapeDtypeStruct((m, d // 2), I32),
                   jax.ShapeDtypeStruct((batch, n_e, n_lat), F32)],
        compiler_params=_cparams(("arbitrary",)),
    )(gla_o, diff_o, x2d, w_o, mod3, mod3, mod3, ln_g.reshape(1, d), ln_b.reshape(1, d), w_router_t)


def _prefix_count(mask_f32, strict_upper):
    rows, n = mask_f32.shape
    carry = jnp.zeros((rows, 1), F32)
    pieces = []
    for j in range(n // LANES):
        blk = mask_f32[:, j * LANES:(j + 1) * LANES]
        within = jnp.dot(blk.astype(BF16), strict_upper, preferred_element_type=F32)
        pieces.append(within + carry)
        carry = carry + jnp.sum(blk, axis=-1, keepdims=True)
    return jnp.concatenate(pieces, axis=-1)


def _route_kernel(aff_ref, idx_ref, gate_ref, pos_ref, *, cap):
    b = pl.program_id(0)
    v = aff_ref[...]
    n_e, n = v.shape
    bits = pltpu.bitcast(v, I32)

    def search(i, t):
        cand = t | (1 << (30 - i))
        cnt = jnp.sum((bits >= cand).astype(I32), axis=-1, keepdims=True)
        return jnp.where(cnt >= cap, cand, t)

    thr = lax.fori_loop(0, 31, search, jnp.zeros((n_e, 1), I32))
    gt = bits > thr
    eq = bits == thr
    need = (cap - jnp.sum(gt.astype(I32), axis=-1, keepdims=True)).astype(F32)
    r = lax.broadcasted_iota(I32, (LANES, LANES), 0)
    cidx = lax.broadcasted_iota(I32, (LANES, LANES), 1)
    strict_upper = (r < cidx).astype(BF16)
    eq_rank = _prefix_count(eq.astype(F32), strict_upper)
    sel = jnp.logical_or(gt, jnp.logical_and(eq, eq_rank < need))
    pos = _prefix_count(sel.astype(F32), strict_upper)
    pos_ref[...] = jnp.where(sel, pos, -1.0).astype(I32)

    tok = lax.broadcasted_iota(I32, (8, n), 1)
    srow = lax.broadcasted_iota(I32, (8, n), 0)
    tok_hi = (tok >> 6).astype(F32)
    tok_lo = (tok & 63).astype(F32)
    slot = lax.broadcasted_iota(I32, (cap, n), 0)

    def compact(e, carry):
        ve = aff_ref[pl.ds(e, 1), :]
        v_hi = ve.astype(BF16).astype(F32)
        r1 = ve - v_hi
        v_mid = r1.astype(BF16).astype(F32)
        v_lo = r1 - v_mid
        lhs = jnp.where(srow == 0, tok_hi,
              jnp.where(srow == 1, tok_lo,
              jnp.where(srow == 2, v_hi,
              jnp.where(srow == 3, v_mid,
              jnp.where(srow == 4, v_lo, 0.0))))).astype(BF16)
        onehot = (slot == pos_ref[pl.ds(e, 1), :]).astype(F32).astype(BF16)
        res = lax.dot_general(lhs, onehot, _NT, preferred_element_type=F32)
        idx_ref[pl.ds(e, 1), :] = (res[0:1] * 64.0 + res[1:2]).astype(I32) + b * n
        gate_ref[pl.ds(e, 1), :] = (res[2:3] + res[3:4]) + res[4:5]
        return carry

    lax.fori_loop(0, n_e, compact, 0)


def _route(aff_t, *, cap):
    batch, n_e, n = aff_t.shape
    kern = functools.partial(_route_kernel, cap=cap)
    return pl.pallas_call(
        kern,
        grid=(batch,),
        in_specs=[pl.BlockSpec((None, n_e, n), lambda b: (b, 0, 0))],
        out_specs=[pl.BlockSpec((None, n_e, cap), lambda b: (b, 0, 0)),
                   pl.BlockSpec((None, n_e, cap), lambda b: (b, 0, 0))],
        out_shape=[jax.ShapeDtypeStruct((batch, n_e, cap), I32),
                   jax.ShapeDtypeStruct((batch, n_e, cap), F32)],
        scratch_shapes=[pltpu.VMEM((n_e, n), I32)],
        compiler_params=_cparams(("arbitrary",)),
    )(aff_t)


DMA_UNROLL = 8


def _moe_kernel(idx_ref, gate_ref, h_hbm, w1_ref, w3_ref, w2_ref, facc_in, facc_hbm,
                stage_ref, acc_ref, gsem, rsem, wsem, *, rows):
    del facc_in
    e = pl.program_id(0)
    f = pl.program_id(1)
    n_e = pl.num_programs(0)
    nf = pl.num_programs(1)
    slot = e % 2

    def issue_rows(make):
        def body(i, carry):
            for u in range(DMA_UNROLL):
                make(i * DMA_UNROLL + u).start()
            return carry
        lax.fori_loop(0, rows // DMA_UNROLL, body, 0)

    def x_row(ex, s):
        return lambda k: pltpu.make_async_copy(
            h_hbm.at[pl.ds(idx_ref[ex * rows + k], 1), :], stage_ref.at[s, pl.ds(k, 1), :], gsem.at[s])

    def acc_row_in(k):
        return pltpu.make_async_copy(facc_hbm.at[pl.ds(idx_ref[e * rows + k], 1), :], acc_ref.at[pl.ds(k, 1), :], rsem)

    def acc_row_out(k):
        return pltpu.make_async_copy(acc_ref.at[pl.ds(k, 1), :], facc_hbm.at[pl.ds(idx_ref[e * rows + k], 1), :], wsem)

    def wait_x(s):
        pltpu.make_async_copy(h_hbm.at[pl.ds(0, rows), :], stage_ref.at[s], gsem.at[s]).wait()

    def wait_acc_in():
        pltpu.make_async_copy(facc_hbm.at[pl.ds(0, rows), :], acc_ref, rsem).wait()

    def wait_acc_out():
        pltpu.make_async_copy(acc_ref, facc_hbm.at[pl.ds(0, rows), :], wsem).wait()

    @pl.when(f == 0)
    def _():
        @pl.when(e > 0)
        def _():
            wait_acc_out()
        issue_rows(acc_row_in)

        @pl.when(e == 0)
        def _():
            issue_rows(x_row(0, 0))
        wait_x(slot)

        @pl.when(e + 1 < n_e)
        def _():
            issue_rows(x_row(e + 1, 1 - slot))
        wait_acc_in()

    u = stage_ref[slot]
    half = u.shape[1]
    x_lo = pltpu.bitcast(u << 16, F32).astype(BF16)
    x_hi = pltpu.bitcast(u & jnp.int32(-65536), F32).astype(BF16)
    tf = w1_ref.shape[1]
    w13 = jnp.concatenate([w1_ref[...], w3_ref[...]], axis=1).astype(BF16)
    ag = (jnp.dot(x_lo, w13[:half], preferred_element_type=F32)
          + jnp.dot(x_hi, w13[half:], preferred_element_type=F32))
    hid = (_silu(ag[:, :tf]) * ag[:, tf:]).astype(BF16)
    gate = gate_ref[...]
    d = acc_ref.shape[1]
    tc = min(512, d)
    for j in range(d // tc):
        y = jnp.dot(hid, w2_ref[:, j * tc:(j + 1) * tc].astype(BF16), preferred_element_type=F32)
        acc_ref[:, j * tc:(j + 1) * tc] += gate * y

    @pl.when(f == nf - 1)
    def _():
        issue_rows(acc_row_out)

        @pl.when(e == n_e - 1)
        def _():
            wait_acc_out()


def _moe(idx_flat, gate_col, h_packed, w1, w3, w2, facc0, *, rows):
    m, d = facc0.shape
    n_e, _, dff = w1.shape
    tf = min(256, dff)
    kern = functools.partial(_moe_kernel, rows=rows)
    grid_spec = pltpu.PrefetchScalarGridSpec(
        num_scalar_prefetch=1,
        grid=(n_e, dff // tf),
        in_specs=[pl.BlockSpec((None, rows, 1), lambda e, f, idx: (e, 0, 0)),
                  pl.BlockSpec(memory_space=pl.ANY),
                  pl.BlockSpec((None, d, tf), lambda e, f, idx: (e, 0, f)),
                  pl.BlockSpec((None, d, tf), lambda e, f, idx: (e, 0, f)),
                  pl.BlockSpec((None, tf, d), lambda e, f, idx: (e, f, 0)),
                  pl.BlockSpec(memory_space=pl.ANY)],
        out_specs=pl.BlockSpec(memory_space=pl.ANY),
        scratch_shapes=[pltpu.VMEM((2, rows, d // 2), I32), pltpu.VMEM((rows, d), F32),
                        pltpu.SemaphoreType.DMA((2,)), pltpu.SemaphoreType.DMA(()), pltpu.SemaphoreType.DMA(())],
    )
    return pl.pallas_call(
        kern,
        grid_spec=grid_spec,
        out_shape=jax.ShapeDtypeStruct((m, d), F32),
        input_output_aliases={6: 0},
        compiler_params=_cparams(("arbitrary", "arbitrary")),
    )(idx_flat, gate_col, h_packed, w1, w3, w2, facc0)


def _final_kernel(x1_ref, f_ref, g2_ref, lg_ref, lb_ref, o_ref):
    o_ref[...] = _layer_norm(ALPHA * x1_ref[...] + g2_ref[...] * f_ref[...], lg_ref[...], lb_ref[...])


def _final(x1, facc, mod3, ln_g, ln_b, *, n_lat):
    m, d = x1.shape
    tm = min(512, n_lat)
    tpb = n_lat // tm
    vec = pl.BlockSpec((1, d), lambda i: (0, 0))
    return pl.pallas_call(
        _final_kernel,
        grid=(m // tm,),
        in_specs=[pl.BlockSpec((tm, d), lambda i: (i, 0)),
                  pl.BlockSpec((tm, d), lambda i: (i, 0)),
                  pl.BlockSpec((None, 1, d), lambda i: (i // tpb, 0, 5)),
                  vec, vec],
        out_specs=pl.BlockSpec((tm, d), lambda i: (i, 0)),
        out_shape=jax.ShapeDtypeStruct((m, d), F32),
        compiler_params=_cparams(("arbitrary",)),
    )(x1, facc, mod3, ln_g.reshape(1, d), ln_b.reshape(1, d))


def _rope_tables(n_lat):
    rows = n_lat // GRID_W
    row = jnp.repeat(jnp.arange(rows, dtype=F32), GRID_W)
    col = jnp.tile(jnp.arange(GRID_W, dtype=F32), rows)
    n_freq = DIFF_DH // 4
    inv = ROPE_BASE ** (-jnp.arange(n_freq, dtype=F32) / n_freq)
    ang = jnp.concatenate([row[:, None] * inv, col[:, None] * inv], axis=-1)
    cos, sin = jnp.cos(ang), jnp.sin(ang)
    reps = LANES // DIFF_DH
    cos_t = jnp.tile(jnp.concatenate([cos, cos], axis=-1), (1, reps))
    sin_t = jnp.tile(jnp.concatenate([-sin, sin], axis=-1), (1, reps))
    return cos_t, sin_t


def kernel(x, c, ctx, c_ctx, w_ada, b_ada, w_in, w_gate2, b_gate, gla_norm_g, diff_lambda, diff_norm_g, w_o,
           ln1_g, ln1_b, w_router, w1, w3, w2, ln2_g, ln2_b):
    batch, n_lat, d = x.shape
    n_ctx = ctx.shape[1]
    assert w_ada.shape[0] == DEPTH == 1 and batch < ADA_ROWS
    n_e = w_router.shape[-1]
    cap = EC_CAPACITY_FACTOR * n_lat // n_e
    l = 0

    c_rows = jnp.concatenate([c, c_ctx[None, :], jnp.zeros((ADA_ROWS - batch - 1, d), F32)], axis=0)
    mod3 = _ada(c_rows, w_ada[l], b_ada[l]).reshape(ADA_ROWS, 1, 6 * d)

    gqkvr = 2 * (d // 4) + 2 * (d // 2)
    r2 = 2 * GLA_RANK
    w_main = jnp.concatenate([w_in[l][:, :gqkvr], w_in[l][:, gqkvr + r2:]], axis=1).astype(BF16)
    w_lr = w_in[l][:, gqkvr:gqkvr + r2].astype(BF16)
    hk = w_gate2.shape[-1]
    zeros = jnp.zeros((GLA_RANK, hk), F32)
    w_g = jnp.concatenate([jnp.concatenate([w_gate2[l, 0], zeros], axis=1),
                           jnp.concatenate([zeros, w_gate2[l, 1]], axis=1)], axis=0).astype(BF16)
    b_g = b_gate[l].reshape(1, 2 * hk)
    cos_t, sin_t = _rope_tables(n_lat)

    tm_l = min(1024, n_lat)
    tpb = n_lat // tm_l
    x2d = x.reshape(batch * n_lat, d)
    pf_l, pb_l, lg_l = _proj(x2d, mod3, lambda i: i // tpb, w_main, w_lr, w_g, b_g, cos_t, sin_t,
                             rope=True, tm=tm_l, n_tab_tiles=tpb)
    ctx2d = ctx.reshape(batch * n_ctx, d)
    tm_c = min(1024, batch * n_ctx)
    pf_c, pb_c, lg_c = _proj(ctx2d, mod3, lambda i: batch, w_main, w_lr, w_g, b_g, cos_t, sin_t,
                             rope=False, tm=tm_c, n_tab_tiles=1)

    gla_o = _gla(pf_l, lg_l, pf_c, lg_c, gla_norm_g[l], batch=batch, n_lat=n_lat, n_ctx=n_ctx)
    diff_o = _diff(pb_l, pb_c, diff_lambda[l], diff_norm_g[l], batch=batch, n_lat=n_lat, n_ctx=n_ctx)

    x1, h_packed, aff_t = _out(gla_o, diff_o, x2d, w_o[l].astype(BF16), mod3, ln1_g[l], ln1_b[l],
                           w_router[l].T, batch=batch, n_lat=n_lat)

    idx, gate = _route(aff_t, cap=cap)
    rows = batch * cap
    idx_flat = idx.transpose(1, 0, 2).reshape(n_e * rows)
    gate_col = gate.transpose(1, 0, 2).reshape(n_e, rows, 1)
    facc = _moe(idx_flat, gate_col, h_packed, w1[l], w3[l], w2[l], jnp.zeros_like(x1), rows=rows)

    out = _final(x1, facc, mod3, ln2_g[l], ln2_b[l], n_lat=n_lat)
    return out.reshape(batch, n_lat, d)
```

```python
import functools
import math

import jax
import jax.numpy as jnp
import numpy as np
from jax import lax
from jax.experimental import pallas as pl
from jax.experimental.pallas import tpu as pltpu

F32 = jnp.float32
BF16 = jnp.bfloat16
I32 = jnp.int32

GRID_W = 64
GLA_HEADS = 4
GLA_RANK = 16
GLA_TAU = 16.0
GLA_CHUNK = 64
DIFF_HEADS = 8
DIFF_DH = 64
ROPE_BASE = 10000.0
EC_CAPACITY_FACTOR = 2
LN_EPS = 1e-5
RMS_EPS = 1e-6
DEPTH = 1
ALPHA = (2.0 * DEPTH) ** 0.25
LAM_INIT = 0.8 - 0.6 * math.exp(-0.3 * 0)

LANES = 128
ADA_ROWS = 8
VMEM_LIMIT = 56 * 1024 * 1024

_NT = (((1,), (1,)), ((), ()))
_TN = (((0,), (0,)), ((), ()))


def _cparams(sem):
    return pltpu.CompilerParams(dimension_semantics=sem, vmem_limit_bytes=VMEM_LIMIT)


def _silu(a):
    return a * (1.0 / (1.0 + jnp.exp(-a)))


def _log_sigmoid(z):
    return jnp.minimum(z, 0.0) - jnp.log1p(jnp.exp(-jnp.abs(z)))


def _ada_kernel(c_ref, w_ref, b_ref, o_ref):
    s = _silu(c_ref[...]).astype(BF16)
    o_ref[...] = jnp.dot(s, w_ref[...].astype(BF16), preferred_element_type=F32) + b_ref[...]


def _ada(c_rows, w_ada, b_ada):
    d, n_out = w_ada.shape
    tn = min(1024, n_out)
    return pl.pallas_call(
        _ada_kernel,
        grid=(n_out // tn,),
        in_specs=[pl.BlockSpec((ADA_ROWS, d), lambda j: (0, 0)),
                  pl.BlockSpec((d, tn), lambda j: (0, j)),
                  pl.BlockSpec((1, tn), lambda j: (0, j))],
        out_specs=pl.BlockSpec((ADA_ROWS, tn), lambda j: (0, j)),
        out_shape=jax.ShapeDtypeStruct((ADA_ROWS, n_out), F32),
        compiler_params=_cparams(("arbitrary",)),
    )(c_rows, w_ada, b_ada.reshape(1, n_out))


def _proj_kernel(x_ref, sh_ref, sc_ref, w_ref, wlr_ref, wg_ref, bg_ref, cos_ref, sin_ref,
                 of_ref, ob_ref, lg_ref, u_ref, *, rope, n_f32, n_q, n_rope):
    n = pl.program_id(1)

    @pl.when(n == 0)
    def _():
        u = (x_ref[...] * (1.0 + sc_ref[...]) + sh_ref[...]).astype(BF16)
        u_ref[...] = u
        lr = jnp.dot(u, wlr_ref[...], preferred_element_type=F32)
        z = jnp.dot(lr.astype(BF16), wg_ref[...], preferred_element_type=F32) + bg_ref[...]
        lg_ref[...] = _log_sigmoid(z) * (1.0 / GLA_TAU)

    acc = jnp.dot(u_ref[...], w_ref[...], preferred_element_type=F32)

    @pl.when(n < n_f32)
    def _():
        of_ref[...] = acc

    @pl.when(jnp.logical_and(n >= n_f32, n < n_f32 + n_rope))
    def _():
        scale = jnp.where(n < n_f32 + n_q, DIFF_DH ** -0.5, 1.0).astype(F32)
        if rope:
            lane = lax.broadcasted_iota(I32, (acc.shape[0], LANES), 1)
            first = (lane % DIFF_DH) < (DIFF_DH // 2)
            cs = cos_ref[...]
            sn = sin_ref[...]
            for j in range(acc.shape[1] // LANES):
                a = acc[:, j * LANES:(j + 1) * LANES]
                partner = jnp.where(first, pltpu.roll(a, LANES - DIFF_DH // 2, 1), pltpu.roll(a, DIFF_DH // 2, 1))
                ob_ref[:, j * LANES:(j + 1) * LANES] = ((a * cs + partner * sn) * scale).astype(BF16)
        else:
            ob_ref[...] = (acc * scale).astype(BF16)

    @pl.when(n >= n_f32 + n_rope)
    def _():
        ob_ref[...] = acc.astype(BF16)


def _proj(x2d, mod3, row_of_tile, w_main, w_lr, w_g, b_g, cos_t, sin_t, *, rope, tm, n_tab_tiles):
    m, d = x2d.shape
    n_main = w_main.shape[1]
    tn = 512
    n_tiles = n_main // tn
    n_f32 = n_tiles // 2
    n_q = n_tiles // 6
    n_rope = n_tiles // 3
    r2 = w_lr.shape[1]
    ng = w_g.shape[1]
    kern = functools.partial(_proj_kernel, rope=rope, n_f32=n_f32, n_q=n_q, n_rope=n_rope)
    return pl.pallas_call(
        kern,
        grid=(m // tm, n_tiles),
        in_specs=[pl.BlockSpec((tm, d), lambda i, j: (i, 0)),
                  pl.BlockSpec((None, 1, d), lambda i, j: (row_of_tile(i), 0, 0)),
                  pl.BlockSpec((None, 1, d), lambda i, j: (row_of_tile(i), 0, 1)),
                  pl.BlockSpec((d, tn), lambda i, j: (0, j)),
                  pl.BlockSpec((d, r2), lambda i, j: (0, 0)),
                  pl.BlockSpec((r2, ng), lambda i, j: (0, 0)),
                  pl.BlockSpec((1, ng), lambda i, j: (0, 0)),
                  pl.BlockSpec((tm, LANES), lambda i, j: (i % n_tab_tiles, 0)),
                  pl.BlockSpec((tm, LANES), lambda i, j: (i % n_tab_tiles, 0))],
        out_specs=[pl.BlockSpec((tm, tn), lambda i, j: (i, jnp.minimum(j, n_f32 - 1))),
                   pl.BlockSpec((tm, tn), lambda i, j: (i, jnp.maximum(j - n_f32, 0))),
                   pl.BlockSpec((tm, ng), lambda i, j: (i, 0))],
        out_shape=[jax.ShapeDtypeStruct((m, n_f32 * tn), F32),
                   jax.ShapeDtypeStruct((m, (n_tiles - n_f32) * tn), BF16),
                   jax.ShapeDtypeStruct((m, ng), F32)],
        scratch_shapes=[pltpu.VMEM((tm, d), BF16)],
        compiler_params=_cparams(("arbitrary", "arbitrary")),
    )(x2d, mod3, mod3, w_main, w_lr, w_g, b_g, cos_t, sin_t)


GLA_GROUP = 4


def _split3(a):
    hi = a.astype(BF16)
    r1 = a - hi.astype(F32)
    mid = r1.astype(BF16)
    return hi, mid, (r1 - mid.astype(F32)).astype(BF16)


def _gla_group(q, k, v, g, st_ref, sum_mat, causal, *, forward, want_out, scale):
    c = GLA_CHUNK
    rows, dk = k.shape
    n_chunks = rows // c
    parts = jnp.dot(sum_mat, jnp.concatenate(_split3(g), axis=1), preferred_element_type=F32)
    sums = (parts[:, :dk] + parts[:, dk:2 * dk]) + parts[:, 2 * dk:]
    bcum = sums[:rows]
    btot = sums[rows:]
    vb = v.astype(BF16)
    k_out = (k * jnp.exp(btot - bcum)).astype(BF16)
    decay = jnp.exp(btot)
    if want_out:
        q_in = ((q * scale) * jnp.exp(bcum)).astype(BF16)
        k_in = (k * jnp.exp(-bcum)).astype(BF16)
        att = lax.dot_general(q_in, k_in, _NT, preferred_element_type=F32)
        att = jnp.where(causal, att, 0.0).astype(BF16)
        intra = jnp.dot(att, vb, preferred_element_type=F32)
    st = st_ref[...]
    outs = [None] * n_chunks
    for ci in (range(n_chunks) if forward else reversed(range(n_chunks))):
        sl = slice(ci * c, (ci + 1) * c)
        if want_out:
            outs[ci] = intra[sl] + lax.dot_general(q_in[sl], st.astype(BF16), _NT, preferred_element_type=F32)
        st = (st * decay[ci * c:ci * c + 1]
              + lax.dot_general(vb[sl], k_out[sl], _TN, preferred_element_type=F32))
    st_ref[...] = st
    return jnp.concatenate(outs, axis=0) if want_out else None


def _gla_kernel(q_ref, k_ref, v_ref, r_ref, gf_ref, gb_ref, kc_ref, vc_ref, gfc_ref, gbc_ref, ng_ref,
                o_ref, of_ref, obk_ref, sf_ref, sb_ref, *, n_lat, n_ctx, dk):
    c = GLA_CHUNK
    scale = dk ** -0.5
    sf_ref[...] = jnp.zeros_like(sf_ref)
    sb_ref[...] = jnp.zeros_like(sb_ref)

    def masks(rows):
        row = lax.broadcasted_iota(I32, (rows, rows), 0)
        col = lax.broadcasted_iota(I32, (rows, rows), 1)
        same = (row // c) == (col // c)
        lower = jnp.logical_and(same, row >= col)
        upper = jnp.logical_and(same, row <= col)
        total = same.astype(BF16)
        return (lower, jnp.concatenate([lower.astype(BF16), total], axis=0),
                upper, jnp.concatenate([upper.astype(BF16), total], axis=0))

    def scan(n, refs_f, refs_b, want_out):
        rows = min(GLA_GROUP * c, n)
        lower, sum_f, upper, sum_b = masks(rows)
        n_groups = n // rows

        def body(i, carry):
            rf = pl.multiple_of(i * rows, rows)
            rb = pl.multiple_of((n_groups - 1 - i) * rows, rows)
            ld = lambda ref, r0: None if ref is None else ref[pl.ds(r0, rows), :]
            out_f = _gla_group(*(ld(r, rf) for r in refs_f), sf_ref, sum_f, lower,
                               forward=True, want_out=want_out, scale=scale)
            out_b = _gla_group(*(ld(r, rb) for r in refs_b), sb_ref, sum_b, upper,
                               forward=False, want_out=want_out, scale=scale)
            if want_out:
                of_ref[pl.ds(rf, rows), :] = out_f
                obk_ref[pl.ds(rb, rows), :] = out_b
            return carry

        lax.fori_loop(0, n_groups, body, 0)

    scan(n_ctx, (None, kc_ref, vc_ref, gfc_ref), (None, kc_ref, vc_ref, gbc_ref), False)
    scan(n_lat, (q_ref, k_ref, v_ref, gf_ref), (q_ref, k_ref, v_ref, gb_ref), True)

    tb = min(512, n_lat)

    def fin_body(i, carry):
        r0 = pl.multiple_of(i * tb, tb)
        o = of_ref[pl.ds(r0, tb), :] + obk_ref[pl.ds(r0, tb), :]
        ms = jnp.mean(o * o, axis=-1, keepdims=True)
        o = o * lax.rsqrt(ms + RMS_EPS) * ng_ref[...]
        o_ref[pl.ds(r0, tb), :] = (o * _silu(r_ref[pl.ds(r0, tb), :])).astype(o_ref.dtype)
        return carry

    lax.fori_loop(0, n_lat // tb, fin_body, 0)


def _gla(pf_l, lg_l, pf_c, lg_c, norm_g, *, batch, n_lat, n_ctx):
    h = GLA_HEADS
    dk = pf_l.shape[1] // 6 // h
    dv = 2 * dk
    kern = functools.partial(_gla_kernel, n_lat=n_lat, n_ctx=n_ctx, dk=dk)
    lat = lambda width, off: pl.BlockSpec((n_lat, width), lambda b, i: (b, off + i))
    ctx = lambda width, off: pl.BlockSpec((n_ctx, width), lambda b, i: (b, off + i))
    return pl.pallas_call(
        kern,
        grid=(batch, h),
        in_specs=[lat(dk, 0), lat(dk, h), lat(dv, h), lat(dv, 2 * h),
                  lat(dk, 0), lat(dk, h),
                  ctx(dk, h), ctx(dv, h), ctx(dk, 0), ctx(dk, h),
                  pl.BlockSpec((1, dv), lambda b, i: (0, 0))],
        out_specs=pl.BlockSpec((n_lat, dv), lambda b, i: (b, i)),
        out_shape=jax.ShapeDtypeStruct((batch * n_lat, h * dv), BF16),
        scratch_shapes=[pltpu.VMEM((n_lat, dv), F32), pltpu.VMEM((n_lat, dv), F32),
                        pltpu.VMEM((dv, dk), F32), pltpu.VMEM((dv, dk), F32)],
        compiler_params=_cparams(("arbitrary", "arbitrary")),
    )(pf_l, pf_l, pf_l, pf_l, lg_l, lg_l, pf_c, pf_c, lg_c, lg_c, norm_g.reshape(1, dv))


DIFF_KEY_CHUNK = 512


def _diff_step(q_ref, kc_ref, kl_ref, vc_ref, vl_ref, lam_ref, ng_ref, o_ref, s_w, m_w, s_r, m_r):
    tq = o_ref.shape[0]
    dv = o_ref.shape[1]
    q = q_ref[...]
    lane = lax.broadcasted_iota(I32, q.shape, 1)
    zero = jnp.zeros_like(q)
    qq = jnp.concatenate([jnp.where(lane < DIFF_DH, q, zero), jnp.where(lane >= DIFF_DH, q, zero)], axis=0)
    mx = jnp.max(m_r[...], axis=-1, keepdims=True)
    m_new = jnp.full((2 * tq, LANES), -jnp.inf, F32)
    acc = jnp.zeros((2 * tq, 2 * dv), F32)
    col = 0
    for k_ref, v_ref in ((kc_ref, vc_ref), (kl_ref, vl_ref)):
        n_keys = k_ref.shape[0]
        w = min(DIFF_KEY_CHUNK, n_keys)
        ones_col = (lax.broadcasted_iota(I32, (w, dv), 1) == 0).astype(BF16)
        for r0 in range(0, n_keys, w):
            s_new = lax.dot_general(qq, k_ref[r0:r0 + w, :], _NT, preferred_element_type=F32)
            s_w[:, col:col + w] = s_new
            for t in range(w // LANES):
                m_new = jnp.maximum(m_new, s_new[:, t * LANES:(t + 1) * LANES])
            e = jnp.exp(s_r[:, col:col + w] - mx).astype(BF16)
            v_ext = jnp.concatenate([v_ref[r0:r0 + w, :], ones_col], axis=1)
            acc = acc + jnp.dot(e, v_ext, preferred_element_type=F32)
            col += w
    m_w[...] = m_new
    dl = lam_ref[...]
    lam = (jnp.exp(jnp.sum(dl[0:1] * dl[1:2], axis=-1, keepdims=True))
           - jnp.exp(jnp.sum(dl[2:3] * dl[3:4], axis=-1, keepdims=True)) + LAM_INIT)
    inv = 1.0 / acc[:, dv:dv + 1]
    o = acc[:tq, :dv] * inv[:tq] - acc[tq:, :dv] * (inv[tq:] * lam)
    ms = jnp.mean(o * o, axis=-1, keepdims=True)
    o_ref[...] = (o * lax.rsqrt(ms + RMS_EPS) * ng_ref[...] * (1.0 - LAM_INIT)).astype(o_ref.dtype)


def _diff_kernel(q_ref, kl_ref, kc_ref, vl_ref, vc_ref, lam_ref, ng_ref, o_ref, sa_ref, sb_ref, ma_ref, mb_ref):
    j = pl.program_id(1)

    @pl.when(j == 0)
    def _():
        sb_ref[...] = jnp.zeros_like(sb_ref)
        mb_ref[...] = jnp.zeros_like(mb_ref)

    args = (q_ref, kc_ref, kl_ref, vc_ref, vl_ref, lam_ref, ng_ref, o_ref)

    @pl.when(j % 2 == 0)
    def _():
        _diff_step(*args, sa_ref, ma_ref, sb_ref, mb_ref)

    @pl.when(j % 2 == 1)
    def _():
        _diff_step(*args, sb_ref, mb_ref, sa_ref, ma_ref)


def _diff(pb_l, pb_c, diff_lambda, norm_g, *, batch, n_lat, n_ctx):
    h = DIFF_HEADS
    dv = 2 * DIFF_DH
    tq = min(256, n_lat)
    nq = n_lat // tq
    nblk = h * nq
    cur = lambda j: jnp.minimum(j, nblk - 1)
    prev = lambda j: jnp.maximum(j - 1, 0)
    return pl.pallas_call(
        _diff_kernel,
        grid=(batch, nblk + 1),
        in_specs=[pl.BlockSpec((tq, dv), lambda b, j: (b * nq + cur(j) % nq, cur(j) // nq)),
                  pl.BlockSpec((n_lat, dv), lambda b, j: (b, h + cur(j) // nq)),
                  pl.BlockSpec((n_ctx, dv), lambda b, j: (b, h + cur(j) // nq)),
                  pl.BlockSpec((n_lat, dv), lambda b, j: (b, 2 * h + prev(j) // nq)),
                  pl.BlockSpec((n_ctx, dv), lambda b, j: (b, 2 * h + prev(j) // nq)),
                  pl.BlockSpec(diff_lambda.shape, lambda b, j: (0, 0)),
                  pl.BlockSpec((1, dv), lambda b, j: (0, 0))],
        out_specs=pl.BlockSpec((tq, dv), lambda b, j: (b * nq + prev(j) % nq, prev(j) // nq)),
        out_shape=jax.ShapeDtypeStruct((batch * n_lat, h * dv), BF16),
        scratch_shapes=[pltpu.VMEM((2 * tq, n_ctx + n_lat), F32), pltpu.VMEM((2 * tq, n_ctx + n_lat), F32),
                        pltpu.VMEM((2 * tq, LANES), F32), pltpu.VMEM((2 * tq, LANES), F32)],
        compiler_params=_cparams(("arbitrary", "arbitrary")),
    )(pb_l, pb_l, pb_c, pb_l, pb_c, diff_lambda, norm_g.reshape(1, dv))


def _layer_norm(y, g, b):
    mu = jnp.mean(y, axis=-1, keepdims=True)
    yc = y - mu
    var = jnp.mean(yc * yc, axis=-1, keepdims=True)
    return yc * lax.rsqrt(var + LN_EPS) * g + b


def _split2(a):
    hi = a.astype(BF16)
    return hi, (a - hi.astype(F32)).astype(BF16)


def _out_kernel(ga_ref, df_ref, x_ref, wo_ref, g1_ref, sh2_ref, sc2_ref, lg_ref, lb_ref, wr_ref,
                x1_ref, h_ref, aff_ref):
    half = ga_ref.shape[1]
    o = (jnp.dot(ga_ref[...], wo_ref[:half, :], preferred_element_type=F32)
         + jnp.dot(df_ref[...], wo_ref[half:, :], preferred_element_type=F32))
    x1 = _layer_norm(ALPHA * x_ref[...] + g1_ref[...] * o, lg_ref[...], lb_ref[...])
    x1_ref[...] = x1
    hmod = x1 * (1.0 + sc2_ref[...]) + sh2_ref[...]
    hbits = pltpu.bitcast(hmod.astype(BF16).astype(F32), I32)
    hw = h_ref.shape[1]
    h_ref[...] = (hbits[:, hw:] & jnp.int32(-65536)) | lax.shift_right_logical(hbits[:, :hw], 16)
    h_hi, h_lo = _split2(hmod)
    w_hi, w_lo = _split2(wr_ref[...])
    logits = (lax.dot_general(w_hi, h_hi, _NT, preferred_element_type=F32)
              + lax.dot_general(w_lo, h_hi, _NT, preferred_element_type=F32)
              + lax.dot_general(w_hi, h_lo, _NT, preferred_element_type=F32))
    e = jnp.exp(logits - jnp.max(logits, axis=0, keepdims=True))
    aff_ref[...] = e / jnp.sum(e, axis=0, keepdims=True)


def _out(gla_o, diff_o, x2d, w_o, mod3, ln_g, ln_b, w_router_t, *, batch, n_lat):
    m, d = x2d.shape
    half = gla_o.shape[1]
    n_e = w_router_t.shape[0]
    tm = min(512, n_lat)
    tpb = n_lat // tm
    modspec = lambda k: pl.BlockSpec((None, 1, d), lambda i: (i // tpb, 0, k))
    vec = pl.BlockSpec((1, d), lambda i: (0, 0))
    return pl.pallas_call(
        _out_kernel,
        grid=(m // tm,),
        in_specs=[pl.BlockSpec((tm, half), lambda i: (i, 0)),
                  pl.BlockSpec((tm, half), lambda i: (i, 0)),
                  pl.BlockSpec((tm, d), lambda i: (i, 0)),
                  pl.BlockSpec((2 * half, d), lambda i: (0, 0), pipeline_mode=pl.Buffered(1)),
                  modspec(2), modspec(3), modspec(4), vec, vec,
                  pl.BlockSpec((n_e, d), lambda i: (0, 0))],
        out_specs=[pl.BlockSpec((tm, d), lambda i: (i, 0)),
                   pl.BlockSpec((tm, d // 2), lambda i: (i, 0)),
                   pl.BlockSpec((None, n_e, tm), lambda i: (i // tpb, 0, i % tpb))],
        out_shape=[jax.ShapeDtypeStruct((m, d), F32),
                   jax.ShapeDtypeStruct((m, d // 2), I32),
                   jax.ShapeDtypeStruct((batch, n_e, n_lat), F32)],
        compiler_params=_cparams(("arbitrary",)),
    )(gla_o, diff_o, x2d, w_o, mod3, mod3, mod3, ln_g.reshape(1, d), ln_b.reshape(1, d), w_router_t)


def _prefix_count(mask_f32, strict_upper):
    rows, n = mask_f32.shape
    carry = jnp.zeros((rows, 1), F32)
    pieces = []
    for j in range(n // LANES):
        blk = mask_f32[:, j * LANES:(j + 1) * LANES]
        within = jnp.dot(blk.astype(BF16), strict_upper, preferred_element_type=F32)
        pieces.append(within + carry)
        carry = carry + jnp.sum(blk, axis=-1, keepdims=True)
    return jnp.concatenate(pieces, axis=-1)


def _route_kernel(aff_ref, idx_ref, gate_ref, pos_ref, *, cap):
    b = pl.program_id(0)
    v = aff_ref[...]
    n_e, n = v.shape
    bits = pltpu.bitcast(v, I32)

    def search(i, t):
        cand = t | (1 << (30 - i))
        cnt = jnp.sum((bits >= cand).astype(I32), axis=-1, keepdims=True)
        return jnp.where(cnt >= cap, cand, t)

    thr = lax.fori_loop(0, 31, search, jnp.zeros((n_e, 1), I32))
    gt = bits > thr
    eq = bits == thr
    need = (cap - jnp.sum(gt.astype(I32), axis=-1, keepdims=True)).astype(F32)
    r = lax.broadcasted_iota(I32, (LANES, LANES), 0)
    cidx = lax.broadcasted_iota(I32, (LANES, LANES), 1)
    strict_upper = (r < cidx).astype(BF16)
    eq_rank = _prefix_count(eq.astype(F32), strict_upper)
    sel = jnp.logical_or(gt, jnp.logical_and(eq, eq_rank < need))
    pos = _prefix_count(sel.astype(F32), strict_upper)
    pos_ref[...] = jnp.where(sel, pos, -1.0).astype(I32)

    tok = lax.broadcasted_iota(I32, (8, n), 1)
    srow = lax.broadcasted_iota(I32, (8, n), 0)
    tok_hi = (tok >> 6).astype(F32)
    tok_lo = (tok & 63).astype(F32)
    slot = lax.broadcasted_iota(I32, (cap, n), 0)

    def compact(e, carry):
        ve = aff_ref[pl.ds(e, 1), :]
        v_hi = ve.astype(BF16).astype(F32)
        r1 = ve - v_hi
        v_mid = r1.astype(BF16).astype(F32)
        v_lo = r1 - v_mid
        lhs = jnp.where(srow == 0, tok_hi,
              jnp.where(srow == 1, tok_lo,
              jnp.where(srow == 2, v_hi,
              jnp.where(srow == 3, v_mid,
              jnp.where(srow == 4, v_lo, 0.0))))).astype(BF16)
        onehot = (slot == pos_ref[pl.ds(e, 1), :]).astype(F32).astype(BF16)
        res = lax.dot_general(lhs, onehot, _NT, preferred_element_type=F32)
        idx_ref[pl.ds(e, 1), :] = (res[0:1] * 64.0 + res[1:2]).astype(I32) + b * n
        gate_ref[pl.ds(e, 1), :] = (res[2:3] + res[3:4]) + res[4:5]
        return carry

    lax.fori_loop(0, n_e, compact, 0)


def _route(aff_t, *, cap):
    batch, n_e, n = aff_t.shape
    kern = functools.partial(_route_kernel, cap=cap)
    return pl.pallas_call(
        kern,
        grid=(batch,),
        in_specs=[pl.BlockSpec((None, n_e, n), lambda b: (b, 0, 0))],
        out_specs=[pl.BlockSpec((None, n_e, cap), lambda b: (b, 0, 0)),
                   pl.BlockSpec((None, n_e, cap), lambda b: (b, 0, 0))],
        out_shape=[jax.ShapeDtypeStruct((batch, n_e, cap), I32),
                   jax.ShapeDtypeStruct((batch, n_e, cap), F32)],
        scratch_shapes=[pltpu.VMEM((n_e, n), I32)],
        compiler_params=_cparams(("arbitrary",)),
    )(aff_t)


DMA_UNROLL = 8


def _moe_kernel(idx_ref, gate_ref, h_hbm, w1_ref, w3_ref, w2_ref, facc_in, facc_hbm,
                stage_ref, acc_ref, gsem, rsem, wsem, *, rows):
    del facc_in
    e = pl.program_id(0)
    f = pl.program_id(1)
    n_e = pl.num_programs(0)
    nf = pl.num_programs(1)
    slot = e % 2

    def issue_rows(make):
        def body(i, carry):
            for u in range(DMA_UNROLL):
                make(i * DMA_UNROLL + u).start()
            return carry
        lax.fori_loop(0, rows // DMA_UNROLL, body, 0)

    def x_row(ex, s):
        return lambda k: pltpu.make_async_copy(
            h_hbm.at[pl.ds(idx_ref[ex * rows + k], 1), :], stage_ref.at[s, pl.ds(k, 1), :], gsem.at[s])

    def acc_row_in(k):
        return pltpu.make_async_copy(facc_hbm.at[pl.ds(idx_ref[e * rows + k], 1), :], acc_ref.at[pl.ds(k, 1), :], rsem)

    def acc_row_out(k):
        return pltpu.make_async_copy(acc_ref.at[pl.ds(k, 1), :], facc_hbm.at[pl.ds(idx_ref[e * rows + k], 1), :], wsem)

    def wait_x(s):
        pltpu.make_async_copy(h_hbm.at[pl.ds(0, rows), :], stage_ref.at[s], gsem.at[s]).wait()

    def wait_acc_in():
        pltpu.make_async_copy(facc_hbm.at[pl.ds(0, rows), :], acc_ref, rsem).wait()

    def wait_acc_out():
        pltpu.make_async_copy(acc_ref, facc_hbm.at[pl.ds(0, rows), :], wsem).wait()

    @pl.when(f == 0)
    def _():
        @pl.when(e > 0)
        def _():
            wait_acc_out()
        issue_rows(acc_row_in)

        @pl.when(e == 0)
        def _():
            issue_rows(x_row(0, 0))
        wait_x(slot)

        @pl.when(e + 1 < n_e)
        def _():
            issue_rows(x_row(e + 1, 1 - slot))
        wait_acc_in()

    u = stage_ref[slot]
    half = u.shape[1]
    x_lo = pltpu.bitcast(u << 16, F32).astype(BF16)
    x_hi = pltpu.bitcast(u & jnp.int32(-65536), F32).astype(BF16)
    tf = w1_ref.shape[1]
    w13 = jnp.concatenate([w1_ref[...], w3_ref[...]], axis=1).astype(BF16)
    ag = (jnp.dot(x_lo, w13[:half], preferred_element_type=F32)
          + jnp.dot(x_hi, w13[half:], preferred_element_type=F32))
    hid = (_silu(ag[:, :tf]) * ag[:, tf:]).astype(BF16)
    gate = gate_ref[...]
    d = acc_ref.shape[1]
    tc = min(512, d)
    for j in range(d // tc):
        y = jnp.dot(hid, w2_ref[:, j * tc:(j + 1) * tc].astype(BF16), preferred_element_type=F32)
        acc_ref[:, j * tc:(j + 1) * tc] += gate * y

    @pl.when(f == nf - 1)
    def _():
        issue_rows(acc_row_out)

        @pl.when(e == n_e - 1)
        def _():
            wait_acc_out()


def _moe(idx_flat, gate_col, h_packed, w1, w3, w2, facc0, *, rows):
    m, d = facc0.shape
    n_e, _, dff = w1.shape
    tf = min(256, dff)
    kern = functools.partial(_moe_kernel, rows=rows)
    grid_spec = pltpu.PrefetchScalarGridSpec(
        num_scalar_prefetch=1,
        grid=(n_e, dff // tf),
        in_specs=[pl.BlockSpec((None, rows, 1), lambda e, f, idx: (e, 0, 0)),
                  pl.BlockSpec(memory_space=pl.ANY),
                  pl.BlockSpec((None, d, tf), lambda e, f, idx: (e, 0, f)),
                  pl.BlockSpec((None, d, tf), lambda e, f, idx: (e, 0, f)),
                  pl.BlockSpec((None, tf, d), lambda e, f, idx: (e, f, 0)),
                  pl.BlockSpec(memory_space=pl.ANY)],
        out_specs=pl.BlockSpec(memory_space=pl.ANY),
        scratch_shapes=[pltpu.VMEM((2, rows, d // 2), I32), pltpu.VMEM((rows, d), F32),
                        pltpu.SemaphoreType.DMA((2,)), pltpu.SemaphoreType.DMA(()), pltpu.SemaphoreType.DMA(())],
    )
    return pl.pallas_call(
        kern,
        grid_spec=grid_spec,
        out_shape=jax.ShapeDtypeStruct((m, d), F32),
        input_output_aliases={6: 0},
        compiler_params=_cparams(("arbitrary", "arbitrary")),
    )(idx_flat, gate_col, h_packed, w1, w3, w2, facc0)


def _final_kernel(x1_ref, f_ref, g2_ref, lg_ref, lb_ref, o_ref):
    o_ref[...] = _layer_norm(ALPHA * x1_ref[...] + g2_ref[...] * f_ref[...], lg_ref[...], lb_ref[...])


def _final(x1, facc, mod3, ln_g, ln_b, *, n_lat):
    m, d = x1.shape
    tm = min(512, n_lat)
    tpb = n_lat // tm
    vec = pl.BlockSpec((1, d), lambda i: (0, 0))
    return pl.pallas_call(
        _final_kernel,
        grid=(m // tm,),
        in_specs=[pl.BlockSpec((tm, d), lambda i: (i, 0)),
                  pl.BlockSpec((tm, d), lambda i: (i, 0)),
                  pl.BlockSpec((None, 1, d), lambda i: (i // tpb, 0, 5)),
                  vec, vec],
        out_specs=pl.BlockSpec((tm, d), lambda i: (i, 0)),
        out_shape=jax.ShapeDtypeStruct((m, d), F32),
        compiler_params=_cparams(("arbitrary",)),
    )(x1, facc, mod3, ln_g.reshape(1, d), ln_b.reshape(1, d))


def _rope_tables(n_lat):
    rows = n_lat // GRID_W
    row = jnp.repeat(jnp.arange(rows, dtype=F32), GRID_W)
    col = jnp.tile(jnp.arange(GRID_W, dtype=F32), rows)
    n_freq = DIFF_DH // 4
    inv = ROPE_BASE ** (-jnp.arange(n_freq, dtype=F32) / n_freq)
    ang = jnp.concatenate([row[:, None] * inv, col[:, None] * inv], axis=-1)
    cos, sin = jnp.cos(ang), jnp.sin(ang)
    reps = LANES // DIFF_DH
    cos_t = jnp.tile(jnp.concatenate([cos, cos], axis=-1), (1, reps))
    sin_t = jnp.tile(jnp.concatenate([-sin, sin], axis=-1), (1, reps))
    return cos_t, sin_t


def kernel(x, c, ctx, c_ctx, w_ada, b_ada, w_in, w_gate2, b_gate, gla_norm_g, diff_lambda, diff_norm_g, w_o,
           ln1_g, ln1_b, w_router, w1, w3, w2, ln2_g, ln2_b):
    batch, n_lat, d = x.shape
    n_ctx = ctx.shape[1]
    assert w_ada.shape[0] == DEPTH == 1 and batch < ADA_ROWS
    n_e = w_router.shape[-1]
    cap = EC_CAPACITY_FACTOR * n_lat // n_e
    l = 0

    c_rows = jnp.concatenate([c, c_ctx[None, :], jnp.zeros((ADA_ROWS - batch - 1, d), F32)], axis=0)
    mod3 = _ada(c_rows, w_ada[l], b_ada[l]).reshape(ADA_ROWS, 1, 6 * d)

    gqkvr = 2 * (d // 4) + 2 * (d // 2)
    r2 = 2 * GLA_RANK
    w_main = jnp.concatenate([w_in[l][:, :gqkvr], w_in[l][:, gqkvr + r2:]], axis=1).astype(BF16)
    w_lr = w_in[l][:, gqkvr:gqkvr + r2].astype(BF16)
    hk = w_gate2.shape[-1]
    zeros = jnp.zeros((GLA_RANK, hk), F32)
    w_g = jnp.concatenate([jnp.concatenate([w_gate2[l, 0], zeros], axis=1),
                           jnp.concatenate([zeros, w_gate2[l, 1]], axis=1)], axis=0).astype(BF16)
    b_g = b_gate[l].reshape(1, 2 * hk)
    cos_t, sin_t = _rope_tables(n_lat)

    tm_l = min(1024, n_lat)
    tpb = n_lat // tm_l
    x2d = x.reshape(batch * n_lat, d)
    pf_l, pb_l, lg_l = _proj(x2d, mod3, lambda i: i // tpb, w_main, w_lr, w_g, b_g, cos_t, sin_t,
                             rope=True, tm=tm_l, n_tab_tiles=tpb)
    ctx2d = ctx.reshape(batch * n_ctx, d)
    tm_c = min(1024, batch * n_ctx)
    pf_c, pb_c, lg_c = _proj(ctx2d, mod3, lambda i: batch, w_main, w_lr, w_g, b_g, cos_t, sin_t,
                             rope=False, tm=tm_c, n_tab_tiles=1)

    gla_o = _gla(pf_l, lg_l, pf_c, lg_c, gla_norm_g[l], batch=batch, n_lat=n_lat, n_ctx=n_ctx)
    diff_o = _diff(pb_l, pb_c, diff_lambda[l], diff_norm_g[l], batch=batch, n_lat=n_lat, n_ctx=n_ctx)

    x1, h_packed, aff_t = _out(gla_o, diff_o, x2d, w_o[l].astype(BF16), mod3, ln1_g[l], ln1_b[l],
                           w_router[l].T, batch=batch, n_lat=n_lat)

    idx, gate = _route(aff_t, cap=cap)
    rows = batch * cap
    idx_flat = idx.transpose(1, 0, 2).reshape(n_e * rows)
    gate_col = gate.transpose(1, 0, 2).reshape(n_e, rows, 1)
    facc = _moe(idx_flat, gate_col, h_packed, w1[l], w3[l], w2[l], jnp.zeros_like(x1), rows=rows)

    out = _final(x1, facc, mod3, ln2_g[l], ln2_b[l], n_lat=n_lat)
    return out.reshape(batch, n_lat, d)
```

```python
import functools
import math

import jax
import jax.numpy as jnp
import numpy as np
from jax import lax
from jax.experimental import pallas as pl
from jax.experimental.pallas import tpu as pltpu

F32 = jnp.float32
BF16 = jnp.bfloat16
I32 = jnp.int32

GRID_W = 64
GLA_HEADS = 4
GLA_RANK = 16
GLA_TAU = 16.0
GLA_CHUNK = 64
DIFF_HEADS = 8
DIFF_DH = 64
ROPE_BASE = 10000.0
EC_CAPACITY_FACTOR = 2
LN_EPS = 1e-5
RMS_EPS = 1e-6
DEPTH = 1
ALPHA = (2.0 * DEPTH) ** 0.25
LAM_INIT = 0.8 - 0.6 * math.exp(-0.3 * 0)

LANES = 128
ADA_ROWS = 8
VMEM_LIMIT = 56 * 1024 * 1024

_NT = (((1,), (1,)), ((), ()))
_TN = (((0,), (0,)), ((), ()))


def _cparams(sem):
    return pltpu.CompilerParams(dimension_semantics=sem, vmem_limit_bytes=VMEM_LIMIT)


def _silu(a):
    return a * (1.0 / (1.0 + jnp.exp(-a)))


def _log_sigmoid(z):
    return jnp.minimum(z, 0.0) - jnp.log1p(jnp.exp(-jnp.abs(z)))


def _ada_kernel(c_ref, w_ref, b_ref, o_ref):
    s = _silu(c_ref[...]).astype(BF16)
    o_ref[...] = jnp.dot(s, w_ref[...].astype(BF16), preferred_element_type=F32) + b_ref[...]


def _ada(c_rows, w_ada, b_ada):
    d, n_out = w_ada.shape
    tn = min(1024, n_out)
    return pl.pallas_call(
        _ada_kernel,
        grid=(n_out // tn,),
        in_specs=[pl.BlockSpec((ADA_ROWS, d), lambda j: (0, 0)),
                  pl.BlockSpec((d, tn), lambda j: (0, j)),
                  pl.BlockSpec((1, tn), lambda j: (0, j))],
        out_specs=pl.BlockSpec((ADA_ROWS, tn), lambda j: (0, j)),
        out_shape=jax.ShapeDtypeStruct((ADA_ROWS, n_out), F32),
        compiler_params=_cparams(("arbitrary",)),
    )(c_rows, w_ada, b_ada.reshape(1, n_out))


def _proj_kernel(x_ref, sh_ref, sc_ref, wa_ref, wb_ref, wlr_ref, wg_ref, bg_ref, cos_ref, sin_ref,
                 of_ref, ob_ref, lg_ref, u_ref, *, rope, n_f32, n_q, n_rope):
    n = pl.program_id(1)

    @pl.when(n == 0)
    def _():
        u = (x_ref[...] * (1.0 + sc_ref[...]) + sh_ref[...]).astype(BF16)
        u_ref[...] = u
        lr = lax.dot_general(u, wlr_ref[...], _NT, preferred_element_type=F32)
        z = jnp.dot(lr.astype(BF16), wg_ref[...], preferred_element_type=F32) + bg_ref[...]
        lg_ref[...] = _log_sigmoid(z) * (1.0 / GLA_TAU)

    @pl.when(n < n_f32)
    def _():
        of_ref[...] = lax.dot_general(u_ref[...], wa_ref[...].astype(BF16), _NT, preferred_element_type=F32)

    @pl.when(jnp.logical_and(n >= n_f32, n < n_f32 + n_rope))
    def _():
        acc = lax.dot_general(u_ref[...], wb_ref[...], _NT, preferred_element_type=F32)
        scale = jnp.where(n < n_f32 + n_q, DIFF_DH ** -0.5, 1.0).astype(F32)
        if rope:
            lane = lax.broadcasted_iota(I32, (acc.shape[0], LANES), 1)
            first = (lane % DIFF_DH) < (DIFF_DH // 2)
            cs = cos_ref[...]
            sn = sin_ref[...]
            for j in range(acc.shape[1] // LANES):
                a = acc[:, j * LANES:(j + 1) * LANES]
                partner = jnp.where(first, pltpu.roll(a, LANES - DIFF_DH // 2, 1), pltpu.roll(a, DIFF_DH // 2, 1))
                ob_ref[:, j * LANES:(j + 1) * LANES] = ((a * cs + partner * sn) * scale).astype(BF16)
        else:
            ob_ref[...] = (acc * scale).astype(BF16)

    @pl.when(n >= n_f32 + n_rope)
    def _():
        ob_ref[...] = lax.dot_general(u_ref[...], wb_ref[...], _NT, preferred_element_type=F32).astype(BF16)


def _proj(x2d, mod3, row_of_tile, w_in, w_diff, w_lr, w_g, b_g, cos_t, sin_t, *, rope, tm, n_tab_tiles):
    m, d = x2d.shape
    tn = 512
    n_f32 = w_diff.shape[0] // tn
    n_tiles = 2 * n_f32
    n_q = n_tiles // 6
    n_rope = n_tiles // 3
    r2 = w_lr.shape[0]
    ng = w_g.shape[1]
    kern = functools.partial(_proj_kernel, rope=rope, n_f32=n_f32, n_q=n_q, n_rope=n_rope)
    return pl.pallas_call(
        kern,
        grid=(m // tm, n_tiles),
        in_specs=[pl.BlockSpec((tm, d), lambda i, j: (i, 0)),
                  pl.BlockSpec((None, 1, d), lambda i, j: (row_of_tile(i), 0, 0)),
                  pl.BlockSpec((None, 1, d), lambda i, j: (row_of_tile(i), 0, 1)),
                  pl.BlockSpec((tn, d), lambda i, j: (jnp.minimum(j, n_f32 - 1), 0)),
                  pl.BlockSpec((tn, d), lambda i, j: (jnp.maximum(j - n_f32, 0), 0)),
                  pl.BlockSpec((r2, d), lambda i, j: (0, 0)),
                  pl.BlockSpec((r2, ng), lambda i, j: (0, 0)),
                  pl.BlockSpec((1, ng), lambda i, j: (0, 0)),
                  pl.BlockSpec((tm, LANES), lambda i, j: (i % n_tab_tiles, 0)),
                  pl.BlockSpec((tm, LANES), lambda i, j: (i % n_tab_tiles, 0))],
        out_specs=[pl.BlockSpec((tm, tn), lambda i, j: (i, jnp.minimum(j, n_f32 - 1))),
                   pl.BlockSpec((tm, tn), lambda i, j: (i, jnp.maximum(j - n_f32, 0))),
                   pl.BlockSpec((tm, ng), lambda i, j: (i, 0))],
        out_shape=[jax.ShapeDtypeStruct((m, n_f32 * tn), F32),
                   jax.ShapeDtypeStruct((m, (n_tiles - n_f32) * tn), BF16),
                   jax.ShapeDtypeStruct((m, ng), F32)],
        scratch_shapes=[pltpu.VMEM((tm, d), BF16)],
        compiler_params=_cparams(("arbitrary", "arbitrary")),
    )(x2d, mod3, mod3, w_in, w_diff, w_lr, w_g, b_g, cos_t, sin_t)


GLA_GROUP = 4


def _split3(a):
    hi = a.astype(BF16)
    r1 = a - hi.astype(F32)
    mid = r1.astype(BF16)
    return hi, mid, (r1 - mid.astype(F32)).astype(BF16)


def _gla_group(q, k, v, g, st_ref, sum_mat, causal, *, forward, want_out, scale):
    c = GLA_CHUNK
    rows, dk = k.shape
    n_chunks = rows // c
    parts = jnp.dot(sum_mat, jnp.concatenate(_split3(g), axis=1), preferred_element_type=F32)
    sums = (parts[:, :dk] + parts[:, dk:2 * dk]) + parts[:, 2 * dk:]
    bcum = sums[:rows]
    btot = sums[rows:]
    vb = v.astype(BF16)
    k_out = (k * jnp.exp(btot - bcum)).astype(BF16)
    decay = jnp.exp(btot)
    if want_out:
        q_in = ((q * scale) * jnp.exp(bcum)).astype(BF16)
        k_in = (k * jnp.exp(-bcum)).astype(BF16)
        att = lax.dot_general(q_in, k_in, _NT, preferred_element_type=F32)
        att = jnp.where(causal, att, 0.0).astype(BF16)
        intra = jnp.dot(att, vb, preferred_element_type=F32)
    st = st_ref[...]
    outs = [None] * n_chunks
    for ci in (range(n_chunks) if forward else reversed(range(n_chunks))):
        sl = slice(ci * c, (ci + 1) * c)
        if want_out:
            outs[ci] = intra[sl] + lax.dot_general(q_in[sl], st.astype(BF16), _NT, preferred_element_type=F32)
        st = (st * decay[ci * c:ci * c + 1]
              + lax.dot_general(vb[sl], k_out[sl], _TN, preferred_element_type=F32))
    st_ref[...] = st
    return jnp.concatenate(outs, axis=0) if want_out else None


def _gla_kernel(q_ref, k_ref, v_ref, r_ref, gf_ref, gb_ref, kc_ref, vc_ref, gfc_ref, gbc_ref, ng_ref,
                o_ref, of_ref, obk_ref, sf_ref, sb_ref, *, n_lat, n_ctx, dk):
    c = GLA_CHUNK
    scale = dk ** -0.5
    sf_ref[...] = jnp.zeros_like(sf_ref)
    sb_ref[...] = jnp.zeros_like(sb_ref)

    def masks(rows):
        row = lax.broadcasted_iota(I32, (rows, rows), 0)
        col = lax.broadcasted_iota(I32, (rows, rows), 1)
        same = (row // c) == (col // c)
        lower = jnp.logical_and(same, row >= col)
        upper = jnp.logical_and(same, row <= col)
        total = same.astype(BF16)
        return (lower, jnp.concatenate([lower.astype(BF16), total], axis=0),
                upper, jnp.concatenate([upper.astype(BF16), total], axis=0))

    def scan(n, refs_f, refs_b, want_out):
        rows = min(GLA_GROUP * c, n)
        lower, sum_f, upper, sum_b = masks(rows)
        n_groups = n // rows

        def body(i, carry):
            rf = pl.multiple_of(i * rows, rows)
            rb = pl.multiple_of((n_groups - 1 - i) * rows, rows)
            ld = lambda ref, r0: None if ref is None else ref[pl.ds(r0, rows), :]
            out_f = _gla_group(*(ld(r, rf) for r in refs_f), sf_ref, sum_f, lower,
                               forward=True, want_out=want_out, scale=scale)
            out_b = _gla_group(*(ld(r, rb) for r in refs_b), sb_ref, sum_b, upper,
                               forward=False, want_out=want_out, scale=scale)
            if want_out:
                of_ref[pl.ds(rf, rows), :] = out_f
                obk_ref[pl.ds(rb, rows), :] = out_b
            return carry

        lax.fori_loop(0, n_groups, body, 0)

    scan(n_ctx, (None, kc_ref, vc_ref, gfc_ref), (None, kc_ref, vc_ref, gbc_ref), False)
    scan(n_lat, (q_ref, k_ref, v_ref, gf_ref), (q_ref, k_ref, v_ref, gb_ref), True)

    tb = min(512, n_lat)

    def fin_body(i, carry):
        r0 = pl.multiple_of(i * tb, tb)
        o = of_ref[pl.ds(r0, tb), :] + obk_ref[pl.ds(r0, tb), :]
        ms = jnp.mean(o * o, axis=-1, keepdims=True)
        o = o * lax.rsqrt(ms + RMS_EPS) * ng_ref[...]
        o_ref[pl.ds(r0, tb), :] = (o * _silu(r_ref[pl.ds(r0, tb), :])).astype(o_ref.dtype)
        return carry

    lax.fori_loop(0, n_lat // tb, fin_body, 0)


def _gla(pf_l, lg_l, pf_c, lg_c, norm_g, *, batch, n_lat, n_ctx):
    h = GLA_HEADS
    dk = pf_l.shape[1] // 6 // h
    dv = 2 * dk
    kern = functools.partial(_gla_kernel, n_lat=n_lat, n_ctx=n_ctx, dk=dk)
    lat = lambda width, off: pl.BlockSpec((n_lat, width), lambda b, i: (b, off + i))
    ctx = lambda width, off: pl.BlockSpec((n_ctx, width), lambda b, i: (b, off + i))
    return pl.pallas_call(
        kern,
        grid=(batch, h),
        in_specs=[lat(dk, 0), lat(dk, h), lat(dv, h), lat(dv, 2 * h),
                  lat(dk, 0), lat(dk, h),
                  ctx(dk, h), ctx(dv, h), ctx(dk, 0), ctx(dk, h),
                  pl.BlockSpec((1, dv), lambda b, i: (0, 0))],
        out_specs=pl.BlockSpec((n_lat, dv), lambda b, i: (b, i)),
        out_shape=jax.ShapeDtypeStruct((batch * n_lat, h * dv), BF16),
        scratch_shapes=[pltpu.VMEM((n_lat, dv), F32), pltpu.VMEM((n_lat, dv), F32),
                        pltpu.VMEM((dv, dk), F32), pltpu.VMEM((dv, dk), F32)],
        compiler_params=_cparams(("arbitrary", "arbitrary")),
    )(pf_l, pf_l, pf_l, pf_l, lg_l, lg_l, pf_c, pf_c, lg_c, lg_c, norm_g.reshape(1, dv))


DIFF_KEY_CHUNK = 512


def _diff_step(q_ref, kc_ref, kl_ref, vc_ref, vl_ref, lam_ref, ng_ref, o_ref, s_w, m_w, s_r, m_r):
    tq = o_ref.shape[0]
    dv = o_ref.shape[1]
    q = q_ref[...]
    lane = lax.broadcasted_iota(I32, q.shape, 1)
    zero = jnp.zeros_like(q)
    qq = jnp.concatenate([jnp.where(lane < DIFF_DH, q, zero), jnp.where(lane >= DIFF_DH, q, zero)], axis=0)
    mx = jnp.max(m_r[...], axis=-1, keepdims=True)
    m_new = jnp.full((2 * tq, LANES), -jnp.inf, F32)
    acc = jnp.zeros((2 * tq, 2 * dv), F32)
    col = 0
    for k_ref, v_ref in ((kc_ref, vc_ref), (kl_ref, vl_ref)):
        n_keys = k_ref.shape[0]
        w = min(DIFF_KEY_CHUNK, n_keys)
        ones_col = (lax.broadcasted_iota(I32, (w, dv), 1) == 0).astype(BF16)
        for r0 in range(0, n_keys, w):
            s_new = lax.dot_general(qq, k_ref[r0:r0 + w, :], _NT, preferred_element_type=F32)
            s_w[:, col:col + w] = s_new
            for t in range(w // LANES):
                m_new = jnp.maximum(m_new, s_new[:, t * LANES:(t + 1) * LANES])
            e = jnp.exp(s_r[:, col:col + w] - mx).astype(BF16)
            v_ext = jnp.concatenate([v_ref[r0:r0 + w, :], ones_col], axis=1)
            acc = acc + jnp.dot(e, v_ext, preferred_element_type=F32)
            col += w
    m_w[...] = m_new
    dl = lam_ref[...]
    lam = (jnp.exp(jnp.sum(dl[0:1] * dl[1:2], axis=-1, keepdims=True))
           - jnp.exp(jnp.sum(dl[2:3] * dl[3:4], axis=-1, keepdims=True)) + LAM_INIT)
    inv = 1.0 / acc[:, dv:dv + 1]
    o = acc[:tq, :dv] * inv[:tq] - acc[tq:, :dv] * (inv[tq:] * lam)
    ms = jnp.mean(o * o, axis=-1, keepdims=True)
    o_ref[...] = (o * lax.rsqrt(ms + RMS_EPS) * ng_ref[...] * (1.0 - LAM_INIT)).astype(o_ref.dtype)


def _diff_kernel(q_ref, kl_ref, kc_ref, vl_ref, vc_ref, lam_ref, ng_ref, o_ref, sa_ref, sb_ref, ma_ref, mb_ref):
    j = pl.program_id(1)

    @pl.when(j == 0)
    def _():
        sb_ref[...] = jnp.zeros_like(sb_ref)
        mb_ref[...] = jnp.zeros_like(mb_ref)

    args = (q_ref, kc_ref, kl_ref, vc_ref, vl_ref, lam_ref, ng_ref, o_ref)

    @pl.when(j % 2 == 0)
    def _():
        _diff_step(*args, sa_ref, ma_ref, sb_ref, mb_ref)

    @pl.when(j % 2 == 1)
    def _():
        _diff_step(*args, sb_ref, mb_ref, sa_ref, ma_ref)


def _diff(pb_l, pb_c, diff_lambda, norm_g, *, batch, n_lat, n_ctx):
    h = DIFF_HEADS
    dv = 2 * DIFF_DH
    tq = min(256, n_lat)
    nq = n_lat // tq
    nblk = h * nq
    cur = lambda j: jnp.minimum(j, nblk - 1)
    prev = lambda j: jnp.maximum(j - 1, 0)
    return pl.pallas_call(
        _diff_kernel,
        grid=(batch, nblk + 1),
        in_specs=[pl.BlockSpec((tq, dv), lambda b, j: (b * nq + cur(j) % nq, cur(j) // nq)),
                  pl.BlockSpec((n_lat, dv), lambda b, j: (b, h + cur(j) // nq)),
                  pl.BlockSpec((n_ctx, dv), lambda b, j: (b, h + cur(j) // nq)),
                  pl.BlockSpec((n_lat, dv), lambda b, j: (b, 2 * h + prev(j) // nq)),
                  pl.BlockSpec((n_ctx, dv), lambda b, j: (b, 2 * h + prev(j) // nq)),
                  pl.BlockSpec(diff_lambda.shape, lambda b, j: (0, 0)),
                  pl.BlockSpec((1, dv), lambda b, j: (0, 0))],
        out_specs=pl.BlockSpec((tq, dv), lambda b, j: (b * nq + prev(j) % nq, prev(j) // nq)),
        out_shape=jax.ShapeDtypeStruct((batch * n_lat, h * dv), BF16),
        scratch_shapes=[pltpu.VMEM((2 * tq, n_ctx + n_lat), F32), pltpu.VMEM((2 * tq, n_ctx + n_lat), F32),
                        pltpu.VMEM((2 * tq, LANES), F32), pltpu.VMEM((2 * tq, LANES), F32)],
        compiler_params=_cparams(("arbitrary", "arbitrary")),
    )(pb_l, pb_l, pb_c, pb_l, pb_c, diff_lambda, norm_g.reshape(1, dv))


def _layer_norm(y, g, b):
    mu = jnp.mean(y, axis=-1, keepdims=True)
    yc = y - mu
    var = jnp.mean(yc * yc, axis=-1, keepdims=True)
    return yc * lax.rsqrt(var + LN_EPS) * g + b


def _split2(a):
    hi = a.astype(BF16)
    return hi, (a - hi.astype(F32)).astype(BF16)


def _out_kernel(ga_ref, df_ref, x_ref, wo_ref, g1_ref, sh2_ref, sc2_ref, lg_ref, lb_ref, wr_ref,
                x1_ref, h_ref, aff_ref):
    half = ga_ref.shape[1]
    o = (jnp.dot(ga_ref[...], wo_ref[:half, :], preferred_element_type=F32)
         + jnp.dot(df_ref[...], wo_ref[half:, :], preferred_element_type=F32))
    x1 = _layer_norm(ALPHA * x_ref[...] + g1_ref[...] * o, lg_ref[...], lb_ref[...])
    x1_ref[...] = ALPHA * x1
    hmod = x1 * (1.0 + sc2_ref[...]) + sh2_ref[...]
    hbits = pltpu.bitcast(hmod.astype(BF16).astype(F32), I32)
    hw = h_ref.shape[1]
    h_ref[...] = (hbits[:, hw:] & jnp.int32(-65536)) | lax.shift_right_logical(hbits[:, :hw], 16)
    h_hi, h_lo = _split2(hmod)
    w_hi, w_lo = _split2(wr_ref[...])
    logits = (lax.dot_general(w_hi, h_hi, _NT, preferred_element_type=F32)
              + lax.dot_general(w_lo, h_hi, _NT, preferred_element_type=F32)
              + lax.dot_general(w_hi, h_lo, _NT, preferred_element_type=F32))
    e = jnp.exp(logits - jnp.max(logits, axis=0, keepdims=True))
    aff_ref[...] = e / jnp.sum(e, axis=0, keepdims=True)


def _out(gla_o, diff_o, x2d, w_o, mod3, ln_g, ln_b, w_router_t, *, batch, n_lat):
    m, d = x2d.shape
    half = gla_o.shape[1]
    n_e = w_router_t.shape[0]
    tm = min(512, n_lat)
    tpb = n_lat // tm
    modspec = lambda k: pl.BlockSpec((None, 1, d), lambda i: (i // tpb, 0, k))
    vec = pl.BlockSpec((1, d), lambda i: (0, 0))
    return pl.pallas_call(
        _out_kernel,
        grid=(m // tm,),
        in_specs=[pl.BlockSpec((tm, half), lambda i: (i, 0)),
                  pl.BlockSpec((tm, half), lambda i: (i, 0)),
                  pl.BlockSpec((tm, d), lambda i: (i, 0)),
                  pl.BlockSpec((2 * half, d), lambda i: (0, 0), pipeline_mode=pl.Buffered(1)),
                  modspec(2), modspec(3), modspec(4), vec, vec,
                  pl.BlockSpec((n_e, d), lambda i: (0, 0))],
        out_specs=[pl.BlockSpec((tm, d), lambda i: (i, 0)),
                   pl.BlockSpec((tm, d // 2), lambda i: (i, 0)),
                   pl.BlockSpec((None, n_e, tm), lambda i: (i // tpb, 0, i % tpb))],
        out_shape=[jax.ShapeDtypeStruct((m, d), F32),
                   jax.ShapeDtypeStruct((m, d // 2), I32),
                   jax.ShapeDtypeStruct((batch, n_e, n_lat), F32)],
        compiler_params=_cparams(("arbitrary",)),
    )(gla_o, diff_o, x2d, w_o, mod3, mod3, mod3, ln_g.reshape(1, d), ln_b.reshape(1, d), w_router_t)


def _prefix_count(mask_f32, strict_upper):
    rows, n = mask_f32.shape
    carry = jnp.zeros((rows, 1), F32)
    pieces = []
    for j in range(n // LANES):
        blk = mask_f32[:, j * LANES:(j + 1) * LANES]
        within = jnp.dot(blk.astype(BF16), strict_upper, preferred_element_type=F32)
        pieces.append(within + carry)
        carry = carry + jnp.sum(blk, axis=-1, keepdims=True)
    return jnp.concatenate(pieces, axis=-1)


def _route_kernel(aff_ref, idx_ref, gate_ref, pos_ref, *, cap):
    b = pl.program_id(0)
    v = aff_ref[...]
    n_e, n = v.shape
    bits = pltpu.bitcast(v, I32)

    def search(i, t):
        cand = t | (1 << (30 - i))
        cnt = jnp.sum((bits >= cand).astype(I32), axis=-1, keepdims=True)
        return jnp.where(cnt >= cap, cand, t)

    thr = lax.fori_loop(0, 31, search, jnp.zeros((n_e, 1), I32))
    gt = bits > thr
    eq = bits == thr
    need = (cap - jnp.sum(gt.astype(I32), axis=-1, keepdims=True)).astype(F32)
    r = lax.broadcasted_iota(I32, (LANES, LANES), 0)
    cidx = lax.broadcasted_iota(I32, (LANES, LANES), 1)
    strict_upper = (r < cidx).astype(BF16)
    eq_rank = _prefix_count(eq.astype(F32), strict_upper)
    sel = jnp.logical_or(gt, jnp.logical_and(eq, eq_rank < need))
    pos = _prefix_count(sel.astype(F32), strict_upper)
    pos_ref[...] = jnp.where(sel, pos, -1.0).astype(I32)

    tok = lax.broadcasted_iota(I32, (8, n), 1)
    srow = lax.broadcasted_iota(I32, (8, n), 0)
    tok_hi = (tok >> 6).astype(F32)
    tok_lo = (tok & 63).astype(F32)
    slot = lax.broadcasted_iota(I32, (cap, n), 0)

    def compact(e, carry):
        ve = aff_ref[pl.ds(e, 1), :]
        v_hi = ve.astype(BF16).astype(F32)
        r1 = ve - v_hi
        v_mid = r1.astype(BF16).astype(F32)
        v_lo = r1 - v_mid
        lhs = jnp.where(srow == 0, tok_hi,
              jnp.where(srow == 1, tok_lo,
              jnp.where(srow == 2, v_hi,
              jnp.where(srow == 3, v_mid,
              jnp.where(srow == 4, v_lo, 0.0))))).astype(BF16)
        onehot = (slot == pos_ref[pl.ds(e, 1), :]).astype(F32).astype(BF16)
        res = lax.dot_general(lhs, onehot, _NT, preferred_element_type=F32)
        idx_ref[pl.ds(e, 1), :] = (res[0:1] * 64.0 + res[1:2]).astype(I32) + b * n
        gate_ref[pl.ds(e, 1), :] = (res[2:3] + res[3:4]) + res[4:5]
        return carry

    lax.fori_loop(0, n_e, compact, 0)


def _route(aff_t, *, cap):
    batch, n_e, n = aff_t.shape
    kern = functools.partial(_route_kernel, cap=cap)
    return pl.pallas_call(
        kern,
        grid=(batch,),
        in_specs=[pl.BlockSpec((None, n_e, n), lambda b: (b, 0, 0))],
        out_specs=[pl.BlockSpec((None, n_e, cap), lambda b: (b, 0, 0)),
                   pl.BlockSpec((None, n_e, cap), lambda b: (b, 0, 0))],
        out_shape=[jax.ShapeDtypeStruct((batch, n_e, cap), I32),
                   jax.ShapeDtypeStruct((batch, n_e, cap), F32)],
        scratch_shapes=[pltpu.VMEM((n_e, n), I32)],
        compiler_params=_cparams(("arbitrary",)),
    )(aff_t)


DMA_UNROLL = 8


def _moe_kernel(idx_ref, gate_ref, h_hbm, w1_ref, w3_ref, w2_ref, g2_ref, facc_in, facc_hbm,
                stage_ref, acc_ref, gsem, rsem, wsem, *, rows, cap):
    del facc_in
    e = pl.program_id(0)
    f = pl.program_id(1)
    n_e = pl.num_programs(0)
    nf = pl.num_programs(1)
    slot = e % 2

    def issue_rows(make):
        def body(i, carry):
            for u in range(DMA_UNROLL):
                make(i * DMA_UNROLL + u).start()
            return carry
        lax.fori_loop(0, rows // DMA_UNROLL, body, 0)

    def x_row(ex, s):
        return lambda k: pltpu.make_async_copy(
            h_hbm.at[pl.ds(idx_ref[ex * rows + k], 1), :], stage_ref.at[s, pl.ds(k, 1), :], gsem.at[s])

    def acc_row_in(k):
        return pltpu.make_async_copy(facc_hbm.at[pl.ds(idx_ref[e * rows + k], 1), :], acc_ref.at[pl.ds(k, 1), :], rsem)

    def acc_row_out(k):
        return pltpu.make_async_copy(acc_ref.at[pl.ds(k, 1), :], facc_hbm.at[pl.ds(idx_ref[e * rows + k], 1), :], wsem)

    def wait_x(s):
        pltpu.make_async_copy(h_hbm.at[pl.ds(0, rows), :], stage_ref.at[s], gsem.at[s]).wait()

    def wait_acc_in():
        pltpu.make_async_copy(facc_hbm.at[pl.ds(0, rows), :], acc_ref, rsem).wait()

    def wait_acc_out():
        pltpu.make_async_copy(acc_ref, facc_hbm.at[pl.ds(0, rows), :], wsem).wait()

    @pl.when(f == 0)
    def _():
        @pl.when(e > 0)
        def _():
            wait_acc_out()
        issue_rows(acc_row_in)

        @pl.when(e == 0)
        def _():
            issue_rows(x_row(0, 0))
        wait_x(slot)

        @pl.when(e + 1 < n_e)
        def _():
            issue_rows(x_row(e + 1, 1 - slot))
        wait_acc_in()

    u = stage_ref[slot]
    half = u.shape[1]
    x_lo = pltpu.bitcast(u << 16, F32).astype(BF16)
    x_hi = pltpu.bitcast(u & jnp.int32(-65536), F32).astype(BF16)
    tf = w1_ref.shape[1]
    w13 = jnp.concatenate([w1_ref[...], w3_ref[...]], axis=1).astype(BF16)
    ag = (jnp.dot(x_lo, w13[:half], preferred_element_type=F32)
          + jnp.dot(x_hi, w13[half:], preferred_element_type=F32))
    hid = (_silu(ag[:, :tf]) * ag[:, tf:]).astype(BF16)
    gate = gate_ref[...]
    d = acc_ref.shape[1]
    tc = min(512, d)
    for j in range(d // tc):
        cols = slice(j * tc, (j + 1) * tc)
        y = gate * jnp.dot(hid, w2_ref[:, cols].astype(BF16), preferred_element_type=F32)
        for b in range(rows // cap):
            rb = slice(b * cap, (b + 1) * cap)
            acc_ref[rb, cols] += y[rb] * g2_ref[b][:, cols]

    @pl.when(f == nf - 1)
    def _():
        issue_rows(acc_row_out)

        @pl.when(e == n_e - 1)
        def _():
            wait_acc_out()


def _moe(idx_flat, gate_col, h_packed, w1, w3, w2, mod3, facc0, *, rows, cap):
    m, d = facc0.shape
    n_e, _, dff = w1.shape
    tf = min(256, dff)
    kern = functools.partial(_moe_kernel, rows=rows, cap=cap)
    grid_spec = pltpu.PrefetchScalarGridSpec(
        num_scalar_prefetch=1,
        grid=(n_e, dff // tf),
        in_specs=[pl.BlockSpec((None, rows, 1), lambda e, f, idx: (e, 0, 0)),
                  pl.BlockSpec(memory_space=pl.ANY),
                  pl.BlockSpec((None, d, tf), lambda e, f, idx: (e, 0, f)),
                  pl.BlockSpec((None, d, tf), lambda e, f, idx: (e, 0, f)),
                  pl.BlockSpec((None, tf, d), lambda e, f, idx: (e, f, 0)),
                  pl.BlockSpec((ADA_ROWS, 1, d), lambda e, f, idx: (0, 0, 5)),
                  pl.BlockSpec(memory_space=pl.ANY)],
        out_specs=pl.BlockSpec(memory_space=pl.ANY),
        scratch_shapes=[pltpu.VMEM((2, rows, d // 2), I32), pltpu.VMEM((rows, d), F32),
                        pltpu.SemaphoreType.DMA((2,)), pltpu.SemaphoreType.DMA(()), pltpu.SemaphoreType.DMA(())],
    )
    return pl.pallas_call(
        kern,
        grid_spec=grid_spec,
        out_shape=jax.ShapeDtypeStruct((m, d), F32),
        input_output_aliases={7: 0},
        compiler_params=_cparams(("arbitrary", "arbitrary")),
    )(idx_flat, gate_col, h_packed, w1, w3, w2, mod3, facc0)


def _final_kernel(y_ref, lg_ref, lb_ref, o_ref):
    o_ref[...] = _layer_norm(y_ref[...], lg_ref[...], lb_ref[...])


def _final(pre_ln, ln_g, ln_b, *, n_lat):
    m, d = pre_ln.shape
    tm = min(512, n_lat)
    vec = pl.BlockSpec((1, d), lambda i: (0, 0))
    return pl.pallas_call(
        _final_kernel,
        grid=(m // tm,),
        in_specs=[pl.BlockSpec((tm, d), lambda i: (i, 0)), vec, vec],
        out_specs=pl.BlockSpec((tm, d), lambda i: (i, 0)),
        out_shape=jax.ShapeDtypeStruct((m, d), F32),
        compiler_params=_cparams(("arbitrary",)),
    )(pre_ln, ln_g.reshape(1, d), ln_b.reshape(1, d))


def _rope_tables(n_lat):
    rows = n_lat // GRID_W
    row = jnp.repeat(jnp.arange(rows, dtype=F32), GRID_W)
    col = jnp.tile(jnp.arange(GRID_W, dtype=F32), rows)
    n_freq = DIFF_DH // 4
    inv = ROPE_BASE ** (-jnp.arange(n_freq, dtype=F32) / n_freq)
    ang = jnp.concatenate([row[:, None] * inv, col[:, None] * inv], axis=-1)
    cos, sin = jnp.cos(ang), jnp.sin(ang)
    reps = LANES // DIFF_DH
    cos_t = jnp.tile(jnp.concatenate([cos, cos], axis=-1), (1, reps))
    sin_t = jnp.tile(jnp.concatenate([-sin, sin], axis=-1), (1, reps))
    return cos_t, sin_t


def kernel(x, c, ctx, c_ctx, w_ada, b_ada, w_in, w_gate2, b_gate, gla_norm_g, diff_lambda, diff_norm_g, w_o,
           ln1_g, ln1_b, w_router, w1, w3, w2, ln2_g, ln2_b):
    batch, n_lat, d = x.shape
    n_ctx = ctx.shape[1]
    assert w_ada.shape[0] == DEPTH == 1 and batch < ADA_ROWS
    n_e = w_router.shape[-1]
    cap = EC_CAPACITY_FACTOR * n_lat // n_e
    l = 0

    c_rows = jnp.concatenate([c, c_ctx[None, :], jnp.zeros((ADA_ROWS - batch - 1, d), F32)], axis=0)
    mod3 = _ada(c_rows, w_ada[l], b_ada[l]).reshape(ADA_ROWS, 1, 6 * d)

    gqkvr = 2 * (d // 4) + 2 * (d // 2)
    r2 = 2 * GLA_RANK
    w_in_t = w_in[l].T
    w_diff = w_in_t[gqkvr + r2:].astype(BF16)
    w_lr = w_in_t[gqkvr:gqkvr + r2].astype(BF16)
    hk = w_gate2.shape[-1]
    zeros = jnp.zeros((GLA_RANK, hk), F32)
    w_g = jnp.concatenate([jnp.concatenate([w_gate2[l, 0], zeros], axis=1),
                           jnp.concatenate([zeros, w_gate2[l, 1]], axis=1)], axis=0).astype(BF16)
    b_g = b_gate[l].reshape(1, 2 * hk)
    cos_t, sin_t = _rope_tables(n_lat)

    tm_l = min(1024, n_lat)
    tpb = n_lat // tm_l
    x2d = x.reshape(batch * n_lat, d)
    pf_l, pb_l, lg_l = _proj(x2d, mod3, lambda i: i // tpb, w_in_t, w_diff, w_lr, w_g, b_g, cos_t, sin_t,
                             rope=True, tm=tm_l, n_tab_tiles=tpb)
    ctx2d = ctx.reshape(batch * n_ctx, d)
    tm_c = min(1024, batch * n_ctx)
    pf_c, pb_c, lg_c = _proj(ctx2d, mod3, lambda i: batch, w_in_t, w_diff, w_lr, w_g, b_g, cos_t, sin_t,
                             rope=False, tm=tm_c, n_tab_tiles=1)

    gla_o = _gla(pf_l, lg_l, pf_c, lg_c, gla_norm_g[l], batch=batch, n_lat=n_lat, n_ctx=n_ctx)
    diff_o = _diff(pb_l, pb_c, diff_lambda[l], diff_norm_g[l], batch=batch, n_lat=n_lat, n_ctx=n_ctx)

    res, h_packed, aff_t = _out(gla_o, diff_o, x2d, w_o[l].astype(BF16), mod3, ln1_g[l], ln1_b[l],
                                w_router[l].T, batch=batch, n_lat=n_lat)

    idx, gate = _route(aff_t, cap=cap)
    rows = batch * cap
    idx_flat = idx.transpose(1, 0, 2).reshape(n_e * rows)
    gate_col = gate.transpose(1, 0, 2).reshape(n_e, rows, 1)
    pre_ln = _moe(idx_flat, gate_col, h_packed, w1[l], w3[l], w2[l], mod3, res, rows=rows, cap=cap)

    out = _final(pre_ln, ln2_g[l], ln2_b[l], n_lat=n_lat)
    return out.reshape(batch, n_lat, d)
```

```python
import functools
import math

import jax
import jax.numpy as jnp
import numpy as np
from jax import lax
from jax.experimental import pallas as pl
from jax.experimental.pallas import tpu as pltpu

F32 = jnp.float32
BF16 = jnp.bfloat16
I32 = jnp.int32

GRID_W = 64
GLA_HEADS = 4
GLA_RANK = 16
GLA_TAU = 16.0
GLA_CHUNK = 64
DIFF_HEADS = 8
DIFF_DH = 64
ROPE_BASE = 10000.0
EC_CAPACITY_FACTOR = 2
LN_EPS = 1e-5
RMS_EPS = 1e-6
DEPTH = 1
ALPHA = (2.0 * DEPTH) ** 0.25
LAM_INIT = 0.8 - 0.6 * math.exp(-0.3 * 0)

LANES = 128
ADA_ROWS = 8
VMEM_LIMIT = 56 * 1024 * 1024

_NT = (((1,), (1,)), ((), ()))
_TN = (((0,), (0,)), ((), ()))


def _cparams(sem):
    return pltpu.CompilerParams(dimension_semantics=sem, vmem_limit_bytes=VMEM_LIMIT)


def _silu(a):
    return a * (1.0 / (1.0 + jnp.exp(-a)))


def _log_sigmoid(z):
    return jnp.minimum(z, 0.0) - jnp.log1p(jnp.exp(-jnp.abs(z)))


def _ada_kernel(c_ref, w_ref, b_ref, o_ref):
    s = _silu(c_ref[...]).astype(BF16)
    o_ref[...] = jnp.dot(s, w_ref[...].astype(BF16), preferred_element_type=F32) + b_ref[...]


def _ada(c_rows, w_ada, b_ada):
    d, n_out = w_ada.shape
    tn = min(1024, n_out)
    return pl.pallas_call(
        _ada_kernel,
        grid=(n_out // tn,),
        in_specs=[pl.BlockSpec((ADA_ROWS, d), lambda j: (0, 0)),
                  pl.BlockSpec((d, tn), lambda j: (0, j)),
                  pl.BlockSpec((1, tn), lambda j: (0, j))],
        out_specs=pl.BlockSpec((ADA_ROWS, tn), lambda j: (0, j)),
        out_shape=jax.ShapeDtypeStruct((ADA_ROWS, n_out), F32),
        compiler_params=_cparams(("arbitrary",)),
    )(c_rows, w_ada, b_ada.reshape(1, n_out))


def _proj_kernel(x_ref, sh_ref, sc_ref, wa_ref, wb_ref, wlr_ref, wg_ref, bg_ref, cos_ref, sin_ref,
                 of_ref, ob_ref, lg_ref, u_ref, *, rope, n_f32, n_q, n_rope):
    n = pl.program_id(1)

    @pl.when(n == 0)
    def _():
        u = (x_ref[...] * (1.0 + sc_ref[...]) + sh_ref[...]).astype(BF16)
        u_ref[...] = u
        lr = lax.dot_general(u, wlr_ref[...], _NT, preferred_element_type=F32)
        z = jnp.dot(lr.astype(BF16), wg_ref[...], preferred_element_type=F32) + bg_ref[...]
        lg_ref[...] = _log_sigmoid(z) * (1.0 / GLA_TAU)

    @pl.when(n < n_f32)
    def _():
        of_ref[...] = lax.dot_general(u_ref[...], wa_ref[...], _NT, preferred_element_type=F32)

    @pl.when(jnp.logical_and(n >= n_f32, n < n_f32 + n_rope))
    def _():
        acc = lax.dot_general(u_ref[...], wb_ref[...], _NT, preferred_element_type=F32)
        scale = jnp.where(n < n_f32 + n_q, DIFF_DH ** -0.5, 1.0).astype(F32)
        if rope:
            lane = lax.broadcasted_iota(I32, (acc.shape[0], LANES), 1)
            first = (lane % DIFF_DH) < (DIFF_DH // 2)
            cs = cos_ref[...]
            sn = sin_ref[...]
            for j in range(acc.shape[1] // LANES):
                a = acc[:, j * LANES:(j + 1) * LANES]
                partner = jnp.where(first, pltpu.roll(a, LANES - DIFF_DH // 2, 1), pltpu.roll(a, DIFF_DH // 2, 1))
                ob_ref[:, j * LANES:(j + 1) * LANES] = ((a * cs + partner * sn) * scale).astype(BF16)
        else:
            ob_ref[...] = (acc * scale).astype(BF16)

    @pl.when(n >= n_f32 + n_rope)
    def _():
        ob_ref[...] = lax.dot_general(u_ref[...], wb_ref[...], _NT, preferred_element_type=F32).astype(BF16)


def _proj(x2d, mod3, row_of_tile, w_in, w_diff, w_lr, w_g, b_g, cos_t, sin_t, *, rope, tm, n_tab_tiles):
    m, d = x2d.shape
    tn = 512
    n_f32 = w_diff.shape[0] // tn
    n_tiles = 2 * n_f32
    n_q = n_tiles // 6
    n_rope = n_tiles // 3
    r2 = w_lr.shape[0]
    ng = w_g.shape[1]
    kern = functools.partial(_proj_kernel, rope=rope, n_f32=n_f32, n_q=n_q, n_rope=n_rope)
    return pl.pallas_call(
        kern,
        grid=(m // tm, n_tiles),
        in_specs=[pl.BlockSpec((tm, d), lambda i, j: (i, 0)),
                  pl.BlockSpec((None, 1, d), lambda i, j: (row_of_tile(i), 0, 0)),
                  pl.BlockSpec((None, 1, d), lambda i, j: (row_of_tile(i), 0, 1)),
                  pl.BlockSpec((tn, d), lambda i, j: (jnp.minimum(j, n_f32 - 1), 0)),
                  pl.BlockSpec((tn, d), lambda i, j: (jnp.maximum(j - n_f32, 0), 0)),
                  pl.BlockSpec((r2, d), lambda i, j: (0, 0)),
                  pl.BlockSpec((r2, ng), lambda i, j: (0, 0)),
                  pl.BlockSpec((1, ng), lambda i, j: (0, 0)),
                  pl.BlockSpec((tm, LANES), lambda i, j: (i % n_tab_tiles, 0)),
                  pl.BlockSpec((tm, LANES), lambda i, j: (i % n_tab_tiles, 0))],
        out_specs=[pl.BlockSpec((tm, tn), lambda i, j: (i, jnp.minimum(j, n_f32 - 1))),
                   pl.BlockSpec((tm, tn), lambda i, j: (i, jnp.maximum(j - n_f32, 0))),
                   pl.BlockSpec((tm, ng), lambda i, j: (i, 0))],
        out_shape=[jax.ShapeDtypeStruct((m, n_f32 * tn), F32),
                   jax.ShapeDtypeStruct((m, (n_tiles - n_f32) * tn), BF16),
                   jax.ShapeDtypeStruct((m, ng), F32)],
        scratch_shapes=[pltpu.VMEM((tm, d), BF16)],
        compiler_params=_cparams(("arbitrary", "arbitrary")),
    )(x2d, mod3, mod3, w_in, w_diff, w_lr, w_g, b_g, cos_t, sin_t)


GLA_GROUP = 4
GLA_UNROLL = 2


def _split3(a):
    hi = a.astype(BF16)
    r1 = a - hi.astype(F32)
    mid = r1.astype(BF16)
    return hi, mid, (r1 - mid.astype(F32)).astype(BF16)


def _gla_groups(dirs, *, want_out, scale):
    c = GLA_CHUNK
    rows, dk = dirs[0]["k"].shape
    n_chunks = rows // c
    for d in dirs:
        d["parts"] = jnp.dot(d["sum_mat"], jnp.concatenate(_split3(d["g"]), axis=1),
                             preferred_element_type=F32)
    for d in dirs:
        parts = d["parts"]
        sums = (parts[:, :dk] + parts[:, dk:2 * dk]) + parts[:, 2 * dk:]
        bcum = sums[:rows]
        btot = sums[rows:]
        d["vb"] = d["v"].astype(BF16)
        d["k_out"] = (d["k"] * jnp.exp(btot - bcum)).astype(BF16)
        d["decay"] = jnp.exp(btot)
        if want_out:
            d["q_in"] = ((d["q"] * scale) * jnp.exp(bcum)).astype(BF16)
            d["k_in"] = (d["k"] * jnp.exp(-bcum)).astype(BF16)
    if want_out:
        for d in dirs:
            d["att"] = lax.dot_general(d["q_in"], d["k_in"], _NT, preferred_element_type=F32)
        for d in dirs:
            d["att"] = jnp.where(d["causal"], d["att"], 0.0).astype(BF16)
        for d in dirs:
            d["intra"] = jnp.dot(d["att"], d["vb"], preferred_element_type=F32)
    chunk = lambda ci: slice(ci * c, (ci + 1) * c)
    for d in dirs:
        d["upd"] = [lax.dot_general(d["vb"][chunk(ci)], d["k_out"][chunk(ci)], _TN, preferred_element_type=F32)
                    for ci in range(n_chunks)]
    states = {}
    for d in dirs:
        st = states.get(id(d["st_ref"]))
        if st is None:
            st = d["st_ref"][...]
        d["st_in"] = [None] * n_chunks
        for ci in (range(n_chunks) if d["forward"] else reversed(range(n_chunks))):
            d["st_in"][ci] = st
            st = st * d["decay"][ci * c:ci * c + 1] + d["upd"][ci]
        states[id(d["st_ref"])] = st
    for d in dirs:
        d["st_ref"][...] = states[id(d["st_ref"])]
    if not want_out:
        return [None for _ in dirs]
    outs = []
    for d in dirs:
        outs.append(jnp.concatenate(
            [d["intra"][chunk(ci)] + lax.dot_general(d["q_in"][chunk(ci)], d["st_in"][ci].astype(BF16), _NT,
                                                     preferred_element_type=F32)
             for ci in range(n_chunks)], axis=0))
    return outs


def _gla_kernel(q_ref, k_ref, v_ref, r_ref, gf_ref, gb_ref, kc_ref, vc_ref, gfc_ref, gbc_ref, ng_ref,
                o_ref, of_ref, obk_ref, sf_ref, sb_ref, *, n_lat, n_ctx, dk):
    c = GLA_CHUNK
    scale = dk ** -0.5
    sf_ref[...] = jnp.zeros_like(sf_ref)
    sb_ref[...] = jnp.zeros_like(sb_ref)

    def masks(rows):
        row = lax.broadcasted_iota(I32, (rows, rows), 0)
        col = lax.broadcasted_iota(I32, (rows, rows), 1)
        same = (row // c) == (col // c)
        lower = jnp.logical_and(same, row >= col)
        upper = jnp.logical_and(same, row <= col)
        total = same.astype(BF16)
        return (lower, jnp.concatenate([lower.astype(BF16), total], axis=0),
                upper, jnp.concatenate([upper.astype(BF16), total], axis=0))

    def scan(n, refs_f, refs_b, want_out):
        rows = min(GLA_GROUP * c, n)
        lower, sum_f, upper, sum_b = masks(rows)
        n_groups = n // rows
        unroll = GLA_UNROLL if n_groups % GLA_UNROLL == 0 else 1
        ld = lambda ref, r0: None if ref is None else ref[pl.ds(r0, rows), :]
        names = ("q", "k", "v", "g")

        def body(i, carry):
            items, stores = [], []
            for u in range(unroll):
                rf = pl.multiple_of((i * unroll + u) * rows, rows)
                rb = pl.multiple_of((n_groups - 1 - (i * unroll + u)) * rows, rows)
                items.append(dict(zip(names, (ld(r, rf) for r in refs_f)), st_ref=sf_ref, sum_mat=sum_f,
                                  causal=lower, forward=True))
                items.append(dict(zip(names, (ld(r, rb) for r in refs_b)), st_ref=sb_ref, sum_mat=sum_b,
                                  causal=upper, forward=False))
                stores += [(of_ref, rf), (obk_ref, rb)]
            outs = _gla_groups(items, want_out=want_out, scale=scale)
            if want_out:
                for (ref, r0), out in zip(stores, outs):
                    ref[pl.ds(r0, rows), :] = out
            return carry

        lax.fori_loop(0, n_groups // unroll, body, 0)

    scan(n_ctx, (None, kc_ref, vc_ref, gfc_ref), (None, kc_ref, vc_ref, gbc_ref), False)
    scan(n_lat, (q_ref, k_ref, v_ref, gf_ref), (q_ref, k_ref, v_ref, gb_ref), True)

    tb = min(512, n_lat)

    def fin_body(i, carry):
        r0 = pl.multiple_of(i * tb, tb)
        o = of_ref[pl.ds(r0, tb), :] + obk_ref[pl.ds(r0, tb), :]
        ms = jnp.mean(o * o, axis=-1, keepdims=True)
        o = o * lax.rsqrt(ms + RMS_EPS) * ng_ref[...]
        o_ref[pl.ds(r0, tb), :] = (o * _silu(r_ref[pl.ds(r0, tb), :])).astype(o_ref.dtype)
        return carry

    lax.fori_loop(0, n_lat // tb, fin_body, 0)


def _gla(pf_l, lg_l, pf_c, lg_c, norm_g, *, batch, n_lat, n_ctx):
    h = GLA_HEADS
    dk = pf_l.shape[1] // 6 // h
    dv = 2 * dk
    kern = functools.partial(_gla_kernel, n_lat=n_lat, n_ctx=n_ctx, dk=dk)
    lat = lambda width, off: pl.BlockSpec((n_lat, width), lambda b, i: (b, off + i))
    ctx = lambda width, off: pl.BlockSpec((n_ctx, width), lambda b, i: (b, off + i))
    return pl.pallas_call(
        kern,
        grid=(batch, h),
        in_specs=[lat(dk, 0), lat(dk, h), lat(dv, h), lat(dv, 2 * h),
                  lat(dk, 0), lat(dk, h),
                  ctx(dk, h), ctx(dv, h), ctx(dk, 0), ctx(dk, h),
                  pl.BlockSpec((1, dv), lambda b, i: (0, 0))],
        out_specs=pl.BlockSpec((n_lat, dv), lambda b, i: (b, i)),
        out_shape=jax.ShapeDtypeStruct((batch * n_lat, h * dv), BF16),
        scratch_shapes=[pltpu.VMEM((n_lat, dv), F32), pltpu.VMEM((n_lat, dv), F32),
                        pltpu.VMEM((dv, dk), F32), pltpu.VMEM((dv, dk), F32)],
        compiler_params=_cparams(("arbitrary", "arbitrary")),
    )(pf_l, pf_l, pf_l, pf_l, lg_l, lg_l, pf_c, pf_c, lg_c, lg_c, norm_g.reshape(1, dv))


DIFF_KEY_CHUNK = 512


def _diff_step(q_ref, kc_ref, kl_ref, vc_ref, vl_ref, lam_ref, ng_ref, o_ref, s_w, m_w, s_r, m_r):
    tq = o_ref.shape[0]
    dv = o_ref.shape[1]
    q = q_ref[...]
    lane = lax.broadcasted_iota(I32, q.shape, 1)
    zero = jnp.zeros_like(q)
    qq = jnp.concatenate([jnp.where(lane < DIFF_DH, q, zero), jnp.where(lane >= DIFF_DH, q, zero)], axis=0)
    mx = jnp.max(m_r[...], axis=-1, keepdims=True)
    m_new = jnp.full((2 * tq, LANES), -jnp.inf, F32)
    acc = jnp.zeros((2 * tq, 2 * dv), F32)
    col = 0
    for k_ref, v_ref in ((kc_ref, vc_ref), (kl_ref, vl_ref)):
        n_keys = k_ref.shape[0]
        w = min(DIFF_KEY_CHUNK, n_keys)
        ones_col = (lax.broadcasted_iota(I32, (w, dv), 1) == 0).astype(BF16)
        for r0 in range(0, n_keys, w):
            s_new = lax.dot_general(qq, k_ref[r0:r0 + w, :], _NT, preferred_element_type=F32)
            s_w[:, col:col + w] = s_new
            for t in range(w // LANES):
                m_new = jnp.maximum(m_new, s_new[:, t * LANES:(t + 1) * LANES])
            e = jnp.exp(s_r[:, col:col + w] - mx).astype(BF16)
            v_ext = jnp.concatenate([v_ref[r0:r0 + w, :], ones_col], axis=1)
            acc = acc + jnp.dot(e, v_ext, preferred_element_type=F32)
            col += w
    m_w[...] = m_new
    dl = lam_ref[...]
    lam = (jnp.exp(jnp.sum(dl[0:1] * dl[1:2], axis=-1, keepdims=True))
           - jnp.exp(jnp.sum(dl[2:3] * dl[3:4], axis=-1, keepdims=True)) + LAM_INIT)
    inv = 1.0 / acc[:, dv:dv + 1]
    o = acc[:tq, :dv] * inv[:tq] - acc[tq:, :dv] * (inv[tq:] * lam)
    ms = jnp.mean(o * o, axis=-1, keepdims=True)
    o_ref[...] = (o * lax.rsqrt(ms + RMS_EPS) * ng_ref[...] * (1.0 - LAM_INIT)).astype(o_ref.dtype)


def _diff_kernel(q_ref, kl_ref, kc_ref, vl_ref, vc_ref, lam_ref, ng_ref, o_ref, sa_ref, sb_ref, ma_ref, mb_ref):
    j = pl.program_id(1)

    @pl.when(j == 0)
    def _():
        sb_ref[...] = jnp.zeros_like(sb_ref)
        mb_ref[...] = jnp.zeros_like(mb_ref)

    args = (q_ref, kc_ref, kl_ref, vc_ref, vl_ref, lam_ref, ng_ref, o_ref)

    @pl.when(j % 2 == 0)
    def _():
        _diff_step(*args, sa_ref, ma_ref, sb_ref, mb_ref)

    @pl.when(j % 2 == 1)
    def _():
        _diff_step(*args, sb_ref, mb_ref, sa_ref, ma_ref)


def _diff(pb_l, pb_c, diff_lambda, norm_g, *, batch, n_lat, n_ctx):
    h = DIFF_HEADS
    dv = 2 * DIFF_DH
    tq = min(256, n_lat)
    nq = n_lat // tq
    nblk = h * nq
    cur = lambda j: jnp.minimum(j, nblk - 1)
    prev = lambda j: jnp.maximum(j - 1, 0)
    return pl.pallas_call(
        _diff_kernel,
        grid=(batch, nblk + 1),
        in_specs=[pl.BlockSpec((tq, dv), lambda b, j: (b * nq + cur(j) % nq, cur(j) // nq)),
                  pl.BlockSpec((n_lat, dv), lambda b, j: (b, h + cur(j) // nq)),
                  pl.BlockSpec((n_ctx, dv), lambda b, j: (b, h + cur(j) // nq)),
                  pl.BlockSpec((n_lat, dv), lambda b, j: (b, 2 * h + prev(j) // nq)),
                  pl.BlockSpec((n_ctx, dv), lambda b, j: (b, 2 * h + prev(j) // nq)),
                  pl.BlockSpec(diff_lambda.shape, lambda b, j: (0, 0)),
                  pl.BlockSpec((1, dv), lambda b, j: (0, 0))],
        out_specs=pl.BlockSpec((tq, dv), lambda b, j: (b * nq + prev(j) % nq, prev(j) // nq)),
        out_shape=jax.ShapeDtypeStruct((batch * n_lat, h * dv), BF16),
        scratch_shapes=[pltpu.VMEM((2 * tq, n_ctx + n_lat), F32), pltpu.VMEM((2 * tq, n_ctx + n_lat), F32),
                        pltpu.VMEM((2 * tq, LANES), F32), pltpu.VMEM((2 * tq, LANES), F32)],
        compiler_params=_cparams(("arbitrary", "arbitrary")),
    )(pb_l, pb_l, pb_c, pb_l, pb_c, diff_lambda, norm_g.reshape(1, dv))


def _layer_norm(y, g, b):
    mu = jnp.mean(y, axis=-1, keepdims=True)
    yc = y - mu
    var = jnp.mean(yc * yc, axis=-1, keepdims=True)
    return yc * lax.rsqrt(var + LN_EPS) * g + b


def _split2(a):
    hi = a.astype(BF16)
    return hi, (a - hi.astype(F32)).astype(BF16)


def _out_kernel(ga_ref, df_ref, x_ref, wo_ref, g1_ref, sh2_ref, sc2_ref, lg_ref, lb_ref, wr_ref,
                x1_ref, h_ref, aff_ref):
    half = ga_ref.shape[1]
    o = (jnp.dot(ga_ref[...], wo_ref[:half, :], preferred_element_type=F32)
         + jnp.dot(df_ref[...], wo_ref[half:, :], preferred_element_type=F32))
    x1 = _layer_norm(ALPHA * x_ref[...] + g1_ref[...] * o, lg_ref[...], lb_ref[...])
    x1_ref[...] = ALPHA * x1
    hmod = x1 * (1.0 + sc2_ref[...]) + sh2_ref[...]
    hbits = pltpu.bitcast(hmod.astype(BF16).astype(F32), I32)
    hw = h_ref.shape[1]
    h_ref[...] = (hbits[:, hw:] & jnp.int32(-65536)) | lax.shift_right_logical(hbits[:, :hw], 16)
    h_hi, h_lo = _split2(hmod)
    w_hi, w_lo = _split2(wr_ref[...])
    logits = (lax.dot_general(w_hi, h_hi, _NT, preferred_element_type=F32)
              + lax.dot_general(w_lo, h_hi, _NT, preferred_element_type=F32)
              + lax.dot_general(w_hi, h_lo, _NT, preferred_element_type=F32))
    e = jnp.exp(logits - jnp.max(logits, axis=0, keepdims=True))
    aff_ref[...] = e / jnp.sum(e, axis=0, keepdims=True)


def _out(gla_o, diff_o, x2d, w_o, mod3, ln_g, ln_b, w_router_t, *, batch, n_lat):
    m, d = x2d.shape
    half = gla_o.shape[1]
    n_e = w_router_t.shape[0]
    tm = min(512, n_lat)
    tpb = n_lat // tm
    modspec = lambda k: pl.BlockSpec((None, 1, d), lambda i: (i // tpb, 0, k))
    vec = pl.BlockSpec((1, d), lambda i: (0, 0))
    return pl.pallas_call(
        _out_kernel,
        grid=(m // tm,),
        in_specs=[pl.BlockSpec((tm, half), lambda i: (i, 0)),
                  pl.BlockSpec((tm, half), lambda i: (i, 0)),
                  pl.BlockSpec((tm, d), lambda i: (i, 0)),
                  pl.BlockSpec((2 * half, d), lambda i: (0, 0), pipeline_mode=pl.Buffered(1)),
                  modspec(2), modspec(3), modspec(4), vec, vec,
                  pl.BlockSpec((n_e, d), lambda i: (0, 0))],
        out_specs=[pl.BlockSpec((tm, d), lambda i: (i, 0)),
                   pl.BlockSpec((tm, d // 2), lambda i: (i, 0)),
                   pl.BlockSpec((None, n_e, tm), lambda i: (i // tpb, 0, i % tpb))],
        out_shape=[jax.ShapeDtypeStruct((m, d), F32),
                   jax.ShapeDtypeStruct((m, d // 2), I32),
                   jax.ShapeDtypeStruct((batch, n_e, n_lat), F32)],
        compiler_params=_cparams(("arbitrary",)),
    )(gla_o, diff_o, x2d, w_o, mod3, mod3, mod3, ln_g.reshape(1, d), ln_b.reshape(1, d), w_router_t)


def _prefix_count(mask_f32, strict_upper):
    rows, n = mask_f32.shape
    carry = jnp.zeros((rows, 1), F32)
    pieces = []
    for j in range(n // LANES):
        blk = mask_f32[:, j * LANES:(j + 1) * LANES]
        within = jnp.dot(blk.astype(BF16), strict_upper, preferred_element_type=F32)
        pieces.append(within + carry)
        carry = carry + jnp.sum(blk, axis=-1, keepdims=True)
    return jnp.concatenate(pieces, axis=-1)


def _route_kernel(aff_ref, idx_ref, gate_ref, pos_ref, *, cap):
    b = pl.program_id(0)
    v = aff_ref[...]
    n_e, n = v.shape
    bits = pltpu.bitcast(v, I32)

    def search(i, t):
        cand = t | (1 << (30 - i))
        cnt = jnp.sum((bits >= cand).astype(I32), axis=-1, keepdims=True)
        return jnp.where(cnt >= cap, cand, t)

    thr = lax.fori_loop(0, 31, search, jnp.zeros((n_e, 1), I32))
    gt = bits > thr
    eq = bits == thr
    need = (cap - jnp.sum(gt.astype(I32), axis=-1, keepdims=True)).astype(F32)
    r = lax.broadcasted_iota(I32, (LANES, LANES), 0)
    cidx = lax.broadcasted_iota(I32, (LANES, LANES), 1)
    strict_upper = (r < cidx).astype(BF16)
    eq_rank = _prefix_count(eq.astype(F32), strict_upper)
    sel = jnp.logical_or(gt, jnp.logical_and(eq, eq_rank < need))
    pos = _prefix_count(sel.astype(F32), strict_upper)
    pos_ref[...] = jnp.where(sel, pos, -1.0).astype(I32)

    tok = lax.broadcasted_iota(I32, (8, n), 1)
    srow = lax.broadcasted_iota(I32, (8, n), 0)
    tok_hi = (tok >> 6).astype(F32)
    tok_lo = (tok & 63).astype(F32)
    slot = lax.broadcasted_iota(I32, (cap, n), 0)

    def compact(e, carry):
        ve = aff_ref[pl.ds(e, 1), :]
        v_hi = ve.astype(BF16).astype(F32)
        r1 = ve - v_hi
        v_mid = r1.astype(BF16).astype(F32)
        v_lo = r1 - v_mid
        lhs = jnp.where(srow == 0, tok_hi,
              jnp.where(srow == 1, tok_lo,
              jnp.where(srow == 2, v_hi,
              jnp.where(srow == 3, v_mid,
              jnp.where(srow == 4, v_lo, 0.0))))).astype(BF16)
        onehot = (slot == pos_ref[pl.ds(e, 1), :]).astype(F32).astype(BF16)
        res = lax.dot_general(lhs, onehot, _NT, preferred_element_type=F32)
        idx_ref[pl.ds(e, 1), :] = (res[0:1] * 64.0 + res[1:2]).astype(I32) + b * n
        gate_ref[pl.ds(e, 1), :] = (res[2:3] + res[3:4]) + res[4:5]
        return carry

    lax.fori_loop(0, n_e, compact, 0)


def _route(aff_t, *, cap):
    batch, n_e, n = aff_t.shape
    kern = functools.partial(_route_kernel, cap=cap)
    return pl.pallas_call(
        kern,
        grid=(batch,),
        in_specs=[pl.BlockSpec((None, n_e, n), lambda b: (b, 0, 0))],
        out_specs=[pl.BlockSpec((None, n_e, cap), lambda b: (b, 0, 0)),
                   pl.BlockSpec((None, n_e, cap), lambda b: (b, 0, 0))],
        out_shape=[jax.ShapeDtypeStruct((batch, n_e, cap), I32),
                   jax.ShapeDtypeStruct((batch, n_e, cap), F32)],
        scratch_shapes=[pltpu.VMEM((n_e, n), I32)],
        compiler_params=_cparams(("arbitrary",)),
    )(aff_t)


DMA_UNROLL = 8


def _moe_kernel(idx_ref, gate_ref, h_hbm, w1_ref, w3_ref, w2_ref, g2_ref, facc_in, facc_hbm,
                stage_ref, acc_ref, gsem, rsem, wsem, *, rows, cap):
    del facc_in
    e = pl.program_id(0)
    f = pl.program_id(1)
    n_e = pl.num_programs(0)
    nf = pl.num_programs(1)
    slot = e % 2

    def issue_rows(make):
        def body(i, carry):
            for u in range(DMA_UNROLL):
                make(i * DMA_UNROLL + u).start()
            return carry
        lax.fori_loop(0, rows // DMA_UNROLL, body, 0)

    def x_row(ex, s):
        return lambda k: pltpu.make_async_copy(
            h_hbm.at[pl.ds(idx_ref[ex * rows + k], 1), :], stage_ref.at[s, pl.ds(k, 1), :], gsem.at[s])

    def acc_row_in(k):
        return pltpu.make_async_copy(facc_hbm.at[pl.ds(idx_ref[e * rows + k], 1), :], acc_ref.at[pl.ds(k, 1), :], rsem)

    def acc_row_out(k):
        return pltpu.make_async_copy(acc_ref.at[pl.ds(k, 1), :], facc_hbm.at[pl.ds(idx_ref[e * rows + k], 1), :], wsem)

    def wait_x(s):
        pltpu.make_async_copy(h_hbm.at[pl.ds(0, rows), :], stage_ref.at[s], gsem.at[s]).wait()

    def wait_acc_in():
        pltpu.make_async_copy(facc_hbm.at[pl.ds(0, rows), :], acc_ref, rsem).wait()

    def wait_acc_out():
        pltpu.make_async_copy(acc_ref, facc_hbm.at[pl.ds(0, rows), :], wsem).wait()

    @pl.when(f == 0)
    def _():
        @pl.when(e > 0)
        def _():
            wait_acc_out()
        issue_rows(acc_row_in)

        @pl.when(e == 0)
        def _():
            issue_rows(x_row(0, 0))
        wait_x(slot)

        @pl.when(e + 1 < n_e)
        def _():
            issue_rows(x_row(e + 1, 1 - slot))
        wait_acc_in()

    u = stage_ref[slot]
    half = u.shape[1]
    x_lo = pltpu.bitcast(u << 16, F32).astype(BF16)
    x_hi = pltpu.bitcast(u & jnp.int32(-65536), F32).astype(BF16)
    tf = w1_ref.shape[1]
    w13 = jnp.concatenate([w1_ref[...], w3_ref[...]], axis=1).astype(BF16)
    ag = (jnp.dot(x_lo, w13[:half], preferred_element_type=F32)
          + jnp.dot(x_hi, w13[half:], preferred_element_type=F32))
    hid = (_silu(ag[:, :tf]) * ag[:, tf:]).astype(BF16)
    gate = gate_ref[...]
    d = acc_ref.shape[1]
    tc = min(512, d)
    for j in range(d // tc):
        cols = slice(j * tc, (j + 1) * tc)
        y = gate * jnp.dot(hid, w2_ref[:, cols].astype(BF16), preferred_element_type=F32)
        for b in range(rows // cap):
            rb = slice(b * cap, (b + 1) * cap)
            acc_ref[rb, cols] += y[rb] * g2_ref[b][:, cols]

    @pl.when(f == nf - 1)
    def _():
        issue_rows(acc_row_out)

        @pl.when(e == n_e - 1)
        def _():
            wait_acc_out()


def _moe(idx_flat, gate_col, h_packed, w1, w3, w2, mod3, facc0, *, rows, cap):
    m, d = facc0.shape
    n_e, _, dff = w1.shape
    tf = min(256, dff)
    kern = functools.partial(_moe_kernel, rows=rows, cap=cap)
    grid_spec = pltpu.PrefetchScalarGridSpec(
        num_scalar_prefetch=1,
        grid=(n_e, dff // tf),
        in_specs=[pl.BlockSpec((None, rows, 1), lambda e, f, idx: (e, 0, 0)),
                  pl.BlockSpec(memory_space=pl.ANY),
                  pl.BlockSpec((None, d, tf), lambda e, f, idx: (e, 0, f)),
                  pl.BlockSpec((None, d, tf), lambda e, f, idx: (e, 0, f)),
                  pl.BlockSpec((None, tf, d), lambda e, f, idx: (e, f, 0)),
                  pl.BlockSpec((ADA_ROWS, 1, d), lambda e, f, idx: (0, 0, 5)),
                  pl.BlockSpec(memory_space=pl.ANY)],
        out_specs=pl.BlockSpec(memory_space=pl.ANY),
        scratch_shapes=[pltpu.VMEM((2, rows, d // 2), I32), pltpu.VMEM((rows, d), F32),
                        pltpu.SemaphoreType.DMA((2,)), pltpu.SemaphoreType.DMA(()), pltpu.SemaphoreType.DMA(())],
    )
    return pl.pallas_call(
        kern,
        grid_spec=grid_spec,
        out_shape=jax.ShapeDtypeStruct((m, d), F32),
        input_output_aliases={7: 0},
        compiler_params=_cparams(("arbitrary", "arbitrary")),
    )(idx_flat, gate_col, h_packed, w1, w3, w2, mod3, facc0)


def _final_kernel(y_ref, lg_ref, lb_ref, o_ref):
    o_ref[...] = _layer_norm(y_ref[...], lg_ref[...], lb_ref[...])


def _final(pre_ln, ln_g, ln_b, *, n_lat):
    m, d = pre_ln.shape
    tm = min(512, n_lat)
    vec = pl.BlockSpec((1, d), lambda i: (0, 0))
    return pl.pallas_call(
        _final_kernel,
        grid=(m // tm,),
        in_specs=[pl.BlockSpec((tm, d), lambda i: (i, 0)), vec, vec],
        out_specs=pl.BlockSpec((tm, d), lambda i: (i, 0)),
        out_shape=jax.ShapeDtypeStruct((m, d), F32),
        compiler_params=_cparams(("arbitrary",)),
    )(pre_ln, ln_g.reshape(1, d), ln_b.reshape(1, d))


def _rope_tables(n_lat):
    rows = n_lat // GRID_W
    row = jnp.repeat(jnp.arange(rows, dtype=F32), GRID_W)
    col = jnp.tile(jnp.arange(GRID_W, dtype=F32), rows)
    n_freq = DIFF_DH // 4
    inv = ROPE_BASE ** (-jnp.arange(n_freq, dtype=F32) / n_freq)
    ang = jnp.concatenate([row[:, None] * inv, col[:, None] * inv], axis=-1)
    cos, sin = jnp.cos(ang), jnp.sin(ang)
    reps = LANES // DIFF_DH
    cos_t = jnp.tile(jnp.concatenate([cos, cos], axis=-1), (1, reps))
    sin_t = jnp.tile(jnp.concatenate([-sin, sin], axis=-1), (1, reps))
    return cos_t, sin_t


def kernel(x, c, ctx, c_ctx, w_ada, b_ada, w_in, w_gate2, b_gate, gla_norm_g, diff_lambda, diff_norm_g, w_o,
           ln1_g, ln1_b, w_router, w1, w3, w2, ln2_g, ln2_b):
    batch, n_lat, d = x.shape
    n_ctx = ctx.shape[1]
    assert w_ada.shape[0] == DEPTH == 1 and batch < ADA_ROWS
    n_e = w_router.shape[-1]
    cap = EC_CAPACITY_FACTOR * n_lat // n_e
    l = 0

    c_rows = jnp.concatenate([c, c_ctx[None, :], jnp.zeros((ADA_ROWS - batch - 1, d), F32)], axis=0)
    mod3 = _ada(c_rows, w_ada[l], b_ada[l]).reshape(ADA_ROWS, 1, 6 * d)

    gqkvr = 2 * (d // 4) + 2 * (d // 2)
    r2 = 2 * GLA_RANK
    w_in_t = w_in[l].T.astype(BF16)
    w_diff = w_in_t[gqkvr + r2:]
    w_lr = w_in_t[gqkvr:gqkvr + r2]
    hk = w_gate2.shape[-1]
    zeros = jnp.zeros((GLA_RANK, hk), F32)
    w_g = jnp.concatenate([jnp.concatenate([w_gate2[l, 0], zeros], axis=1),
                           jnp.concatenate([zeros, w_gate2[l, 1]], axis=1)], axis=0).astype(BF16)
    b_g = b_gate[l].reshape(1, 2 * hk)
    cos_t, sin_t = _rope_tables(n_lat)

    tm_l = min(1024, n_lat)
    tpb = n_lat // tm_l
    x2d = x.reshape(batch * n_lat, d)
    pf_l, pb_l, lg_l = _proj(x2d, mod3, lambda i: i // tpb, w_in_t, w_diff, w_lr, w_g, b_g, cos_t, sin_t,
                             rope=True, tm=tm_l, n_tab_tiles=tpb)
    ctx2d = ctx.reshape(batch * n_ctx, d)
    tm_c = min(1024, batch * n_ctx)
    pf_c, pb_c, lg_c = _proj(ctx2d, mod3, lambda i: batch, w_in_t, w_diff, w_lr, w_g, b_g, cos_t, sin_t,
                             rope=False, tm=tm_c, n_tab_tiles=1)

    gla_o = _gla(pf_l, lg_l, pf_c, lg_c, gla_norm_g[l], batch=batch, n_lat=n_lat, n_ctx=n_ctx)
    diff_o = _diff(pb_l, pb_c, diff_lambda[l], diff_norm_g[l], batch=batch, n_lat=n_lat, n_ctx=n_ctx)

    res, h_packed, aff_t = _out(gla_o, diff_o, x2d, w_o[l].astype(BF16), mod3, ln1_g[l], ln1_b[l],
                                w_router[l].T, batch=batch, n_lat=n_lat)

    idx, gate = _route(aff_t, cap=cap)
    rows = batch * cap
    idx_flat = idx.transpose(1, 0, 2).reshape(n_e * rows)
    gate_col = gate.transpose(1, 0, 2).reshape(n_e, rows, 1)
    pre_ln = _moe(idx_flat, gate_col, h_packed, w1[l], w3[l], w2[l], mod3, res, rows=rows, cap=cap)

    out = _final(pre_ln, ln2_g[l], ln2_b[l], n_lat=n_lat)
    return out.reshape(batch, n_lat, d)
```

```python
import functools
import math

import jax
import jax.numpy as jnp
import numpy as np
from jax import lax
from jax.experimental import pallas as pl
from jax.experimental.pallas import tpu as pltpu

F32 = jnp.float32
BF16 = jnp.bfloat16
I32 = jnp.int32

GRID_W = 64
GLA_HEADS = 4
GLA_RANK = 16
GLA_TAU = 16.0
GLA_CHUNK = 64
DIFF_HEADS = 8
DIFF_DH = 64
ROPE_BASE = 10000.0
EC_CAPACITY_FACTOR = 2
LN_EPS = 1e-5
RMS_EPS = 1e-6
DEPTH = 1
ALPHA = (2.0 * DEPTH) ** 0.25
LAM_INIT = 0.8 - 0.6 * math.exp(-0.3 * 0)

LANES = 128
ADA_ROWS = 8
VMEM_LIMIT = 60 * 1024 * 1024

_NT = (((1,), (1,)), ((), ()))
_TN = (((0,), (0,)), ((), ()))


def _cparams(sem):
    return pltpu.CompilerParams(dimension_semantics=sem, vmem_limit_bytes=VMEM_LIMIT)


def _silu(a):
    return a * (1.0 / (1.0 + jnp.exp(-a)))


def _log_sigmoid(z):
    return jnp.minimum(z, 0.0) - jnp.log1p(jnp.exp(-jnp.abs(z)))


def _ada_kernel(c_ref, w_ref, b_ref, o_ref):
    s = _silu(c_ref[...]).astype(BF16)
    o_ref[...] = jnp.dot(s, w_ref[...].astype(BF16), preferred_element_type=F32) + b_ref[...]


def _ada(c_rows, w_ada, b_ada):
    d, n_out = w_ada.shape
    tn = min(1024, n_out)
    return pl.pallas_call(
        _ada_kernel,
        grid=(n_out // tn,),
        in_specs=[pl.BlockSpec((ADA_ROWS, d), lambda j: (0, 0)),
                  pl.BlockSpec((d, tn), lambda j: (0, j)),
                  pl.BlockSpec((1, tn), lambda j: (0, j))],
        out_specs=pl.BlockSpec((ADA_ROWS, tn), lambda j: (0, j)),
        out_shape=jax.ShapeDtypeStruct((ADA_ROWS, n_out), F32),
        compiler_params=_cparams(("arbitrary",)),
    )(c_rows, w_ada, b_ada.reshape(1, n_out))


def _proj_kernel(x_ref, sh_ref, sc_ref, wa_ref, wb_ref, wlr_ref, wg_ref, bg_ref, cos_ref, sin_ref,
                 of_ref, ob_ref, lg_ref, u_ref, *, rope, n_f32, n_q, n_rope):
    n = pl.program_id(1)

    @pl.when(n == 0)
    def _():
        u = (x_ref[...] * (1.0 + sc_ref[...]) + sh_ref[...]).astype(BF16)
        u_ref[...] = u
        lr = lax.dot_general(u, wlr_ref[...], _NT, preferred_element_type=F32)
        z = jnp.dot(lr.astype(BF16), wg_ref[...], preferred_element_type=F32) + bg_ref[...]
        lg_ref[...] = _log_sigmoid(z) * (1.0 / GLA_TAU)

    @pl.when(n < n_f32)
    def _():
        of_ref[...] = lax.dot_general(u_ref[...], wa_ref[...], _NT, preferred_element_type=F32)

    @pl.when(jnp.logical_and(n >= n_f32, n < n_f32 + n_rope))
    def _():
        acc = lax.dot_general(u_ref[...], wb_ref[...], _NT, preferred_element_type=F32)
        scale = jnp.where(n < n_f32 + n_q, DIFF_DH ** -0.5, 1.0).astype(F32)
        if rope:
            lane = lax.broadcasted_iota(I32, (acc.shape[0], LANES), 1)
            first = (lane % DIFF_DH) < (DIFF_DH // 2)
            cs = cos_ref[...]
            sn = sin_ref[...]
            for j in range(acc.shape[1] // LANES):
                a = acc[:, j * LANES:(j + 1) * LANES]
                partner = jnp.where(first, pltpu.roll(a, LANES - DIFF_DH // 2, 1), pltpu.roll(a, DIFF_DH // 2, 1))
                ob_ref[:, j * LANES:(j + 1) * LANES] = ((a * cs + partner * sn) * scale).astype(BF16)
        else:
            ob_ref[...] = (acc * scale).astype(BF16)

    @pl.when(n >= n_f32 + n_rope)
    def _():
        ob_ref[...] = lax.dot_general(u_ref[...], wb_ref[...], _NT, preferred_element_type=F32).astype(BF16)


def _proj(x2d, mod3, row_of_tile, w_in, w_diff, w_lr, w_g, b_g, cos_t, sin_t, *, rope, tm, n_tab_tiles):
    m, d = x2d.shape
    tn = 512
    n_f32 = w_diff.shape[0] // tn
    n_tiles = 2 * n_f32
    n_q = n_tiles // 6
    n_rope = n_tiles // 3
    r2 = w_lr.shape[0]
    ng = w_g.shape[1]
    kern = functools.partial(_proj_kernel, rope=rope, n_f32=n_f32, n_q=n_q, n_rope=n_rope)
    return pl.pallas_call(
        kern,
        grid=(m // tm, n_tiles),
        in_specs=[pl.BlockSpec((tm, d), lambda i, j: (i, 0)),
                  pl.BlockSpec((None, 1, d), lambda i, j: (row_of_tile(i), 0, 0)),
                  pl.BlockSpec((None, 1, d), lambda i, j: (row_of_tile(i), 0, 1)),
                  pl.BlockSpec((tn, d), lambda i, j: (jnp.minimum(j, n_f32 - 1), 0)),
                  pl.BlockSpec((tn, d), lambda i, j: (jnp.maximum(j - n_f32, 0), 0)),
                  pl.BlockSpec((r2, d), lambda i, j: (0, 0)),
                  pl.BlockSpec((r2, ng), lambda i, j: (0, 0)),
                  pl.BlockSpec((1, ng), lambda i, j: (0, 0)),
                  pl.BlockSpec((tm, LANES), lambda i, j: (i % n_tab_tiles, 0)),
                  pl.BlockSpec((tm, LANES), lambda i, j: (i % n_tab_tiles, 0))],
        out_specs=[pl.BlockSpec((tm, tn), lambda i, j: (i, jnp.minimum(j, n_f32 - 1))),
                   pl.BlockSpec((tm, tn), lambda i, j: (i, jnp.maximum(j - n_f32, 0))),
                   pl.BlockSpec((tm, ng), lambda i, j: (i, 0))],
        out_shape=[jax.ShapeDtypeStruct((m, n_f32 * tn), F32),
                   jax.ShapeDtypeStruct((m, (n_tiles - n_f32) * tn), BF16),
                   jax.ShapeDtypeStruct((m, ng), F32)],
        scratch_shapes=[pltpu.VMEM((tm, d), BF16)],
        compiler_params=_cparams(("arbitrary", "arbitrary")),
    )(x2d, mod3, mod3, w_in, w_diff, w_lr, w_g, b_g, cos_t, sin_t)


GLA_GROUP = 4
GLA_UNROLL = 2


def _split3(a):
    hi = a.astype(BF16)
    r1 = a - hi.astype(F32)
    mid = r1.astype(BF16)
    return hi, mid, (r1 - mid.astype(F32)).astype(BF16)


def _gla_groups(dirs, *, want_out, scale):
    c = GLA_CHUNK
    rows, dk = dirs[0]["k"].shape
    n_chunks = rows // c
    for d in dirs:
        d["parts"] = jnp.dot(d["sum_mat"], jnp.concatenate(_split3(d["g"]), axis=1),
                             preferred_element_type=F32)
    for d in dirs:
        parts = d["parts"]
        sums = (parts[:, :dk] + parts[:, dk:2 * dk]) + parts[:, 2 * dk:]
        bcum = sums[:rows]
        btot = sums[rows:]
        d["vb"] = d["v"].astype(BF16)
        d["k_out"] = (d["k"] * jnp.exp(btot - bcum)).astype(BF16)
        d["decay"] = jnp.exp(btot)
        if want_out:
            d["q_in"] = ((d["q"] * scale) * jnp.exp(bcum)).astype(BF16)
            d["k_in"] = (d["k"] * jnp.exp(-bcum)).astype(BF16)
    if want_out:
        for d in dirs:
            d["att"] = lax.dot_general(d["q_in"], d["k_in"], _NT, preferred_element_type=F32)
        for d in dirs:
            d["att"] = jnp.where(d["causal"], d["att"], 0.0).astype(BF16)
        for d in dirs:
            d["intra"] = jnp.dot(d["att"], d["vb"], preferred_element_type=F32)
    chunk = lambda ci: slice(ci * c, (ci + 1) * c)
    for d in dirs:
        d["upd"] = [lax.dot_general(d["vb"][chunk(ci)], d["k_out"][chunk(ci)], _TN, preferred_element_type=F32)
                    for ci in range(n_chunks)]
    states = {}
    for d in dirs:
        st = states.get(id(d["st_ref"]))
        if st is None:
            st = d["st_ref"][...]
        d["st_in"] = [None] * n_chunks
        for ci in (range(n_chunks) if d["forward"] else reversed(range(n_chunks))):
            d["st_in"][ci] = st
            st = st * d["decay"][ci * c:ci * c + 1] + d["upd"][ci]
        states[id(d["st_ref"])] = st
    for d in dirs:
        d["st_ref"][...] = states[id(d["st_ref"])]
    if not want_out:
        return [None for _ in dirs]
    outs = []
    for d in dirs:
        outs.append(jnp.concatenate(
            [d["intra"][chunk(ci)] + lax.dot_general(d["q_in"][chunk(ci)], d["st_in"][ci].astype(BF16), _NT,
                                                     preferred_element_type=F32)
             for ci in range(n_chunks)], axis=0))
    return outs


def _gla_kernel(q_ref, k_ref, v_ref, r_ref, gf_ref, gb_ref, kc_ref, vc_ref, gfc_ref, gbc_ref, ng_ref,
                o_ref, of_ref, obk_ref, sf_ref, sb_ref, *, n_lat, n_ctx, dk):
    c = GLA_CHUNK
    scale = dk ** -0.5
    sf_ref[...] = jnp.zeros_like(sf_ref)
    sb_ref[...] = jnp.zeros_like(sb_ref)

    def masks(rows):
        row = lax.broadcasted_iota(I32, (rows, rows), 0)
        col = lax.broadcasted_iota(I32, (rows, rows), 1)
        same = (row // c) == (col // c)
        lower = jnp.logical_and(same, row >= col)
        upper = jnp.logical_and(same, row <= col)
        total = same.astype(BF16)
        return (lower, jnp.concatenate([lower.astype(BF16), total], axis=0),
                upper, jnp.concatenate([upper.astype(BF16), total], axis=0))

    def scan(n, refs_f, refs_b, want_out):
        rows = min(GLA_GROUP * c, n)
        lower, sum_f, upper, sum_b = masks(rows)
        n_groups = n // rows
        unroll = GLA_UNROLL if n_groups % GLA_UNROLL == 0 else 1
        ld = lambda ref, r0: None if ref is None else ref[pl.ds(r0, rows), :]
        names = ("q", "k", "v", "g")

        def body(i, carry):
            items, stores = [], []
            for u in range(unroll):
                rf = pl.multiple_of((i * unroll + u) * rows, rows)
                rb = pl.multiple_of((n_groups - 1 - (i * unroll + u)) * rows, rows)
                items.append(dict(zip(names, (ld(r, rf) for r in refs_f)), st_ref=sf_ref, sum_mat=sum_f,
                                  causal=lower, forward=True))
                items.append(dict(zip(names, (ld(r, rb) for r in refs_b)), st_ref=sb_ref, sum_mat=sum_b,
                                  causal=upper, forward=False))
                stores += [(of_ref, rf), (obk_ref, rb)]
            outs = _gla_groups(items, want_out=want_out, scale=scale)
            if want_out:
                for (ref, r0), out in zip(stores, outs):
                    ref[pl.ds(r0, rows), :] = out
            return carry

        lax.fori_loop(0, n_groups // unroll, body, 0)

    scan(n_ctx, (None, kc_ref, vc_ref, gfc_ref), (None, kc_ref, vc_ref, gbc_ref), False)
    scan(n_lat, (q_ref, k_ref, v_ref, gf_ref), (q_ref, k_ref, v_ref, gb_ref), True)

    tb = min(512, n_lat)

    def fin_body(i, carry):
        r0 = pl.multiple_of(i * tb, tb)
        o = of_ref[pl.ds(r0, tb), :] + obk_ref[pl.ds(r0, tb), :]
        ms = jnp.mean(o * o, axis=-1, keepdims=True)
        o = o * lax.rsqrt(ms + RMS_EPS) * ng_ref[...]
        o_ref[pl.ds(r0, tb), :] = (o * _silu(r_ref[pl.ds(r0, tb), :])).astype(o_ref.dtype)
        return carry

    lax.fori_loop(0, n_lat // tb, fin_body, 0)


def _gla(pf_l, lg_l, pf_c, lg_c, norm_g, *, batch, n_lat, n_ctx):
    h = GLA_HEADS
    dk = pf_l.shape[1] // 6 // h
    dv = 2 * dk
    kern = functools.partial(_gla_kernel, n_lat=n_lat, n_ctx=n_ctx, dk=dk)
    lat = lambda width, off: pl.BlockSpec((n_lat, width), lambda b, i: (b, off + i))
    ctx = lambda width, off: pl.BlockSpec((n_ctx, width), lambda b, i: (b, off + i))
    return pl.pallas_call(
        kern,
        grid=(batch, h),
        in_specs=[lat(dk, 0), lat(dk, h), lat(dv, h), lat(dv, 2 * h),
                  lat(dk, 0), lat(dk, h),
                  ctx(dk, h), ctx(dv, h), ctx(dk, 0), ctx(dk, h),
                  pl.BlockSpec((1, dv), lambda b, i: (0, 0))],
        out_specs=pl.BlockSpec((n_lat, dv), lambda b, i: (b, i)),
        out_shape=jax.ShapeDtypeStruct((batch * n_lat, h * dv), BF16),
        scratch_shapes=[pltpu.VMEM((n_lat, dv), F32), pltpu.VMEM((n_lat, dv), F32),
                        pltpu.VMEM((dv, dk), F32), pltpu.VMEM((dv, dk), F32)],
        compiler_params=_cparams(("arbitrary", "arbitrary")),
    )(pf_l, pf_l, pf_l, pf_l, lg_l, lg_l, pf_c, pf_c, lg_c, lg_c, norm_g.reshape(1, dv))


DIFF_KEY_CHUNK = 512


def _diff_step(q_ref, kc_ref, kl_ref, vc_ref, vl_ref, lam_ref, ng_ref, o_ref, s_w, m_w, s_r, m_r):
    tq = o_ref.shape[0]
    dv = o_ref.shape[1]
    q = q_ref[...]
    lane = lax.broadcasted_iota(I32, q.shape, 1)
    zero = jnp.zeros_like(q)
    qq = jnp.concatenate([jnp.where(lane < DIFF_DH, q, zero), jnp.where(lane >= DIFF_DH, q, zero)], axis=0)
    mx = jnp.max(m_r[...], axis=-1, keepdims=True)
    m_new = jnp.full((2 * tq, LANES), -jnp.inf, F32)
    acc = jnp.zeros((2 * tq, 2 * dv), F32)
    col = 0
    for k_ref, v_ref in ((kc_ref, vc_ref), (kl_ref, vl_ref)):
        n_keys = k_ref.shape[0]
        w = min(DIFF_KEY_CHUNK, n_keys)
        ones_col = (lax.broadcasted_iota(I32, (w, dv), 1) == 0).astype(BF16)
        for r0 in range(0, n_keys, w):
            s_new = lax.dot_general(qq, k_ref[r0:r0 + w, :], _NT, preferred_element_type=F32)
            s_w[:, col:col + w] = s_new
            for t in range(w // LANES):
                m_new = jnp.maximum(m_new, s_new[:, t * LANES:(t + 1) * LANES])
            e = jnp.exp(s_r[:, col:col + w] - mx).astype(BF16)
            v_ext = jnp.concatenate([v_ref[r0:r0 + w, :], ones_col], axis=1)
            acc = acc + jnp.dot(e, v_ext, preferred_element_type=F32)
            col += w
    m_w[...] = m_new
    dl = lam_ref[...]
    lam = (jnp.exp(jnp.sum(dl[0:1] * dl[1:2], axis=-1, keepdims=True))
           - jnp.exp(jnp.sum(dl[2:3] * dl[3:4], axis=-1, keepdims=True)) + LAM_INIT)
    inv = 1.0 / acc[:, dv:dv + 1]
    o = acc[:tq, :dv] * inv[:tq] - acc[tq:, :dv] * (inv[tq:] * lam)
    ms = jnp.mean(o * o, axis=-1, keepdims=True)
    o_ref[...] = (o * lax.rsqrt(ms + RMS_EPS) * ng_ref[...] * (1.0 - LAM_INIT)).astype(o_ref.dtype)


def _diff_kernel(q_ref, kl_ref, kc_ref, vl_ref, vc_ref, lam_ref, ng_ref, o_ref, sa_ref, sb_ref, ma_ref, mb_ref):
    j = pl.program_id(1)

    @pl.when(j == 0)
    def _():
        sb_ref[...] = jnp.zeros_like(sb_ref)
        mb_ref[...] = jnp.zeros_like(mb_ref)

    args = (q_ref, kc_ref, kl_ref, vc_ref, vl_ref, lam_ref, ng_ref, o_ref)

    @pl.when(j % 2 == 0)
    def _():
        _diff_step(*args, sa_ref, ma_ref, sb_ref, mb_ref)

    @pl.when(j % 2 == 1)
    def _():
        _diff_step(*args, sb_ref, mb_ref, sa_ref, ma_ref)


def _diff(pb_l, pb_c, diff_lambda, norm_g, *, batch, n_lat, n_ctx):
    h = DIFF_HEADS
    dv = 2 * DIFF_DH
    tq = min(256, n_lat)
    nq = n_lat // tq
    nblk = h * nq
    cur = lambda j: jnp.minimum(j, nblk - 1)
    prev = lambda j: jnp.maximum(j - 1, 0)
    return pl.pallas_call(
        _diff_kernel,
        grid=(batch, nblk + 1),
        in_specs=[pl.BlockSpec((tq, dv), lambda b, j: (b * nq + cur(j) % nq, cur(j) // nq)),
                  pl.BlockSpec((n_lat, dv), lambda b, j: (b, h + cur(j) // nq)),
                  pl.BlockSpec((n_ctx, dv), lambda b, j: (b, h + cur(j) // nq)),
                  pl.BlockSpec((n_lat, dv), lambda b, j: (b, 2 * h + prev(j) // nq)),
                  pl.BlockSpec((n_ctx, dv), lambda b, j: (b, 2 * h + prev(j) // nq)),
                  pl.BlockSpec(diff_lambda.shape, lambda b, j: (0, 0)),
                  pl.BlockSpec((1, dv), lambda b, j: (0, 0))],
        out_specs=pl.BlockSpec((tq, dv), lambda b, j: (b * nq + prev(j) % nq, prev(j) // nq)),
        out_shape=jax.ShapeDtypeStruct((batch * n_lat, h * dv), BF16),
        scratch_shapes=[pltpu.VMEM((2 * tq, n_ctx + n_lat), F32), pltpu.VMEM((2 * tq, n_ctx + n_lat), F32),
                        pltpu.VMEM((2 * tq, LANES), F32), pltpu.VMEM((2 * tq, LANES), F32)],
        compiler_params=_cparams(("arbitrary", "arbitrary")),
    )(pb_l, pb_l, pb_c, pb_l, pb_c, diff_lambda, norm_g.reshape(1, dv))


def _layer_norm(y, g, b):
    mu = jnp.mean(y, axis=-1, keepdims=True)
    yc = y - mu
    var = jnp.mean(yc * yc, axis=-1, keepdims=True)
    return yc * lax.rsqrt(var + LN_EPS) * g + b


def _split2(a):
    hi = a.astype(BF16)
    return hi, (a - hi.astype(F32)).astype(BF16)


def _out_kernel(ga_ref, df_ref, x_ref, wo_ref, g1_ref, sh2_ref, sc2_ref, lg_ref, lb_ref, wr_ref,
                x1_ref, h_ref, aff_ref):
    half = ga_ref.shape[1]
    o = (jnp.dot(ga_ref[...], wo_ref[:half, :], preferred_element_type=F32)
         + jnp.dot(df_ref[...], wo_ref[half:, :], preferred_element_type=F32))
    x1 = _layer_norm(ALPHA * x_ref[...] + g1_ref[...] * o, lg_ref[...], lb_ref[...])
    x1_ref[...] = ALPHA * x1
    hmod = x1 * (1.0 + sc2_ref[...]) + sh2_ref[...]
    hbits = pltpu.bitcast(hmod.astype(BF16).astype(F32), I32)
    hw = hbits.shape[1] // 2
    packed = (hbits[:, hw:] & jnp.int32(-65536)) | lax.shift_right_logical(hbits[:, :hw], 16)
    tm = packed.shape[0]
    wt = hw // LANES
    for s in range(wt):
        h_ref[pl.ds(s, tm, stride=wt), :] = packed[:, s * LANES:(s + 1) * LANES]
    h_hi, h_lo = _split2(hmod)
    w_hi, w_lo = _split2(wr_ref[...])
    logits = (lax.dot_general(w_hi, h_hi, _NT, preferred_element_type=F32)
              + lax.dot_general(w_lo, h_hi, _NT, preferred_element_type=F32)
              + lax.dot_general(w_hi, h_lo, _NT, preferred_element_type=F32))
    e = jnp.exp(logits - jnp.max(logits, axis=0, keepdims=True))
    aff_ref[...] = e / jnp.sum(e, axis=0, keepdims=True)


def _out(gla_o, diff_o, x2d, w_o, mod3, ln_g, ln_b, w_router_t, *, batch, n_lat):
    m, d = x2d.shape
    half = gla_o.shape[1]
    n_e = w_router_t.shape[0]
    tm = min(512, n_lat)
    tpb = n_lat // tm
    modspec = lambda k: pl.BlockSpec((None, 1, d), lambda i: (i // tpb, 0, k))
    vec = pl.BlockSpec((1, d), lambda i: (0, 0))
    return pl.pallas_call(
        _out_kernel,
        grid=(m // tm,),
        in_specs=[pl.BlockSpec((tm, half), lambda i: (i, 0)),
                  pl.BlockSpec((tm, half), lambda i: (i, 0)),
                  pl.BlockSpec((tm, d), lambda i: (i, 0)),
                  pl.BlockSpec((2 * half, d), lambda i: (0, 0), pipeline_mode=pl.Buffered(1)),
                  modspec(2), modspec(3), modspec(4), vec, vec,
                  pl.BlockSpec((n_e, d), lambda i: (0, 0))],
        out_specs=[pl.BlockSpec((tm, d), lambda i: (i, 0)),
                   pl.BlockSpec((tm * (d // 2 // LANES), LANES), lambda i: (i, 0)),
                   pl.BlockSpec((None, n_e, tm), lambda i: (i // tpb, 0, i % tpb))],
        out_shape=[jax.ShapeDtypeStruct((m, d), F32),
                   jax.ShapeDtypeStruct((m * (d // 2 // LANES), LANES), I32),
                   jax.ShapeDtypeStruct((batch, n_e, n_lat), F32)],
        compiler_params=_cparams(("arbitrary",)),
    )(gla_o, diff_o, x2d, w_o, mod3, mod3, mod3, ln_g.reshape(1, d), ln_b.reshape(1, d), w_router_t)


def _prefix_count(mask_f32, strict_upper):
    rows, n = mask_f32.shape
    carry = jnp.zeros((rows, 1), F32)
    pieces = []
    for j in range(n // LANES):
        blk = mask_f32[:, j * LANES:(j + 1) * LANES]
        within = jnp.dot(blk.astype(BF16), strict_upper, preferred_element_type=F32)
        pieces.append(within + carry)
        carry = carry + jnp.sum(blk, axis=-1, keepdims=True)
    return jnp.concatenate(pieces, axis=-1)


def _route_kernel(aff_ref, idx_ref, gate_ref, pos_ref, *, cap):
    b = pl.program_id(0)
    v = aff_ref[...]
    n_e, n = v.shape
    bits = pltpu.bitcast(v, I32)

    def search(i, t):
        cand = t | (1 << (30 - i))
        cnt = jnp.sum((bits >= cand).astype(I32), axis=-1, keepdims=True)
        return jnp.where(cnt >= cap, cand, t)

    thr = lax.fori_loop(0, 31, search, jnp.zeros((n_e, 1), I32))
    gt = bits > thr
    eq = bits == thr
    need = (cap - jnp.sum(gt.astype(I32), axis=-1, keepdims=True)).astype(F32)
    r = lax.broadcasted_iota(I32, (LANES, LANES), 0)
    cidx = lax.broadcasted_iota(I32, (LANES, LANES), 1)
    strict_upper = (r < cidx).astype(BF16)
    eq_rank = _prefix_count(eq.astype(F32), strict_upper)
    sel = jnp.logical_or(gt, jnp.logical_and(eq, eq_rank < need))
    pos = _prefix_count(sel.astype(F32), strict_upper)
    pos_ref[...] = jnp.where(sel, pos, -1.0).astype(I32)

    tok = lax.broadcasted_iota(I32, (8, n), 1)
    srow = lax.broadcasted_iota(I32, (8, n), 0)
    tok_hi = (tok >> 6).astype(F32)
    tok_lo = (tok & 63).astype(F32)
    slot = lax.broadcasted_iota(I32, (cap, n), 0)

    def compact(e, carry):
        ve = aff_ref[pl.ds(e, 1), :]
        v_hi = ve.astype(BF16).astype(F32)
        r1 = ve - v_hi
        v_mid = r1.astype(BF16).astype(F32)
        v_lo = r1 - v_mid
        lhs = jnp.where(srow == 0, tok_hi,
              jnp.where(srow == 1, tok_lo,
              jnp.where(srow == 2, v_hi,
              jnp.where(srow == 3, v_mid,
              jnp.where(srow == 4, v_lo, 0.0))))).astype(BF16)
        onehot = (slot == pos_ref[pl.ds(e, 1), :]).astype(F32).astype(BF16)
        res = lax.dot_general(lhs, onehot, _NT, preferred_element_type=F32)
        idx_ref[pl.ds(e, 1), :] = (res[0:1] * 64.0 + res[1:2]).astype(I32) + b * n
        gate_ref[pl.ds(e, 1), :] = (res[2:3] + res[3:4]) + res[4:5]
        return carry

    lax.fori_loop(0, n_e, compact, 0)


def _route(aff_t, *, cap):
    batch, n_e, n = aff_t.shape
    kern = functools.partial(_route_kernel, cap=cap)
    return pl.pallas_call(
        kern,
        grid=(batch,),
        in_specs=[pl.BlockSpec((None, n_e, n), lambda b: (b, 0, 0))],
        out_specs=[pl.BlockSpec((None, n_e, cap), lambda b: (b, 0, 0)),
                   pl.BlockSpec((None, n_e, cap), lambda b: (b, 0, 0))],
        out_shape=[jax.ShapeDtypeStruct((batch, n_e, cap), I32),
                   jax.ShapeDtypeStruct((batch, n_e, cap), F32)],
        scratch_shapes=[pltpu.VMEM((n_e, n), I32)],
        compiler_params=_cparams(("arbitrary",)),
    )(aff_t)


DMA_UNROLL = 8


def _moe_kernel(idx_ref, gate_ref, h_hbm, w1_ref, w3_ref, w2_ref, g2_ref, facc_in, facc_hbm,
                stage_ref, acc_ref, gsem, rsem, wsem, *, rows, cap):
    del facc_in
    e = pl.program_id(0)
    f = pl.program_id(1)
    n_e = pl.num_programs(0)
    nf = pl.num_programs(1)
    slot = e % 2

    def issue_rows(make):
        def body(i, carry):
            for u in range(DMA_UNROLL):
                make(i * DMA_UNROLL + u).start()
            return carry
        lax.fori_loop(0, rows // DMA_UNROLL, body, 0)

    wt = stage_ref.shape[1] // rows

    def x_row(ex, s):
        def make(k):
            src = pl.multiple_of(idx_ref[ex * rows + k] * wt, wt)
            dst = pl.multiple_of(k * wt, wt)
            return pltpu.make_async_copy(h_hbm.at[pl.ds(src, wt), :], stage_ref.at[s, pl.ds(dst, wt), :], gsem.at[s])
        return make

    def acc_row_in(k):
        return pltpu.make_async_copy(facc_hbm.at[pl.ds(idx_ref[e * rows + k], 1), :], acc_ref.at[pl.ds(k, 1), :], rsem)

    def acc_row_out(k):
        return pltpu.make_async_copy(acc_ref.at[pl.ds(k, 1), :], facc_hbm.at[pl.ds(idx_ref[e * rows + k], 1), :], wsem)

    def wait_x(s):
        pltpu.make_async_copy(h_hbm.at[pl.ds(0, rows * wt), :], stage_ref.at[s], gsem.at[s]).wait()

    def wait_acc_in():
        pltpu.make_async_copy(facc_hbm.at[pl.ds(0, rows), :], acc_ref, rsem).wait()

    def wait_acc_out():
        pltpu.make_async_copy(acc_ref, facc_hbm.at[pl.ds(0, rows), :], wsem).wait()

    @pl.when(f == 0)
    def _():
        @pl.when(e > 0)
        def _():
            wait_acc_out()
        issue_rows(acc_row_in)

        @pl.when(e == 0)
        def _():
            issue_rows(x_row(0, 0))
        wait_x(slot)

        @pl.when(e + 1 < n_e)
        def _():
            issue_rows(x_row(e + 1, 1 - slot))
        wait_acc_in()

    words = [stage_ref[slot, pl.ds(s, rows, stride=wt), :] for s in range(wt)]
    half = wt * LANES
    x_lo = jnp.concatenate([pltpu.bitcast(u << 16, F32).astype(BF16) for u in words], axis=1)
    x_hi = jnp.concatenate([pltpu.bitcast(u & jnp.int32(-65536), F32).astype(BF16) for u in words],
                           axis=1)
    tf = w1_ref.shape[1]
    w13 = jnp.concatenate([w1_ref[...], w3_ref[...]], axis=1).astype(BF16)
    ag = (jnp.dot(x_lo, w13[:half], preferred_element_type=F32)
          + jnp.dot(x_hi, w13[half:], preferred_element_type=F32))
    hid = (_silu(ag[:, :tf]) * ag[:, tf:]).astype(BF16)
    gate = gate_ref[...]
    d = acc_ref.shape[1]
    tc = min(512, d)
    for j in range(d // tc):
        cols = slice(j * tc, (j + 1) * tc)
        y = gate * jnp.dot(hid, w2_ref[:, cols].astype(BF16), preferred_element_type=F32)
        for b in range(rows // cap):
            rb = slice(b * cap, (b + 1) * cap)
            acc_ref[rb, cols] += y[rb] * g2_ref[b][:, cols]

    @pl.when(f == nf - 1)
    def _():
        issue_rows(acc_row_out)

        @pl.when(e == n_e - 1)
        def _():
            wait_acc_out()


def _moe(idx_flat, gate_col, h_packed, w1, w3, w2, mod3, facc0, *, rows, cap):
    m, d = facc0.shape
    n_e, _, dff = w1.shape
    tf = min(256, dff)
    kern = functools.partial(_moe_kernel, rows=rows, cap=cap)
    grid_spec = pltpu.PrefetchScalarGridSpec(
        num_scalar_prefetch=1,
        grid=(n_e, dff // tf),
        in_specs=[pl.BlockSpec((None, rows, 1), lambda e, f, idx: (e, 0, 0)),
                  pl.BlockSpec(memory_space=pl.ANY),
                  pl.BlockSpec((None, d, tf), lambda e, f, idx: (e, 0, f)),
                  pl.BlockSpec((None, d, tf), lambda e, f, idx: (e, 0, f)),
                  pl.BlockSpec((None, tf, d), lambda e, f, idx: (e, f, 0)),
                  pl.BlockSpec((ADA_ROWS, 1, d), lambda e, f, idx: (0, 0, 5)),
                  pl.BlockSpec(memory_space=pl.ANY)],
        out_specs=pl.BlockSpec(memory_space=pl.ANY),
        scratch_shapes=[pltpu.VMEM((2, rows * (d // 2 // LANES), LANES), I32), pltpu.VMEM((rows, d), F32),
                        pltpu.SemaphoreType.DMA((2,)), pltpu.SemaphoreType.DMA(()), pltpu.SemaphoreType.DMA(())],
    )
    return pl.pallas_call(
        kern,
        grid_spec=grid_spec,
        out_shape=jax.ShapeDtypeStruct((m, d), F32),
        input_output_aliases={7: 0},
        compiler_params=_cparams(("arbitrary", "arbitrary")),
    )(idx_flat, gate_col, h_packed, w1, w3, w2, mod3, facc0)


def _final_kernel(y_ref, lg_ref, lb_ref, o_ref):
    o_ref[...] = _layer_norm(y_ref[...], lg_ref[...], lb_ref[...])


def _final(pre_ln, ln_g, ln_b, *, n_lat):
    m, d = pre_ln.shape
    tm = min(512, n_lat)
    vec = pl.BlockSpec((1, d), lambda i: (0, 0))
    return pl.pallas_call(
        _final_kernel,
        grid=(m // tm,),
        in_specs=[pl.BlockSpec((tm, d), lambda i: (i, 0)), vec, vec],
        out_specs=pl.BlockSpec((tm, d), lambda i: (i, 0)),
        out_shape=jax.ShapeDtypeStruct((m, d), F32),
        compiler_params=_cparams(("arbitrary",)),
    )(pre_ln, ln_g.reshape(1, d), ln_b.reshape(1, d))


def _rope_tables(n_lat):
    rows = n_lat // GRID_W
    row = jnp.repeat(jnp.arange(rows, dtype=F32), GRID_W)
    col = jnp.tile(jnp.arange(GRID_W, dtype=F32), rows)
    n_freq = DIFF_DH // 4
    inv = ROPE_BASE ** (-jnp.arange(n_freq, dtype=F32) / n_freq)
    ang = jnp.concatenate([row[:, None] * inv, col[:, None] * inv], axis=-1)
    cos, sin = jnp.cos(ang), jnp.sin(ang)
    reps = LANES // DIFF_DH
    cos_t = jnp.tile(jnp.concatenate([cos, cos], axis=-1), (1, reps))
    sin_t = jnp.tile(jnp.concatenate([-sin, sin], axis=-1), (1, reps))
    return cos_t, sin_t


def kernel(x, c, ctx, c_ctx, w_ada, b_ada, w_in, w_gate2, b_gate, gla_norm_g, diff_lambda, diff_norm_g, w_o,
           ln1_g, ln1_b, w_router, w1, w3, w2, ln2_g, ln2_b):
    batch, n_lat, d = x.shape
    n_ctx = ctx.shape[1]
    assert w_ada.shape[0] == DEPTH == 1 and batch < ADA_ROWS
    n_e = w_router.shape[-1]
    cap = EC_CAPACITY_FACTOR * n_lat // n_e
    l = 0

    c_rows = jnp.concatenate([c, c_ctx[None, :], jnp.zeros((ADA_ROWS - batch - 1, d), F32)], axis=0)
    mod3 = _ada(c_rows, w_ada[l], b_ada[l]).reshape(ADA_ROWS, 1, 6 * d)

    gqkvr = 2 * (d // 4) + 2 * (d // 2)
    r2 = 2 * GLA_RANK
    w_in_t = w_in[l].T.astype(BF16)
    w_diff = w_in_t[gqkvr + r2:]
    w_lr = w_in_t[gqkvr:gqkvr + r2]
    hk = w_gate2.shape[-1]
    zeros = jnp.zeros((GLA_RANK, hk), F32)
    w_g = jnp.concatenate([jnp.concatenate([w_gate2[l, 0], zeros], axis=1),
                           jnp.concatenate([zeros, w_gate2[l, 1]], axis=1)], axis=0).astype(BF16)
    b_g = b_gate[l].reshape(1, 2 * hk)
    cos_t, sin_t = _rope_tables(n_lat)

    tm_l = min(1024, n_lat)
    tpb = n_lat // tm_l
    x2d = x.reshape(batch * n_lat, d)
    pf_l, pb_l, lg_l = _proj(x2d, mod3, lambda i: i // tpb, w_in_t, w_diff, w_lr, w_g, b_g, cos_t, sin_t,
                             rope=True, tm=tm_l, n_tab_tiles=tpb)
    ctx2d = ctx.reshape(batch * n_ctx, d)
    tm_c = min(1024, batch * n_ctx)
    pf_c, pb_c, lg_c = _proj(ctx2d, mod3, lambda i: batch, w_in_t, w_diff, w_lr, w_g, b_g, cos_t, sin_t,
                             rope=False, tm=tm_c, n_tab_tiles=1)

    gla_o = _gla(pf_l, lg_l, pf_c, lg_c, gla_norm_g[l], batch=batch, n_lat=n_lat, n_ctx=n_ctx)
    diff_o = _diff(pb_l, pb_c, diff_lambda[l], diff_norm_g[l], batch=batch, n_lat=n_lat, n_ctx=n_ctx)

    res, h_packed, aff_t = _out(gla_o, diff_o, x2d, w_o[l].astype(BF16), mod3, ln1_g[l], ln1_b[l],
                                w_router[l].T, batch=batch, n_lat=n_lat)

    idx, gate = _route(aff_t, cap=cap)
    rows = batch * cap
    idx_flat = idx.transpose(1, 0, 2).reshape(n_e * rows)
    gate_col = gate.transpose(1, 0, 2).reshape(n_e, rows, 1)
    pre_ln = _moe(idx_flat, gate_col, h_packed, w1[l], w3[l], w2[l], mod3, res, rows=rows, cap=cap)

    out = _final(pre_ln, ln2_g[l], ln2_b[l], n_lat=n_lat)
    return out.reshape(batch, n_lat, d)
```

```python
import functools
import math

import jax
import jax.numpy as jnp
import numpy as np
from jax import lax
from jax.experimental import pallas as pl
from jax.experimental.pallas import tpu as pltpu

F32 = jnp.float32
BF16 = jnp.bfloat16
I32 = jnp.int32

GRID_W = 64
GLA_HEADS = 4
GLA_RANK = 16
GLA_TAU = 16.0
GLA_CHUNK = 64
DIFF_HEADS = 8
DIFF_DH = 64
ROPE_BASE = 10000.0
EC_CAPACITY_FACTOR = 2
LN_EPS = 1e-5
RMS_EPS = 1e-6
DEPTH = 1
ALPHA = (2.0 * DEPTH) ** 0.25
LAM_INIT = 0.8 - 0.6 * math.exp(-0.3 * 0)

LANES = 128
ADA_ROWS = 8
VMEM_LIMIT = 60 * 1024 * 1024

_NT = (((1,), (1,)), ((), ()))
_TN = (((0,), (0,)), ((), ()))


def _cparams(sem):
    return pltpu.CompilerParams(dimension_semantics=sem, vmem_limit_bytes=VMEM_LIMIT)


def _silu(a):
    return a * (1.0 / (1.0 + jnp.exp(-a)))


def _log_sigmoid(z):
    return jnp.minimum(z, 0.0) - jnp.log1p(jnp.exp(-jnp.abs(z)))


def _ada_kernel(c_ref, w_ref, b_ref, o_ref):
    s = _silu(c_ref[...]).astype(BF16)
    o_ref[...] = jnp.dot(s, w_ref[...].astype(BF16), preferred_element_type=F32) + b_ref[...]


def _ada(c_rows, w_ada, b_ada):
    d, n_out = w_ada.shape
    tn = min(1024, n_out)
    return pl.pallas_call(
        _ada_kernel,
        grid=(n_out // tn,),
        in_specs=[pl.BlockSpec((ADA_ROWS, d), lambda j: (0, 0)),
                  pl.BlockSpec((d, tn), lambda j: (0, j)),
                  pl.BlockSpec((1, tn), lambda j: (0, j))],
        out_specs=pl.BlockSpec((ADA_ROWS, tn), lambda j: (0, j)),
        out_shape=jax.ShapeDtypeStruct((ADA_ROWS, n_out), F32),
        compiler_params=_cparams(("arbitrary",)),
    )(c_rows, w_ada, b_ada.reshape(1, n_out))


def _proj_kernel(x_ref, sh_ref, sc_ref, wa_ref, wb_ref, wlr_ref, wg_ref, bg_ref, cos_ref, sin_ref,
                 of_ref, ob_ref, lg_ref, u_ref, *, rope, n_f32, n_q, n_rope):
    n = pl.program_id(1)

    @pl.when(n == 0)
    def _():
        u = (x_ref[...] * (1.0 + sc_ref[...]) + sh_ref[...]).astype(BF16)
        u_ref[...] = u
        lr = lax.dot_general(u, wlr_ref[...], _NT, preferred_element_type=F32)
        z = jnp.dot(lr.astype(BF16), wg_ref[...], preferred_element_type=F32) + bg_ref[...]
        lg_ref[...] = _log_sigmoid(z) * (1.0 / GLA_TAU)

    @pl.when(n < n_f32)
    def _():
        of_ref[...] = lax.dot_general(u_ref[...], wa_ref[...], _NT, preferred_element_type=F32)

    @pl.when(jnp.logical_and(n >= n_f32, n < n_f32 + n_rope))
    def _():
        acc = lax.dot_general(u_ref[...], wb_ref[...], _NT, preferred_element_type=F32)
        scale = jnp.where(n < n_f32 + n_q, DIFF_DH ** -0.5, 1.0).astype(F32)
        if rope:
            lane = lax.broadcasted_iota(I32, (acc.shape[0], LANES), 1)
            first = (lane % DIFF_DH) < (DIFF_DH // 2)
            cs = cos_ref[...]
            sn = sin_ref[...]
            for j in range(acc.shape[1] // LANES):
                a = acc[:, j * LANES:(j + 1) * LANES]
                partner = jnp.where(first, pltpu.roll(a, LANES - DIFF_DH // 2, 1), pltpu.roll(a, DIFF_DH // 2, 1))
                ob_ref[:, j * LANES:(j + 1) * LANES] = ((a * cs + partner * sn) * scale).astype(BF16)
        else:
            ob_ref[...] = (acc * scale).astype(BF16)

    @pl.when(n >= n_f32 + n_rope)
    def _():
        ob_ref[...] = lax.dot_general(u_ref[...], wb_ref[...], _NT, preferred_element_type=F32).astype(BF16)


def _proj(x2d, mod3, row_of_tile, w_in, w_diff, w_lr, w_g, b_g, cos_t, sin_t, *, rope, tm, n_tab_tiles):
    m, d = x2d.shape
    tn = 512
    n_f32 = w_diff.shape[0] // tn
    n_tiles = 2 * n_f32
    n_q = n_tiles // 6
    n_rope = n_tiles // 3
    r2 = w_lr.shape[0]
    ng = w_g.shape[1]
    kern = functools.partial(_proj_kernel, rope=rope, n_f32=n_f32, n_q=n_q, n_rope=n_rope)
    return pl.pallas_call(
        kern,
        grid=(m // tm, n_tiles),
        in_specs=[pl.BlockSpec((tm, d), lambda i, j: (i, 0)),
                  pl.BlockSpec((None, 1, d), lambda i, j: (row_of_tile(i), 0, 0)),
                  pl.BlockSpec((None, 1, d), lambda i, j: (row_of_tile(i), 0, 1)),
                  pl.BlockSpec((tn, d), lambda i, j: (jnp.minimum(j, n_f32 - 1), 0)),
                  pl.BlockSpec((tn, d), lambda i, j: (jnp.maximum(j - n_f32, 0), 0)),
                  pl.BlockSpec((r2, d), lambda i, j: (0, 0)),
                  pl.BlockSpec((r2, ng), lambda i, j: (0, 0)),
                  pl.BlockSpec((1, ng), lambda i, j: (0, 0)),
                  pl.BlockSpec((tm, LANES), lambda i, j: (i % n_tab_tiles, 0)),
                  pl.BlockSpec((tm, LANES), lambda i, j: (i % n_tab_tiles, 0))],
        out_specs=[pl.BlockSpec((tm, tn), lambda i, j: (i, jnp.minimum(j, n_f32 - 1))),
                   pl.BlockSpec((tm, tn), lambda i, j: (i, jnp.maximum(j - n_f32, 0))),
                   pl.BlockSpec((tm, ng), lambda i, j: (i, 0))],
        out_shape=[jax.ShapeDtypeStruct((m, n_f32 * tn), F32),
                   jax.ShapeDtypeStruct((m, (n_tiles - n_f32) * tn), BF16),
                   jax.ShapeDtypeStruct((m, ng), F32)],
        scratch_shapes=[pltpu.VMEM((tm, d), BF16)],
        compiler_params=_cparams(("arbitrary", "arbitrary")),
    )(x2d, mod3, mod3, w_in, w_diff, w_lr, w_g, b_g, cos_t, sin_t)


GLA_GROUP = 4
GLA_UNROLL = 2


def _split3(a):
    hi = a.astype(BF16)
    r1 = a - hi.astype(F32)
    mid = r1.astype(BF16)
    return hi, mid, (r1 - mid.astype(F32)).astype(BF16)


def _gla_groups(dirs, *, want_out, scale):
    c = GLA_CHUNK
    rows, dk = dirs[0]["k"].shape
    n_chunks = rows // c
    for d in dirs:
        d["parts"] = jnp.dot(d["sum_mat"], jnp.concatenate(_split3(d["g"]), axis=1),
                             preferred_element_type=F32)
    for d in dirs:
        parts = d["parts"]
        sums = (parts[:, :dk] + parts[:, dk:2 * dk]) + parts[:, 2 * dk:]
        bcum = sums[:rows]
        btot = sums[rows:]
        d["vb"] = d["v"].astype(BF16)
        d["k_out"] = (d["k"] * jnp.exp(btot - bcum)).astype(BF16)
        d["decay"] = jnp.exp(btot)
        if want_out:
            d["q_in"] = ((d["q"] * scale) * jnp.exp(bcum)).astype(BF16)
            d["k_in"] = (d["k"] * jnp.exp(-bcum)).astype(BF16)
    if want_out:
        for d in dirs:
            d["att"] = lax.dot_general(d["q_in"], d["k_in"], _NT, preferred_element_type=F32)
        for d in dirs:
            d["att"] = jnp.where(d["causal"], d["att"], 0.0).astype(BF16)
        for d in dirs:
            d["intra"] = jnp.dot(d["att"], d["vb"], preferred_element_type=F32)
    chunk = lambda ci: slice(ci * c, (ci + 1) * c)
    for d in dirs:
        d["upd"] = [lax.dot_general(d["vb"][chunk(ci)], d["k_out"][chunk(ci)], _TN, preferred_element_type=F32)
                    for ci in range(n_chunks)]
    states = {}
    for d in dirs:
        st = states.get(id(d["st_ref"]))
        if st is None:
            st = d["st_ref"][...]
        d["st_in"] = [None] * n_chunks
        for ci in (range(n_chunks) if d["forward"] else reversed(range(n_chunks))):
            d["st_in"][ci] = st
            st = st * d["decay"][ci * c:ci * c + 1] + d["upd"][ci]
        states[id(d["st_ref"])] = st
    for d in dirs:
        d["st_ref"][...] = states[id(d["st_ref"])]
    if not want_out:
        return [None for _ in dirs]
    outs = []
    for d in dirs:
        outs.append(jnp.concatenate(
            [d["intra"][chunk(ci)] + lax.dot_general(d["q_in"][chunk(ci)], d["st_in"][ci].astype(BF16), _NT,
                                                     preferred_element_type=F32)
             for ci in range(n_chunks)], axis=0))
    return outs


def _gla_kernel(q_ref, k_ref, v_ref, r_ref, gf_ref, gb_ref, kc_ref, vc_ref, gfc_ref, gbc_ref, ng_ref,
                o_ref, of_ref, obk_ref, sf_ref, sb_ref, *, n_lat, n_ctx, dk):
    c = GLA_CHUNK
    scale = dk ** -0.5
    sf_ref[...] = jnp.zeros_like(sf_ref)
    sb_ref[...] = jnp.zeros_like(sb_ref)

    def masks(rows):
        row = lax.broadcasted_iota(I32, (rows, rows), 0)
        col = lax.broadcasted_iota(I32, (rows, rows), 1)
        same = (row // c) == (col // c)
        lower = jnp.logical_and(same, row >= col)
        upper = jnp.logical_and(same, row <= col)
        total = same.astype(BF16)
        return (lower, jnp.concatenate([lower.astype(BF16), total], axis=0),
                upper, jnp.concatenate([upper.astype(BF16), total], axis=0))

    def scan(n, refs_f, refs_b, want_out):
        rows = min(GLA_GROUP * c, n)
        lower, sum_f, upper, sum_b = masks(rows)
        n_groups = n // rows
        unroll = GLA_UNROLL if n_groups % GLA_UNROLL == 0 else 1
        ld = lambda ref, r0: None if ref is None else ref[pl.ds(r0, rows), :]
        names = ("q", "k", "v", "g")

        def body(i, carry):
            items, stores = [], []
            for u in range(unroll):
                rf = pl.multiple_of((i * unroll + u) * rows, rows)
                rb = pl.multiple_of((n_groups - 1 - (i * unroll + u)) * rows, rows)
                items.append(dict(zip(names, (ld(r, rf) for r in refs_f)), st_ref=sf_ref, sum_mat=sum_f,
                                  causal=lower, forward=True))
                items.append(dict(zip(names, (ld(r, rb) for r in refs_b)), st_ref=sb_ref, sum_mat=sum_b,
                                  causal=upper, forward=False))
                stores += [(of_ref, rf), (obk_ref, rb)]
            outs = _gla_groups(items, want_out=want_out, scale=scale)
            if want_out:
                for (ref, r0), out in zip(stores, outs):
                    ref[pl.ds(r0, rows), :] = out
            return carry

        lax.fori_loop(0, n_groups // unroll, body, 0)

    scan(n_ctx, (None, kc_ref, vc_ref, gfc_ref), (None, kc_ref, vc_ref, gbc_ref), False)
    scan(n_lat, (q_ref, k_ref, v_ref, gf_ref), (q_ref, k_ref, v_ref, gb_ref), True)

    tb = min(512, n_lat)

    def fin_body(i, carry):
        r0 = pl.multiple_of(i * tb, tb)
        o = of_ref[pl.ds(r0, tb), :] + obk_ref[pl.ds(r0, tb), :]
        ms = jnp.mean(o * o, axis=-1, keepdims=True)
        o = o * lax.rsqrt(ms + RMS_EPS) * ng_ref[...]
        o_ref[pl.ds(r0, tb), :] = (o * _silu(r_ref[pl.ds(r0, tb), :])).astype(o_ref.dtype)
        return carry

    lax.fori_loop(0, n_lat // tb, fin_body, 0)


def _gla(pf_l, lg_l, pf_c, lg_c, norm_g, *, batch, n_lat, n_ctx):
    h = GLA_HEADS
    dk = pf_l.shape[1] // 6 // h
    dv = 2 * dk
    kern = functools.partial(_gla_kernel, n_lat=n_lat, n_ctx=n_ctx, dk=dk)
    lat = lambda width, off: pl.BlockSpec((n_lat, width), lambda b, i: (b, off + i))
    ctx = lambda width, off: pl.BlockSpec((n_ctx, width), lambda b, i: (b, off + i))
    return pl.pallas_call(
        kern,
        grid=(batch, h),
        in_specs=[lat(dk, 0), lat(dk, h), lat(dv, h), lat(dv, 2 * h),
                  lat(dk, 0), lat(dk, h),
                  ctx(dk, h), ctx(dv, h), ctx(dk, 0), ctx(dk, h),
                  pl.BlockSpec((1, dv), lambda b, i: (0, 0))],
        out_specs=pl.BlockSpec((n_lat, dv), lambda b, i: (b, i)),
        out_shape=jax.ShapeDtypeStruct((batch * n_lat, h * dv), BF16),
        scratch_shapes=[pltpu.VMEM((n_lat, dv), F32), pltpu.VMEM((n_lat, dv), F32),
                        pltpu.VMEM((dv, dk), F32), pltpu.VMEM((dv, dk), F32)],
        compiler_params=_cparams(("arbitrary", "arbitrary")),
    )(pf_l, pf_l, pf_l, pf_l, lg_l, lg_l, pf_c, pf_c, lg_c, lg_c, norm_g.reshape(1, dv))


DIFF_KEY_CHUNK = 512


def _diff_step(q_ref, kc_ref, kl_ref, vc_ref, vl_ref, lam_ref, ng_ref, o_ref, acc_ref, s_w, m_w, s_r, m_r):
    tq = o_ref.shape[0]
    dv = o_ref.shape[1]
    done = acc_ref[...]
    dl = lam_ref[...]
    lam = (jnp.exp(jnp.sum(dl[0:1] * dl[1:2], axis=-1, keepdims=True))
           - jnp.exp(jnp.sum(dl[2:3] * dl[3:4], axis=-1, keepdims=True)) + LAM_INIT)
    inv = 1.0 / done[:, dv:dv + 1]
    o = done[:tq, :dv] * inv[:tq] - done[tq:, :dv] * (inv[tq:] * lam)
    ms = jnp.mean(o * o, axis=-1, keepdims=True)
    o_ref[...] = (o * lax.rsqrt(ms + RMS_EPS) * ng_ref[...] * (1.0 - LAM_INIT)).astype(o_ref.dtype)

    q = q_ref[...]
    lane = lax.broadcasted_iota(I32, q.shape, 1)
    zero = jnp.zeros_like(q)
    qq = jnp.concatenate([jnp.where(lane < DIFF_DH, q, zero), jnp.where(lane >= DIFF_DH, q, zero)], axis=0)
    mx = jnp.max(m_r[...], axis=-1, keepdims=True)
    m_new = jnp.full((2 * tq, LANES), -jnp.inf, F32)
    acc = jnp.zeros((2 * tq, 2 * dv), F32)
    col = 0
    for k_ref, v_ref in ((kc_ref, vc_ref), (kl_ref, vl_ref)):
        n_keys = k_ref.shape[0]
        w = min(DIFF_KEY_CHUNK, n_keys)
        ones_col = (lax.broadcasted_iota(I32, (w, dv), 1) == 0).astype(BF16)
        for r0 in range(0, n_keys, w):
            s_new = lax.dot_general(qq, k_ref[r0:r0 + w, :], _NT, preferred_element_type=F32)
            s_w[:, col:col + w] = s_new
            for t in range(w // LANES):
                m_new = jnp.maximum(m_new, s_new[:, t * LANES:(t + 1) * LANES])
            e = jnp.exp(s_r[:, col:col + w] - mx).astype(BF16)
            v_ext = jnp.concatenate([v_ref[r0:r0 + w, :], ones_col], axis=1)
            acc = acc + jnp.dot(e, v_ext, preferred_element_type=F32)
            col += w
    m_w[...] = m_new
    acc_ref[...] = acc


def _diff_kernel(q_ref, kl_ref, kc_ref, vl_ref, vc_ref, lam_ref, ng_ref, o_ref, acc_ref,
                 sa_ref, sb_ref, ma_ref, mb_ref):
    j = pl.program_id(1)

    @pl.when(j == 0)
    def _():
        sb_ref[...] = jnp.zeros_like(sb_ref)
        mb_ref[...] = jnp.zeros_like(mb_ref)
        acc_ref[...] = jnp.ones_like(acc_ref)

    args = (q_ref, kc_ref, kl_ref, vc_ref, vl_ref, lam_ref, ng_ref, o_ref, acc_ref)

    @pl.when(j % 2 == 0)
    def _():
        _diff_step(*args, sa_ref, ma_ref, sb_ref, mb_ref)

    @pl.when(j % 2 == 1)
    def _():
        _diff_step(*args, sb_ref, mb_ref, sa_ref, ma_ref)


def _diff(pb_l, pb_c, diff_lambda, norm_g, *, batch, n_lat, n_ctx):
    h = DIFF_HEADS
    dv = 2 * DIFF_DH
    tq = min(256, n_lat)
    nq = n_lat // tq
    nblk = h * nq
    cur = lambda j: jnp.minimum(j, nblk - 1)
    prev = lambda j: jnp.clip(j - 1, 0, nblk - 1)
    fin = lambda j: jnp.maximum(j - 2, 0)
    return pl.pallas_call(
        _diff_kernel,
        grid=(batch, nblk + 2),
        in_specs=[pl.BlockSpec((tq, dv), lambda b, j: (b * nq + cur(j) % nq, cur(j) // nq)),
                  pl.BlockSpec((n_lat, dv), lambda b, j: (b, h + cur(j) // nq)),
                  pl.BlockSpec((n_ctx, dv), lambda b, j: (b, h + cur(j) // nq)),
                  pl.BlockSpec((n_lat, dv), lambda b, j: (b, 2 * h + prev(j) // nq)),
                  pl.BlockSpec((n_ctx, dv), lambda b, j: (b, 2 * h + prev(j) // nq)),
                  pl.BlockSpec(diff_lambda.shape, lambda b, j: (0, 0)),
                  pl.BlockSpec((1, dv), lambda b, j: (0, 0))],
        out_specs=pl.BlockSpec((tq, dv), lambda b, j: (b * nq + fin(j) % nq, fin(j) // nq)),
        out_shape=jax.ShapeDtypeStruct((batch * n_lat, h * dv), BF16),
        scratch_shapes=[pltpu.VMEM((2 * tq, 2 * dv), F32),
                        pltpu.VMEM((2 * tq, n_ctx + n_lat), F32), pltpu.VMEM((2 * tq, n_ctx + n_lat), F32),
                        pltpu.VMEM((2 * tq, LANES), F32), pltpu.VMEM((2 * tq, LANES), F32)],
        compiler_params=_cparams(("arbitrary", "arbitrary")),
    )(pb_l, pb_l, pb_c, pb_l, pb_c, diff_lambda, norm_g.reshape(1, dv))


def _layer_norm(y, g, b):
    mu = jnp.mean(y, axis=-1, keepdims=True)
    yc = y - mu
    var = jnp.mean(yc * yc, axis=-1, keepdims=True)
    return yc * lax.rsqrt(var + LN_EPS) * g + b


def _split2(a):
    hi = a.astype(BF16)
    return hi, (a - hi.astype(F32)).astype(BF16)


def _out_kernel(ga_ref, df_ref, x_ref, wo_ref, g1_ref, sh2_ref, sc2_ref, lg_ref, lb_ref, wr_ref,
                x1_ref, h_ref, aff_ref):
    half = ga_ref.shape[1]
    o = (jnp.dot(ga_ref[...], wo_ref[:half, :], preferred_element_type=F32)
         + jnp.dot(df_ref[...], wo_ref[half:, :], preferred_element_type=F32))
    x1 = _layer_norm(ALPHA * x_ref[...] + g1_ref[...] * o, lg_ref[...], lb_ref[...])
    x1_ref[...] = ALPHA * x1
    hmod = x1 * (1.0 + sc2_ref[...]) + sh2_ref[...]
    hbits = pltpu.bitcast(hmod.astype(BF16).astype(F32), I32)
    hw = hbits.shape[1] // 2
    packed = (hbits[:, hw:] & jnp.int32(-65536)) | lax.shift_right_logical(hbits[:, :hw], 16)
    tm = packed.shape[0]
    wt = hw // LANES
    for s in range(wt):
        h_ref[pl.ds(s, tm, stride=wt), :] = packed[:, s * LANES:(s + 1) * LANES]
    h_hi, h_lo = _split2(hmod)
    w_hi, w_lo = _split2(wr_ref[...])
    logits = (lax.dot_general(w_hi, h_hi, _NT, preferred_element_type=F32)
              + lax.dot_general(w_lo, h_hi, _NT, preferred_element_type=F32)
              + lax.dot_general(w_hi, h_lo, _NT, preferred_element_type=F32))
    e = jnp.exp(logits - jnp.max(logits, axis=0, keepdims=True))
    aff_ref[...] = e / jnp.sum(e, axis=0, keepdims=True)


def _out(gla_o, diff_o, x2d, w_o, mod3, ln_g, ln_b, w_router_t, *, batch, n_lat):
    m, d = x2d.shape
    half = gla_o.shape[1]
    n_e = w_router_t.shape[0]
    tm = min(512, n_lat)
    tpb = n_lat // tm
    modspec = lambda k: pl.BlockSpec((None, 1, d), lambda i: (i // tpb, 0, k))
    vec = pl.BlockSpec((1, d), lambda i: (0, 0))
    return pl.pallas_call(
        _out_kernel,
        grid=(m // tm,),
        in_specs=[pl.BlockSpec((tm, half), lambda i: (i, 0)),
                  pl.BlockSpec((tm, half), lambda i: (i, 0)),
                  pl.BlockSpec((tm, d), lambda i: (i, 0)),
                  pl.BlockSpec((2 * half, d), lambda i: (0, 0), pipeline_mode=pl.Buffered(1)),
                  modspec(2), modspec(3), modspec(4), vec, vec,
                  pl.BlockSpec((n_e, d), lambda i: (0, 0))],
        out_specs=[pl.BlockSpec((tm, d), lambda i: (i, 0)),
                   pl.BlockSpec((tm * (d // 2 // LANES), LANES), lambda i: (i, 0)),
                   pl.BlockSpec((None, n_e, tm), lambda i: (i // tpb, 0, i % tpb))],
        out_shape=[jax.ShapeDtypeStruct((m, d), F32),
                   jax.ShapeDtypeStruct((m * (d // 2 // LANES), LANES), I32),
                   jax.ShapeDtypeStruct((batch, n_e, n_lat), F32)],
        compiler_params=_cparams(("arbitrary",)),
    )(gla_o, diff_o, x2d, w_o, mod3, mod3, mod3, ln_g.reshape(1, d), ln_b.reshape(1, d), w_router_t)


def _prefix_count(mask_f32, strict_upper):
    rows, n = mask_f32.shape
    carry = jnp.zeros((rows, 1), F32)
    pieces = []
    for j in range(n // LANES):
        blk = mask_f32[:, j * LANES:(j + 1) * LANES]
        within = jnp.dot(blk.astype(BF16), strict_upper, preferred_element_type=F32)
        pieces.append(within + carry)
        carry = carry + jnp.sum(blk, axis=-1, keepdims=True)
    return jnp.concatenate(pieces, axis=-1)


def _route_kernel(aff_ref, idx_ref, gate_ref, pos_ref, *, cap):
    b = pl.program_id(0)
    v = aff_ref[...]
    n_e, n = v.shape
    bits = pltpu.bitcast(v, I32)

    def search(i, t):
        cand = t | (1 << (30 - i))
        cnt = jnp.sum((bits >= cand).astype(I32), axis=-1, keepdims=True)
        return jnp.where(cnt >= cap, cand, t)

    thr = lax.fori_loop(0, 31, search, jnp.zeros((n_e, 1), I32))
    gt = bits > thr
    eq = bits == thr
    need = (cap - jnp.sum(gt.astype(I32), axis=-1, keepdims=True)).astype(F32)
    r = lax.broadcasted_iota(I32, (LANES, LANES), 0)
    cidx = lax.broadcasted_iota(I32, (LANES, LANES), 1)
    strict_upper = (r < cidx).astype(BF16)
    eq_rank = _prefix_count(eq.astype(F32), strict_upper)
    sel = jnp.logical_or(gt, jnp.logical_and(eq, eq_rank < need))
    pos = _prefix_count(sel.astype(F32), strict_upper)
    pos_ref[...] = jnp.where(sel, pos, -1.0).astype(I32)

    tok = lax.broadcasted_iota(I32, (8, n), 1)
    srow = lax.broadcasted_iota(I32, (8, n), 0)
    tok_hi = (tok >> 6).astype(F32)
    tok_lo = (tok & 63).astype(F32)
    slot = lax.broadcasted_iota(I32, (cap, n), 0)

    def compact(e, carry):
        ve = aff_ref[pl.ds(e, 1), :]
        v_hi = ve.astype(BF16).astype(F32)
        r1 = ve - v_hi
        v_mid = r1.astype(BF16).astype(F32)
        v_lo = r1 - v_mid
        lhs = jnp.where(srow == 0, tok_hi,
              jnp.where(srow == 1, tok_lo,
              jnp.where(srow == 2, v_hi,
              jnp.where(srow == 3, v_mid,
              jnp.where(srow == 4, v_lo, 0.0))))).astype(BF16)
        onehot = (slot == pos_ref[pl.ds(e, 1), :]).astype(F32).astype(BF16)
        res = lax.dot_general(lhs, onehot, _NT, preferred_element_type=F32)
        idx_ref[pl.ds(e, 1), :] = (res[0:1] * 64.0 + res[1:2]).astype(I32) + b * n
        gate_ref[pl.ds(e, 1), :] = (res[2:3] + res[3:4]) + res[4:5]
        return carry

    lax.fori_loop(0, n_e, compact, 0)


def _route(aff_t, *, cap):
    batch, n_e, n = aff_t.shape
    kern = functools.partial(_route_kernel, cap=cap)
    return pl.pallas_call(
        kern,
        grid=(batch,),
        in_specs=[pl.BlockSpec((None, n_e, n), lambda b: (b, 0, 0))],
        out_specs=[pl.BlockSpec((None, n_e, cap), lambda b: (b, 0, 0)),
                   pl.BlockSpec((None, n_e, cap), lambda b: (b, 0, 0))],
        out_shape=[jax.ShapeDtypeStruct((batch, n_e, cap), I32),
                   jax.ShapeDtypeStruct((batch, n_e, cap), F32)],
        scratch_shapes=[pltpu.VMEM((n_e, n), I32)],
        compiler_params=_cparams(("arbitrary",)),
    )(aff_t)


DMA_UNROLL = 8


def _moe_kernel(idx_ref, gate_ref, h_hbm, w1_ref, w3_ref, w2_ref, g2_ref, facc_in, facc_hbm,
                stage_ref, acc_ref, gsem, rsem, wsem, *, rows, cap):
    del facc_in
    e = pl.program_id(0)
    f = pl.program_id(1)
    n_e = pl.num_programs(0)
    nf = pl.num_programs(1)
    slot = e % 2

    def issue_rows(make):
        def body(i, carry):
            for u in range(DMA_UNROLL):
                make(i * DMA_UNROLL + u).start()
            return carry
        lax.fori_loop(0, rows // DMA_UNROLL, body, 0)

    wt = stage_ref.shape[1] // rows

    def x_row(ex, s):
        def make(k):
            src = pl.multiple_of(idx_ref[ex * rows + k] * wt, wt)
            dst = pl.multiple_of(k * wt, wt)
            return pltpu.make_async_copy(h_hbm.at[pl.ds(src, wt), :], stage_ref.at[s, pl.ds(dst, wt), :], gsem.at[s])
        return make

    def acc_row_in(k):
        return pltpu.make_async_copy(facc_hbm.at[pl.ds(idx_ref[e * rows + k], 1), :], acc_ref.at[pl.ds(k, 1), :], rsem)

    def acc_row_out(k):
        return pltpu.make_async_copy(acc_ref.at[pl.ds(k, 1), :], facc_hbm.at[pl.ds(idx_ref[e * rows + k], 1), :], wsem)

    def wait_x(s):
        pltpu.make_async_copy(h_hbm.at[pl.ds(0, rows * wt), :], stage_ref.at[s], gsem.at[s]).wait()

    def wait_acc_in():
        pltpu.make_async_copy(facc_hbm.at[pl.ds(0, rows), :], acc_ref, rsem).wait()

    def wait_acc_out():
        pltpu.make_async_copy(acc_ref, facc_hbm.at[pl.ds(0, rows), :], wsem).wait()

    @pl.when(f == 0)
    def _():
        @pl.when(e > 0)
        def _():
            wait_acc_out()
        issue_rows(acc_row_in)

        @pl.when(e == 0)
        def _():
            issue_rows(x_row(0, 0))
        wait_x(slot)

        @pl.when(e + 1 < n_e)
        def _():
            issue_rows(x_row(e + 1, 1 - slot))
        wait_acc_in()

    words = [stage_ref[slot, pl.ds(s, rows, stride=wt), :] for s in range(wt)]
    half = wt * LANES
    x_lo = jnp.concatenate([pltpu.bitcast(u << 16, F32).astype(BF16) for u in words], axis=1)
    x_hi = jnp.concatenate([pltpu.bitcast(u & jnp.int32(-65536), F32).astype(BF16) for u in words],
                           axis=1)
    tf = w1_ref.shape[1]
    w13 = jnp.concatenate([w1_ref[...], w3_ref[...]], axis=1).astype(BF16)
    ag = (jnp.dot(x_lo, w13[:half], preferred_element_type=F32)
          + jnp.dot(x_hi, w13[half:], preferred_element_type=F32))
    hid = (_silu(ag[:, :tf]) * ag[:, tf:]).astype(BF16)
    gate = gate_ref[...]
    d = acc_ref.shape[1]
    tc = min(512, d)
    for j in range(d // tc):
        cols = slice(j * tc, (j + 1) * tc)
        y = gate * jnp.dot(hid, w2_ref[:, cols].astype(BF16), preferred_element_type=F32)
        for b in range(rows // cap):
            rb = slice(b * cap, (b + 1) * cap)
            acc_ref[rb, cols] += y[rb] * g2_ref[b][:, cols]

    @pl.when(f == nf - 1)
    def _():
        issue_rows(acc_row_out)

        @pl.when(e == n_e - 1)
        def _():
            wait_acc_out()


def _moe(idx_flat, gate_col, h_packed, w1, w3, w2, mod3, facc0, *, rows, cap):
    m, d = facc0.shape
    n_e, _, dff = w1.shape
    tf = min(256, dff)
    kern = functools.partial(_moe_kernel, rows=rows, cap=cap)
    grid_spec = pltpu.PrefetchScalarGridSpec(
        num_scalar_prefetch=1,
        grid=(n_e, dff // tf),
        in_specs=[pl.BlockSpec((None, rows, 1), lambda e, f, idx: (e, 0, 0)),
                  pl.BlockSpec(memory_space=pl.ANY),
                  pl.BlockSpec((None, d, tf), lambda e, f, idx: (e, 0, f)),
                  pl.BlockSpec((None, d, tf), lambda e, f, idx: (e, 0, f)),
                  pl.BlockSpec((None, tf, d), lambda e, f, idx: (e, f, 0)),
                  pl.BlockSpec((ADA_ROWS, 1, d), lambda e, f, idx: (0, 0, 5)),
                  pl.BlockSpec(memory_space=pl.ANY)],
        out_specs=pl.BlockSpec(memory_space=pl.ANY),
        scratch_shapes=[pltpu.VMEM((2, rows * (d // 2 // LANES), LANES), I32), pltpu.VMEM((rows, d), F32),
                        pltpu.SemaphoreType.DMA((2,)), pltpu.SemaphoreType.DMA(()), pltpu.SemaphoreType.DMA(())],
    )
    return pl.pallas_call(
        kern,
        grid_spec=grid_spec,
        out_shape=jax.ShapeDtypeStruct((m, d), F32),
        input_output_aliases={7: 0},
        compiler_params=_cparams(("arbitrary", "arbitrary")),
    )(idx_flat, gate_col, h_packed, w1, w3, w2, mod3, facc0)


def _final_kernel(y_ref, lg_ref, lb_ref, o_ref):
    o_ref[...] = _layer_norm(y_ref[...], lg_ref[...], lb_ref[...])


def _final(pre_ln, ln_g, ln_b, *, n_lat):
    m, d = pre_ln.shape
    tm = min(512, n_lat)
    vec = pl.BlockSpec((1, d), lambda i: (0, 0))
    return pl.pallas_call(
        _final_kernel,
        grid=(m // tm,),
        in_specs=[pl.BlockSpec((tm, d), lambda i: (i, 0)), vec, vec],
        out_specs=pl.BlockSpec((tm, d), lambda i: (i, 0)),
        out_shape=jax.ShapeDtypeStruct((m, d), F32),
        compiler_params=_cparams(("arbitrary",)),
    )(pre_ln, ln_g.reshape(1, d), ln_b.reshape(1, d))


def _rope_tables(n_lat):
    rows = n_lat // GRID_W
    row = jnp.repeat(jnp.arange(rows, dtype=F32), GRID_W)
    col = jnp.tile(jnp.arange(GRID_W, dtype=F32), rows)
    n_freq = DIFF_DH // 4
    inv = ROPE_BASE ** (-jnp.arange(n_freq, dtype=F32) / n_freq)
    ang = jnp.concatenate([row[:, None] * inv, col[:, None] * inv], axis=-1)
    cos, sin = jnp.cos(ang), jnp.sin(ang)
    reps = LANES // DIFF_DH
    cos_t = jnp.tile(jnp.concatenate([cos, cos], axis=-1), (1, reps))
    sin_t = jnp.tile(jnp.concatenate([-sin, sin], axis=-1), (1, reps))
    return cos_t, sin_t


def kernel(x, c, ctx, c_ctx, w_ada, b_ada, w_in, w_gate2, b_gate, gla_norm_g, diff_lambda, diff_norm_g, w_o,
           ln1_g, ln1_b, w_router, w1, w3, w2, ln2_g, ln2_b):
    batch, n_lat, d = x.shape
    n_ctx = ctx.shape[1]
    assert w_ada.shape[0] == DEPTH == 1 and batch < ADA_ROWS
    n_e = w_router.shape[-1]
    cap = EC_CAPACITY_FACTOR * n_lat // n_e
    l = 0

    c_rows = jnp.concatenate([c, c_ctx[None, :], jnp.zeros((ADA_ROWS - batch - 1, d), F32)], axis=0)
    mod3 = _ada(c_rows, w_ada[l], b_ada[l]).reshape(ADA_ROWS, 1, 6 * d)

    gqkvr = 2 * (d // 4) + 2 * (d // 2)
    r2 = 2 * GLA_RANK
    w_in_t = w_in[l].T.astype(BF16)
    w_diff = w_in_t[gqkvr + r2:]
    w_lr = w_in_t[gqkvr:gqkvr + r2]
    hk = w_gate2.shape[-1]
    zeros = jnp.zeros((GLA_RANK, hk), F32)
    w_g = jnp.concatenate([jnp.concatenate([w_gate2[l, 0], zeros], axis=1),
                           jnp.concatenate([zeros, w_gate2[l, 1]], axis=1)], axis=0).astype(BF16)
    b_g = b_gate[l].reshape(1, 2 * hk)
    cos_t, sin_t = _rope_tables(n_lat)

    tm_l = min(1024, n_lat)
    tpb = n_lat // tm_l
    x2d = x.reshape(batch * n_lat, d)
    pf_l, pb_l, lg_l = _proj(x2d, mod3, lambda i: i // tpb, w_in_t, w_diff, w_lr, w_g, b_g, cos_t, sin_t,
                             rope=True, tm=tm_l, n_tab_tiles=tpb)
    ctx2d = ctx.reshape(batch * n_ctx, d)
    tm_c = min(1024, batch * n_ctx)
    pf_c, pb_c, lg_c = _proj(ctx2d, mod3, lambda i: batch, w_in_t, w_diff, w_lr, w_g, b_g, cos_t, sin_t,
                             rope=False, tm=tm_c, n_tab_tiles=1)

    gla_o = _gla(pf_l, lg_l, pf_c, lg_c, gla_norm_g[l], batch=batch, n_lat=n_lat, n_ctx=n_ctx)
    diff_o = _diff(pb_l, pb_c, diff_lambda[l], diff_norm_g[l], batch=batch, n_lat=n_lat, n_ctx=n_ctx)

    res, h_packed, aff_t = _out(gla_o, diff_o, x2d, w_o[l].astype(BF16), mod3, ln1_g[l], ln1_b[l],
                                w_router[l].T, batch=batch, n_lat=n_lat)

    idx, gate = _route(aff_t, cap=cap)
    rows = batch * cap
    idx_flat = idx.transpose(1, 0, 2).reshape(n_e * rows)
    gate_col = gate.transpose(1, 0, 2).reshape(n_e, rows, 1)
    pre_ln = _moe(idx_flat, gate_col, h_packed, w1[l], w3[l], w2[l], mod3, res, rows=rows, cap=cap)

    out = _final(pre_ln, ln2_g[l], ln2_b[l], n_lat=n_lat)
    return out.reshape(batch, n_lat, d)
```

```python
import functools
import math

import jax
import jax.numpy as jnp
import numpy as np
from jax import lax
from jax.experimental import pallas as pl
from jax.experimental.pallas import tpu as pltpu

F32 = jnp.float32
BF16 = jnp.bfloat16
I32 = jnp.int32

GRID_W = 64
GLA_HEADS = 4
GLA_RANK = 16
GLA_TAU = 16.0
GLA_CHUNK = 64
DIFF_HEADS = 8
DIFF_DH = 64
ROPE_BASE = 10000.0
EC_CAPACITY_FACTOR = 2
LN_EPS = 1e-5
RMS_EPS = 1e-6
DEPTH = 1
ALPHA = (2.0 * DEPTH) ** 0.25
LAM_INIT = 0.8 - 0.6 * math.exp(-0.3 * 0)

LANES = 128
ADA_ROWS = 8
VMEM_LIMIT = 60 * 1024 * 1024

_NT = (((1,), (1,)), ((), ()))
_TN = (((0,), (0,)), ((), ()))


def _cparams(sem):
    return pltpu.CompilerParams(dimension_semantics=sem, vmem_limit_bytes=VMEM_LIMIT)


def _silu(a):
    return a * (1.0 / (1.0 + jnp.exp(-a)))


def _log_sigmoid(z):
    return jnp.minimum(z, 0.0) - jnp.log1p(jnp.exp(-jnp.abs(z)))


def _ada_kernel(c_ref, w_ref, b_ref, o_ref):
    s = _silu(c_ref[...]).astype(BF16)
    o_ref[...] = jnp.dot(s, w_ref[...].astype(BF16), preferred_element_type=F32) + b_ref[...]


def _ada(c_rows, w_ada, b_ada):
    d, n_out = w_ada.shape
    tn = min(1024, n_out)
    return pl.pallas_call(
        _ada_kernel,
        grid=(n_out // tn,),
        in_specs=[pl.BlockSpec((ADA_ROWS, d), lambda j: (0, 0)),
                  pl.BlockSpec((d, tn), lambda j: (0, j)),
                  pl.BlockSpec((1, tn), lambda j: (0, j))],
        out_specs=pl.BlockSpec((ADA_ROWS, tn), lambda j: (0, j)),
        out_shape=jax.ShapeDtypeStruct((ADA_ROWS, n_out), F32),
        compiler_params=_cparams(("arbitrary",)),
    )(c_rows, w_ada, b_ada.reshape(1, n_out))


def _proj_kernel(x_ref, sh_ref, sc_ref, wa_ref, wb_ref, wlr_ref, wg_ref, bg_ref, cos_ref, sin_ref,
                 of_ref, ob_ref, lg_ref, u_ref, *, rope, n_f32, n_q, n_rope):
    n = pl.program_id(1)

    @pl.when(n == 0)
    def _():
        u = (x_ref[...] * (1.0 + sc_ref[...]) + sh_ref[...]).astype(BF16)
        u_ref[...] = u
        lr = lax.dot_general(u, wlr_ref[...], _NT, preferred_element_type=F32)
        z = jnp.dot(lr.astype(BF16), wg_ref[...], preferred_element_type=F32) + bg_ref[...]
        lg_ref[...] = _log_sigmoid(z) * (1.0 / GLA_TAU)

    @pl.when(n < n_f32)
    def _():
        of_ref[...] = lax.dot_general(u_ref[...], wa_ref[...], _NT, preferred_element_type=F32)

    @pl.when(jnp.logical_and(n >= n_f32, n < n_f32 + n_rope))
    def _():
        acc = lax.dot_general(u_ref[...], wb_ref[...], _NT, preferred_element_type=F32)
        scale = jnp.where(n < n_f32 + n_q, DIFF_DH ** -0.5, 1.0).astype(F32)
        if rope:
            lane = lax.broadcasted_iota(I32, (acc.shape[0], LANES), 1)
            first = (lane % DIFF_DH) < (DIFF_DH // 2)
            cs = cos_ref[...]
            sn = sin_ref[...]
            for j in range(acc.shape[1] // LANES):
                a = acc[:, j * LANES:(j + 1) * LANES]
                partner = jnp.where(first, pltpu.roll(a, LANES - DIFF_DH // 2, 1), pltpu.roll(a, DIFF_DH // 2, 1))
                ob_ref[:, j * LANES:(j + 1) * LANES] = ((a * cs + partner * sn) * scale).astype(BF16)
        else:
            ob_ref[...] = (acc * scale).astype(BF16)

    @pl.when(n >= n_f32 + n_rope)
    def _():
        ob_ref[...] = lax.dot_general(u_ref[...], wb_ref[...], _NT, preferred_element_type=F32).astype(BF16)


def _proj(x2d, mod3, row_of_tile, w_in, w_diff, w_lr, w_g, b_g, cos_t, sin_t, *, rope, tm, n_tab_tiles):
    m, d = x2d.shape
    tn = 512
    n_f32 = w_diff.shape[0] // tn
    n_tiles = 2 * n_f32
    n_q = n_tiles // 6
    n_rope = n_tiles // 3
    r2 = w_lr.shape[0]
    ng = w_g.shape[1]
    kern = functools.partial(_proj_kernel, rope=rope, n_f32=n_f32, n_q=n_q, n_rope=n_rope)
    return pl.pallas_call(
        kern,
        grid=(m // tm, n_tiles),
        in_specs=[pl.BlockSpec((tm, d), lambda i, j: (i, 0)),
                  pl.BlockSpec((None, 1, d), lambda i, j: (row_of_tile(i), 0, 0)),
                  pl.BlockSpec((None, 1, d), lambda i, j: (row_of_tile(i), 0, 1)),
                  pl.BlockSpec((tn, d), lambda i, j: (jnp.minimum(j, n_f32 - 1), 0)),
                  pl.BlockSpec((tn, d), lambda i, j: (jnp.maximum(j - n_f32, 0), 0)),
                  pl.BlockSpec((r2, d), lambda i, j: (0, 0)),
                  pl.BlockSpec((r2, ng), lambda i, j: (0, 0)),
                  pl.BlockSpec((1, ng), lambda i, j: (0, 0)),
                  pl.BlockSpec((tm, LANES), lambda i, j: (i % n_tab_tiles, 0)),
                  pl.BlockSpec((tm, LANES), lambda i, j: (i % n_tab_tiles, 0))],
        out_specs=[pl.BlockSpec((tm, tn), lambda i, j: (i, jnp.minimum(j, n_f32 - 1))),
                   pl.BlockSpec((tm, tn), lambda i, j: (i, jnp.maximum(j - n_f32, 0))),
                   pl.BlockSpec((tm, ng), lambda i, j: (i, 0))],
        out_shape=[jax.ShapeDtypeStruct((m, n_f32 * tn), F32),
                   jax.ShapeDtypeStruct((m, (n_tiles - n_f32) * tn), BF16),
                   jax.ShapeDtypeStruct((m, ng), F32)],
        scratch_shapes=[pltpu.VMEM((tm, d), BF16)],
        compiler_params=_cparams(("arbitrary", "arbitrary")),
    )(x2d, mod3, mod3, w_in, w_diff, w_lr, w_g, b_g, cos_t, sin_t)


GLA_GROUP = 4
GLA_UNROLL = 2


def _split3(a):
    hi = a.astype(BF16)
    r1 = a - hi.astype(F32)
    mid = r1.astype(BF16)
    return hi, mid, (r1 - mid.astype(F32)).astype(BF16)


def _gla_groups(dirs, *, want_out, scale):
    c = GLA_CHUNK
    rows, dk = dirs[0]["k"].shape
    n_chunks = rows // c
    for d in dirs:
        d["parts"] = jnp.dot(d["sum_mat"], jnp.concatenate(_split3(d["g"]), axis=1),
                             preferred_element_type=F32)
    for d in dirs:
        parts = d["parts"]
        sums = (parts[:, :dk] + parts[:, dk:2 * dk]) + parts[:, 2 * dk:]
        bcum = sums[:rows]
        btot = sums[rows:]
        d["vb"] = d["v"].astype(BF16)
        d["k_out"] = (d["k"] * jnp.exp(btot - bcum)).astype(BF16)
        d["decay"] = jnp.exp(btot)
        if want_out:
            d["q_in"] = ((d["q"] * scale) * jnp.exp(bcum)).astype(BF16)
            d["k_in"] = (d["k"] * jnp.exp(-bcum)).astype(BF16)
    if want_out:
        for d in dirs:
            d["att"] = lax.dot_general(d["q_in"], d["k_in"], _NT, preferred_element_type=F32)
        for d in dirs:
            d["att"] = jnp.where(d["causal"], d["att"], 0.0).astype(BF16)
        for d in dirs:
            d["intra"] = jnp.dot(d["att"], d["vb"], preferred_element_type=F32)
    chunk = lambda ci: slice(ci * c, (ci + 1) * c)
    for d in dirs:
        d["upd"] = [lax.dot_general(d["vb"][chunk(ci)], d["k_out"][chunk(ci)], _TN, preferred_element_type=F32)
                    for ci in range(n_chunks)]
    states = {}
    for d in dirs:
        st = states.get(id(d["st_ref"]))
        if st is None:
            st = d["st_ref"][...]
        d["st_in"] = [None] * n_chunks
        for ci in (range(n_chunks) if d["forward"] else reversed(range(n_chunks))):
            d["st_in"][ci] = st
            st = st * d["decay"][ci * c:ci * c + 1] + d["upd"][ci]
        states[id(d["st_ref"])] = st
    for d in dirs:
        d["st_ref"][...] = states[id(d["st_ref"])]
    if not want_out:
        return [None for _ in dirs]
    outs = []
    for d in dirs:
        outs.append(jnp.concatenate(
            [d["intra"][chunk(ci)] + lax.dot_general(d["q_in"][chunk(ci)], d["st_in"][ci].astype(BF16), _NT,
                                                     preferred_element_type=F32)
             for ci in range(n_chunks)], axis=0))
    return outs


def _gla_kernel(q_ref, k_ref, v_ref, r_ref, gf_ref, gb_ref, kc_ref, vc_ref, gfc_ref, gbc_ref, ng_ref,
                o_ref, of_ref, obk_ref, sf_ref, sb_ref, *, n_lat, n_ctx, dk):
    c = GLA_CHUNK
    scale = dk ** -0.5
    sf_ref[...] = jnp.zeros_like(sf_ref)
    sb_ref[...] = jnp.zeros_like(sb_ref)

    def masks(rows):
        row = lax.broadcasted_iota(I32, (rows, rows), 0)
        col = lax.broadcasted_iota(I32, (rows, rows), 1)
        same = (row // c) == (col // c)
        lower = jnp.logical_and(same, row >= col)
        upper = jnp.logical_and(same, row <= col)
        total = same.astype(BF16)
        return (lower, jnp.concatenate([lower.astype(BF16), total], axis=0),
                upper, jnp.concatenate([upper.astype(BF16), total], axis=0))

    def scan(n, refs_f, refs_b, want_out):
        rows = min(GLA_GROUP * c, n)
        lower, sum_f, upper, sum_b = masks(rows)
        n_groups = n // rows
        unroll = GLA_UNROLL if n_groups % GLA_UNROLL == 0 else 1
        ld = lambda ref, r0: None if ref is None else ref[pl.ds(r0, rows), :]
        names = ("q", "k", "v", "g")

        def body(i, carry):
            items, stores = [], []
            for u in range(unroll):
                rf = pl.multiple_of((i * unroll + u) * rows, rows)
                rb = pl.multiple_of((n_groups - 1 - (i * unroll + u)) * rows, rows)
                items.append(dict(zip(names, (ld(r, rf) for r in refs_f)), st_ref=sf_ref, sum_mat=sum_f,
                                  causal=lower, forward=True))
                items.append(dict(zip(names, (ld(r, rb) for r in refs_b)), st_ref=sb_ref, sum_mat=sum_b,
                                  causal=upper, forward=False))
                stores += [(of_ref, rf), (obk_ref, rb)]
            outs = _gla_groups(items, want_out=want_out, scale=scale)
            if want_out:
                for (ref, r0), out in zip(stores, outs):
                    ref[pl.ds(r0, rows), :] = out
            return carry

        lax.fori_loop(0, n_groups // unroll, body, 0)

    scan(n_ctx, (None, kc_ref, vc_ref, gfc_ref), (None, kc_ref, vc_ref, gbc_ref), False)
    scan(n_lat, (q_ref, k_ref, v_ref, gf_ref), (q_ref, k_ref, v_ref, gb_ref), True)

    tb = min(512, n_lat)

    def fin_body(i, carry):
        r0 = pl.multiple_of(i * tb, tb)
        o = of_ref[pl.ds(r0, tb), :] + obk_ref[pl.ds(r0, tb), :]
        ms = jnp.mean(o * o, axis=-1, keepdims=True)
        o = o * lax.rsqrt(ms + RMS_EPS) * ng_ref[...]
        o_ref[pl.ds(r0, tb), :] = (o * _silu(r_ref[pl.ds(r0, tb), :])).astype(o_ref.dtype)
        return carry

    lax.fori_loop(0, n_lat // tb, fin_body, 0)


def _gla(pf_l, lg_l, pf_c, lg_c, norm_g, *, batch, n_lat, n_ctx):
    h = GLA_HEADS
    dk = pf_l.shape[1] // 6 // h
    dv = 2 * dk
    kern = functools.partial(_gla_kernel, n_lat=n_lat, n_ctx=n_ctx, dk=dk)
    lat = lambda width, off: pl.BlockSpec((n_lat, width), lambda b, i: (b, off + i))
    ctx = lambda width, off: pl.BlockSpec((n_ctx, width), lambda b, i: (b, off + i))
    return pl.pallas_call(
        kern,
        grid=(batch, h),
        in_specs=[lat(dk, 0), lat(dk, h), lat(dv, h), lat(dv, 2 * h),
                  lat(dk, 0), lat(dk, h),
                  ctx(dk, h), ctx(dv, h), ctx(dk, 0), ctx(dk, h),
                  pl.BlockSpec((1, dv), lambda b, i: (0, 0))],
        out_specs=pl.BlockSpec((n_lat, dv), lambda b, i: (b, i)),
        out_shape=jax.ShapeDtypeStruct((batch * n_lat, h * dv), BF16),
        scratch_shapes=[pltpu.VMEM((n_lat, dv), F32), pltpu.VMEM((n_lat, dv), F32),
                        pltpu.VMEM((dv, dk), F32), pltpu.VMEM((dv, dk), F32)],
        compiler_params=_cparams(("arbitrary", "arbitrary")),
    )(pf_l, pf_l, pf_l, pf_l, lg_l, lg_l, pf_c, pf_c, lg_c, lg_c, norm_g.reshape(1, dv))


DIFF_KEY_CHUNK = 512


def _diff_step(q_ref, kc_ref, kl_ref, vc_ref, vl_ref, lam_ref, ng_ref, o_ref, acc_ref, s_w, m_w, s_r, m_r):
    tq = o_ref.shape[0]
    dv = o_ref.shape[1]
    done = acc_ref[...]
    dl = lam_ref[...]
    lam = (jnp.exp(jnp.sum(dl[0:1] * dl[1:2], axis=-1, keepdims=True))
           - jnp.exp(jnp.sum(dl[2:3] * dl[3:4], axis=-1, keepdims=True)) + LAM_INIT)
    inv = 1.0 / done[:, dv:dv + 1]
    o = done[:tq, :dv] * inv[:tq] - done[tq:, :dv] * (inv[tq:] * lam)
    ms = jnp.mean(o * o, axis=-1, keepdims=True)
    o_ref[...] = (o * lax.rsqrt(ms + RMS_EPS) * ng_ref[...] * (1.0 - LAM_INIT)).astype(o_ref.dtype)

    q = q_ref[...]
    lane = lax.broadcasted_iota(I32, q.shape, 1)
    zero = jnp.zeros_like(q)
    qq = jnp.concatenate([jnp.where(lane < DIFF_DH, q, zero), jnp.where(lane >= DIFF_DH, q, zero)], axis=0)
    mx = jnp.max(m_r[...], axis=-1, keepdims=True)
    m_new = jnp.full((2 * tq, LANES), -jnp.inf, F32)
    acc = jnp.zeros((2 * tq, 2 * dv), F32)
    col = 0
    for k_ref, v_ref in ((kc_ref, vc_ref), (kl_ref, vl_ref)):
        n_keys = k_ref.shape[0]
        w = min(DIFF_KEY_CHUNK, n_keys)
        ones_col = (lax.broadcasted_iota(I32, (w, dv), 1) == 0).astype(BF16)
        for r0 in range(0, n_keys, w):
            s_new = lax.dot_general(qq, k_ref[r0:r0 + w, :], _NT, preferred_element_type=F32)
            s_w[:, col:col + w] = s_new
            for t in range(w // LANES):
                m_new = jnp.maximum(m_new, s_new[:, t * LANES:(t + 1) * LANES])
            e = jnp.exp(s_r[:, col:col + w] - mx).astype(BF16)
            v_ext = jnp.concatenate([v_ref[r0:r0 + w, :], ones_col], axis=1)
            acc = acc + jnp.dot(e, v_ext, preferred_element_type=F32)
            col += w
    m_w[...] = m_new
    acc_ref[...] = acc


def _diff_kernel(q_ref, kl_ref, kc_ref, vl_ref, vc_ref, lam_ref, ng_ref, o_ref, acc_ref,
                 sa_ref, sb_ref, ma_ref, mb_ref):
    j = pl.program_id(1)

    @pl.when(j == 0)
    def _():
        sb_ref[...] = jnp.zeros_like(sb_ref)
        mb_ref[...] = jnp.zeros_like(mb_ref)
        acc_ref[...] = jnp.ones_like(acc_ref)

    args = (q_ref, kc_ref, kl_ref, vc_ref, vl_ref, lam_ref, ng_ref, o_ref, acc_ref)

    @pl.when(j % 2 == 0)
    def _():
        _diff_step(*args, sa_ref, ma_ref, sb_ref, mb_ref)

    @pl.when(j % 2 == 1)
    def _():
        _diff_step(*args, sb_ref, mb_ref, sa_ref, ma_ref)


def _diff(pb_l, pb_c, diff_lambda, norm_g, *, batch, n_lat, n_ctx):
    h = DIFF_HEADS
    dv = 2 * DIFF_DH
    tq = min(256, n_lat)
    nq = n_lat // tq
    nblk = h * nq
    cur = lambda j: jnp.minimum(j, nblk - 1)
    prev = lambda j: jnp.clip(j - 1, 0, nblk - 1)
    fin = lambda j: jnp.maximum(j - 2, 0)
    return pl.pallas_call(
        _diff_kernel,
        grid=(batch, nblk + 2),
        in_specs=[pl.BlockSpec((tq, dv), lambda b, j: (b * nq + cur(j) % nq, cur(j) // nq)),
                  pl.BlockSpec((n_lat, dv), lambda b, j: (b, h + cur(j) // nq)),
                  pl.BlockSpec((n_ctx, dv), lambda b, j: (b, h + cur(j) // nq)),
                  pl.BlockSpec((n_lat, dv), lambda b, j: (b, 2 * h + prev(j) // nq)),
                  pl.BlockSpec((n_ctx, dv), lambda b, j: (b, 2 * h + prev(j) // nq)),
                  pl.BlockSpec(diff_lambda.shape, lambda b, j: (0, 0)),
                  pl.BlockSpec((1, dv), lambda b, j: (0, 0))],
        out_specs=pl.BlockSpec((tq, dv), lambda b, j: (b * nq + fin(j) % nq, fin(j) // nq)),
        out_shape=jax.ShapeDtypeStruct((batch * n_lat, h * dv), BF16),
        scratch_shapes=[pltpu.VMEM((2 * tq, 2 * dv), F32),
                        pltpu.VMEM((2 * tq, n_ctx + n_lat), F32), pltpu.VMEM((2 * tq, n_ctx + n_lat), F32),
                        pltpu.VMEM((2 * tq, LANES), F32), pltpu.VMEM((2 * tq, LANES), F32)],
        compiler_params=_cparams(("arbitrary", "arbitrary")),
    )(pb_l, pb_l, pb_c, pb_l, pb_c, diff_lambda, norm_g.reshape(1, dv))


def _layer_norm(y, g, b):
    mu = jnp.mean(y, axis=-1, keepdims=True)
    yc = y - mu
    var = jnp.mean(yc * yc, axis=-1, keepdims=True)
    return yc * lax.rsqrt(var + LN_EPS) * g + b


def _split2(a):
    hi = a.astype(BF16)
    return hi, (a - hi.astype(F32)).astype(BF16)


def _out_kernel(ga_ref, df_ref, x_ref, wo_ref, g1_ref, sh2_ref, sc2_ref, lg_ref, lb_ref, wr_ref,
                x1_ref, h_ref, aff_ref):
    half = ga_ref.shape[1]
    o = (jnp.dot(ga_ref[...], wo_ref[:half, :], preferred_element_type=F32)
         + jnp.dot(df_ref[...], wo_ref[half:, :], preferred_element_type=F32))
    x1 = _layer_norm(ALPHA * x_ref[...] + g1_ref[...] * o, lg_ref[...], lb_ref[...])
    x1_ref[...] = ALPHA * x1
    hmod = x1 * (1.0 + sc2_ref[...]) + sh2_ref[...]
    hbits = pltpu.bitcast(hmod.astype(BF16).astype(F32), I32)
    hw = hbits.shape[1] // 2
    packed = (hbits[:, hw:] & jnp.int32(-65536)) | lax.shift_right_logical(hbits[:, :hw], 16)
    tm = packed.shape[0]
    wt = hw // LANES
    for s in range(wt):
        h_ref[pl.ds(s, tm, stride=wt), :] = packed[:, s * LANES:(s + 1) * LANES]
    h_hi, h_lo = _split2(hmod)
    w_hi, w_lo = _split2(wr_ref[...])
    logits = (lax.dot_general(w_hi, h_hi, _NT, preferred_element_type=F32)
              + lax.dot_general(w_lo, h_hi, _NT, preferred_element_type=F32)
              + lax.dot_general(w_hi, h_lo, _NT, preferred_element_type=F32))
    e = jnp.exp(logits - jnp.max(logits, axis=0, keepdims=True))
    aff_ref[...] = e / jnp.sum(e, axis=0, keepdims=True)


def _out(gla_o, diff_o, x2d, w_o, mod3, ln_g, ln_b, w_router_t, *, batch, n_lat):
    m, d = x2d.shape
    half = gla_o.shape[1]
    n_e = w_router_t.shape[0]
    tm = min(512, n_lat)
    tpb = n_lat // tm
    modspec = lambda k: pl.BlockSpec((None, 1, d), lambda i: (i // tpb, 0, k))
    vec = pl.BlockSpec((1, d), lambda i: (0, 0))
    return pl.pallas_call(
        _out_kernel,
        grid=(m // tm,),
        in_specs=[pl.BlockSpec((tm, half), lambda i: (i, 0)),
                  pl.BlockSpec((tm, half), lambda i: (i, 0)),
                  pl.BlockSpec((tm, d), lambda i: (i, 0)),
                  pl.BlockSpec((2 * half, d), lambda i: (0, 0), pipeline_mode=pl.Buffered(1)),
                  modspec(2), modspec(3), modspec(4), vec, vec,
                  pl.BlockSpec((n_e, d), lambda i: (0, 0))],
        out_specs=[pl.BlockSpec((tm, d), lambda i: (i, 0)),
                   pl.BlockSpec((tm * (d // 2 // LANES), LANES), lambda i: (i, 0)),
                   pl.BlockSpec((None, n_e, tm), lambda i: (i // tpb, 0, i % tpb))],
        out_shape=[jax.ShapeDtypeStruct((m, d), F32),
                   jax.ShapeDtypeStruct((m * (d // 2 // LANES), LANES), I32),
                   jax.ShapeDtypeStruct((batch, n_e, n_lat), F32)],
        compiler_params=_cparams(("arbitrary",)),
    )(gla_o, diff_o, x2d, w_o, mod3, mod3, mod3, ln_g.reshape(1, d), ln_b.reshape(1, d), w_router_t)


def _prefix_count(mask_f32, strict_upper):
    rows, n = mask_f32.shape
    carry = jnp.zeros((rows, 1), F32)
    pieces = []
    for j in range(n // LANES):
        blk = mask_f32[:, j * LANES:(j + 1) * LANES]
        within = jnp.dot(blk.astype(BF16), strict_upper, preferred_element_type=F32)
        pieces.append(within + carry)
        carry = carry + jnp.sum(blk, axis=-1, keepdims=True)
    return jnp.concatenate(pieces, axis=-1)


def _route_kernel(aff_ref, idx_ref, gate_ref, pos_ref, *, cap):
    b = pl.program_id(0)
    v = aff_ref[...]
    n_e, n = v.shape
    bits = pltpu.bitcast(v, I32)

    def search(i, t):
        cand = t | (1 << (30 - i))
        cnt = jnp.sum((bits >= cand).astype(I32), axis=-1, keepdims=True)
        return jnp.where(cnt >= cap, cand, t)

    thr = lax.fori_loop(0, 31, search, jnp.zeros((n_e, 1), I32))
    gt = bits > thr
    eq = bits == thr
    need = (cap - jnp.sum(gt.astype(I32), axis=-1, keepdims=True)).astype(F32)
    r = lax.broadcasted_iota(I32, (LANES, LANES), 0)
    cidx = lax.broadcasted_iota(I32, (LANES, LANES), 1)
    strict_upper = (r < cidx).astype(BF16)
    eq_rank = _prefix_count(eq.astype(F32), strict_upper)
    sel = jnp.logical_or(gt, jnp.logical_and(eq, eq_rank < need))
    pos = _prefix_count(sel.astype(F32), strict_upper)
    pos_ref[...] = jnp.where(sel, pos, -1.0).astype(I32)

    tok = lax.broadcasted_iota(I32, (8, n), 1)
    srow = lax.broadcasted_iota(I32, (8, n), 0)
    tok_hi = (tok >> 6).astype(F32)
    tok_lo = (tok & 63).astype(F32)
    slot = lax.broadcasted_iota(I32, (cap, n), 0)

    def compact(e, carry):
        ve = aff_ref[pl.ds(e, 1), :]
        v_hi = ve.astype(BF16).astype(F32)
        r1 = ve - v_hi
        v_mid = r1.astype(BF16).astype(F32)
        v_lo = r1 - v_mid
        lhs = jnp.where(srow == 0, tok_hi,
              jnp.where(srow == 1, tok_lo,
              jnp.where(srow == 2, v_hi,
              jnp.where(srow == 3, v_mid,
              jnp.where(srow == 4, v_lo, 0.0))))).astype(BF16)
        onehot = (slot == pos_ref[pl.ds(e, 1), :]).astype(F32).astype(BF16)
        res = lax.dot_general(lhs, onehot, _NT, preferred_element_type=F32)
        idx_ref[pl.ds(e, 1), :] = (res[0:1] * 64.0 + res[1:2]).astype(I32) + b * n
        gate_ref[pl.ds(e, 1), :] = (res[2:3] + res[3:4]) + res[4:5]
        return carry

    lax.fori_loop(0, n_e, compact, 0)


def _route(aff_t, *, cap):
    batch, n_e, n = aff_t.shape
    kern = functools.partial(_route_kernel, cap=cap)
    return pl.pallas_call(
        kern,
        grid=(batch,),
        in_specs=[pl.BlockSpec((None, n_e, n), lambda b: (b, 0, 0))],
        out_specs=[pl.BlockSpec((None, n_e, cap), lambda b: (b, 0, 0)),
                   pl.BlockSpec((None, n_e, cap), lambda b: (b, 0, 0))],
        out_shape=[jax.ShapeDtypeStruct((batch, n_e, cap), I32),
                   jax.ShapeDtypeStruct((batch, n_e, cap), F32)],
        scratch_shapes=[pltpu.VMEM((n_e, n), I32)],
        compiler_params=_cparams(("arbitrary",)),
    )(aff_t)


DMA_UNROLL = 8


def _moe_kernel(idx_ref, gate_ref, h_hbm, w1_ref, w3_ref, w2_ref, g2_ref, facc_in, facc_hbm,
                stage_ref, acc_ref, gsem, rsem, wsem, *, rows, cap):
    del facc_in
    e = pl.program_id(0)
    f = pl.program_id(1)
    n_e = pl.num_programs(0)
    nf = pl.num_programs(1)
    slot = e % 2

    def issue_rows(make):
        def body(i, carry):
            for u in range(DMA_UNROLL):
                make(i, u).start()
            return carry
        lax.fori_loop(0, rows // DMA_UNROLL, body, 0)

    wt = stage_ref.shape[1] // rows
    assert acc_ref.shape[1] == DMA_UNROLL

    def x_row(ex, s):
        def make(i, u):
            k = i * DMA_UNROLL + u
            src = pl.multiple_of(idx_ref[ex * rows + k] * wt, wt)
            dst = pl.multiple_of(k * wt, wt)
            return pltpu.make_async_copy(h_hbm.at[pl.ds(src, wt), :], stage_ref.at[s, pl.ds(dst, wt), :], gsem.at[s])
        return make

    def acc_row_in(i, u):
        tok = idx_ref[e * rows + i * DMA_UNROLL + u]
        return pltpu.make_async_copy(facc_hbm.at[pl.ds(tok, 1), :], acc_ref.at[i, pl.ds(u, 1), :], rsem)

    def acc_row_out(i, u):
        tok = idx_ref[e * rows + i * DMA_UNROLL + u]
        return pltpu.make_async_copy(acc_ref.at[i, pl.ds(u, 1), :], facc_hbm.at[pl.ds(tok, 1), :], wsem)

    def wait_x(s):
        pltpu.make_async_copy(h_hbm.at[pl.ds(0, rows * wt), :], stage_ref.at[s], gsem.at[s]).wait()

    def wait_acc_in():
        pltpu.make_async_copy(acc_ref, acc_ref, rsem).wait()

    def wait_acc_out():
        pltpu.make_async_copy(acc_ref, acc_ref, wsem).wait()

    @pl.when(f == 0)
    def _():
        @pl.when(e > 0)
        def _():
            wait_acc_out()
        issue_rows(acc_row_in)

        @pl.when(e == 0)
        def _():
            issue_rows(x_row(0, 0))
        wait_x(slot)

        @pl.when(e + 1 < n_e)
        def _():
            issue_rows(x_row(e + 1, 1 - slot))
        wait_acc_in()

    words = [stage_ref[slot, pl.ds(s, rows, stride=wt), :] for s in range(wt)]
    half = wt * LANES
    x_lo = jnp.concatenate([pltpu.bitcast(u << 16, F32).astype(BF16) for u in words], axis=1)
    x_hi = jnp.concatenate([pltpu.bitcast(u & jnp.int32(-65536), F32).astype(BF16) for u in words],
                           axis=1)
    tf = w1_ref.shape[1]
    w13 = jnp.concatenate([w1_ref[...], w3_ref[...]], axis=1).astype(BF16)
    ag = (jnp.dot(x_lo, w13[:half], preferred_element_type=F32)
          + jnp.dot(x_hi, w13[half:], preferred_element_type=F32))
    hid = (_silu(ag[:, :tf]) * ag[:, tf:]).astype(BF16)
    gate = gate_ref[...]
    sub = acc_ref.shape[1]
    d = acc_ref.shape[2]
    tc = min(512, d)
    for j in range(d // tc):
        cols = slice(j * tc, (j + 1) * tc)
        y = gate * jnp.dot(hid, w2_ref[:, cols].astype(BF16), preferred_element_type=F32)
        for b in range(rows // cap):
            yb = y[b * cap:(b + 1) * cap] * g2_ref[b][:, cols]
            acc_ref[b * cap // sub:(b + 1) * cap // sub, :, cols] += yb.reshape(cap // sub, sub, tc)

    @pl.when(f == nf - 1)
    def _():
        issue_rows(acc_row_out)

        @pl.when(e == n_e - 1)
        def _():
            wait_acc_out()


def _moe(idx_flat, gate_col, h_packed, w1, w3, w2, mod3, facc0, *, rows, cap):
    m, d = facc0.shape
    n_e, _, dff = w1.shape
    tf = min(256, dff)
    kern = functools.partial(_moe_kernel, rows=rows, cap=cap)
    grid_spec = pltpu.PrefetchScalarGridSpec(
        num_scalar_prefetch=1,
        grid=(n_e, dff // tf),
        in_specs=[pl.BlockSpec((None, rows, 1), lambda e, f, idx: (e, 0, 0)),
                  pl.BlockSpec(memory_space=pl.ANY),
                  pl.BlockSpec((None, d, tf), lambda e, f, idx: (e, 0, f)),
                  pl.BlockSpec((None, d, tf), lambda e, f, idx: (e, 0, f)),
                  pl.BlockSpec((None, tf, d), lambda e, f, idx: (e, f, 0)),
                  pl.BlockSpec((ADA_ROWS, 1, d), lambda e, f, idx: (0, 0, 5)),
                  pl.BlockSpec(memory_space=pl.ANY)],
        out_specs=pl.BlockSpec(memory_space=pl.ANY),
        scratch_shapes=[pltpu.VMEM((2, rows * (d // 2 // LANES), LANES), I32),
                        pltpu.VMEM((rows // DMA_UNROLL, DMA_UNROLL, d), F32),
                        pltpu.SemaphoreType.DMA((2,)), pltpu.SemaphoreType.DMA(()), pltpu.SemaphoreType.DMA(())],
    )
    return pl.pallas_call(
        kern,
        grid_spec=grid_spec,
        out_shape=jax.ShapeDtypeStruct((m, d), F32),
        input_output_aliases={7: 0},
        compiler_params=_cparams(("arbitrary", "arbitrary")),
    )(idx_flat, gate_col, h_packed, w1, w3, w2, mod3, facc0)


def _final_kernel(y_ref, lg_ref, lb_ref, o_ref):
    o_ref[...] = _layer_norm(y_ref[...], lg_ref[...], lb_ref[...])


def _final(pre_ln, ln_g, ln_b, *, n_lat):
    m, d = pre_ln.shape
    tm = min(512, n_lat)
    vec = pl.BlockSpec((1, d), lambda i: (0, 0))
    return pl.pallas_call(
        _final_kernel,
        grid=(m // tm,),
        in_specs=[pl.BlockSpec((tm, d), lambda i: (i, 0)), vec, vec],
        out_specs=pl.BlockSpec((tm, d), lambda i: (i, 0)),
        out_shape=jax.ShapeDtypeStruct((m, d), F32),
        compiler_params=_cparams(("arbitrary",)),
    )(pre_ln, ln_g.reshape(1, d), ln_b.reshape(1, d))


def _rope_tables(n_lat):
    rows = n_lat // GRID_W
    row = jnp.repeat(jnp.arange(rows, dtype=F32), GRID_W)
    col = jnp.tile(jnp.arange(GRID_W, dtype=F32), rows)
    n_freq = DIFF_DH // 4
    inv = ROPE_BASE ** (-jnp.arange(n_freq, dtype=F32) / n_freq)
    ang = jnp.concatenate([row[:, None] * inv, col[:, None] * inv], axis=-1)
    cos, sin = jnp.cos(ang), jnp.sin(ang)
    reps = LANES // DIFF_DH
    cos_t = jnp.tile(jnp.concatenate([cos, cos], axis=-1), (1, reps))
    sin_t = jnp.tile(jnp.concatenate([-sin, sin], axis=-1), (1, reps))
    return cos_t, sin_t


def kernel(x, c, ctx, c_ctx, w_ada, b_ada, w_in, w_gate2, b_gate, gla_norm_g, diff_lambda, diff_norm_g, w_o,
           ln1_g, ln1_b, w_router, w1, w3, w2, ln2_g, ln2_b):
    batch, n_lat, d = x.shape
    n_ctx = ctx.shape[1]
    assert w_ada.shape[0] == DEPTH == 1 and batch < ADA_ROWS
    n_e = w_router.shape[-1]
    cap = EC_CAPACITY_FACTOR * n_lat // n_e
    l = 0

    c_rows = jnp.concatenate([c, c_ctx[None, :], jnp.zeros((ADA_ROWS - batch - 1, d), F32)], axis=0)
    mod3 = _ada(c_rows, w_ada[l], b_ada[l]).reshape(ADA_ROWS, 1, 6 * d)

    gqkvr = 2 * (d // 4) + 2 * (d // 2)
    r2 = 2 * GLA_RANK
    w_in_t = w_in[l].T.astype(BF16)
    w_diff = w_in_t[gqkvr + r2:]
    w_lr = w_in_t[gqkvr:gqkvr + r2]
    hk = w_gate2.shape[-1]
    zeros = jnp.zeros((GLA_RANK, hk), F32)
    w_g = jnp.concatenate([jnp.concatenate([w_gate2[l, 0], zeros], axis=1),
                           jnp.concatenate([zeros, w_gate2[l, 1]], axis=1)], axis=0).astype(BF16)
    b_g = b_gate[l].reshape(1, 2 * hk)
    cos_t, sin_t = _rope_tables(n_lat)

    tm_l = min(1024, n_lat)
    tpb = n_lat // tm_l
    x2d = x.reshape(batch * n_lat, d)
    pf_l, pb_l, lg_l = _proj(x2d, mod3, lambda i: i // tpb, w_in_t, w_diff, w_lr, w_g, b_g, cos_t, sin_t,
                             rope=True, tm=tm_l, n_tab_tiles=tpb)
    ctx2d = ctx.reshape(batch * n_ctx, d)
    tm_c = min(1024, batch * n_ctx)
    pf_c, pb_c, lg_c = _proj(ctx2d, mod3, lambda i: batch, w_in_t, w_diff, w_lr, w_g, b_g, cos_t, sin_t,
                             rope=False, tm=tm_c, n_tab_tiles=1)

    gla_o = _gla(pf_l, lg_l, pf_c, lg_c, gla_norm_g[l], batch=batch, n_lat=n_lat, n_ctx=n_ctx)
    diff_o = _diff(pb_l, pb_c, diff_lambda[l], diff_norm_g[l], batch=batch, n_lat=n_lat, n_ctx=n_ctx)

    res, h_packed, aff_t = _out(gla_o, diff_o, x2d, w_o[l].astype(BF16), mod3, ln1_g[l], ln1_b[l],
                                w_router[l].T, batch=batch, n_lat=n_lat)

    idx, gate = _route(aff_t, cap=cap)
    rows = batch * cap
    idx_flat = idx.transpose(1, 0, 2).reshape(n_e * rows)
    gate_col = gate.transpose(1, 0, 2).reshape(n_e, rows, 1)
    pre_ln = _moe(idx_flat, gate_col, h_packed, w1[l], w3[l], w2[l], mod3, res, rows=rows, cap=cap)

    out = _final(pre_ln, ln2_g[l], ln2_b[l], n_lat=n_lat)
    return out.reshape(batch, n_lat, d)
```

```python
import functools
import math

import jax
import jax.numpy as jnp
import numpy as np
from jax import lax
from jax.experimental import pallas as pl
from jax.experimental.pallas import tpu as pltpu

F32 = jnp.float32
BF16 = jnp.bfloat16
I32 = jnp.int32

GRID_W = 64
GLA_HEADS = 4
GLA_RANK = 16
GLA_TAU = 16.0
GLA_CHUNK = 64
DIFF_HEADS = 8
DIFF_DH = 64
ROPE_BASE = 10000.0
EC_CAPACITY_FACTOR = 2
LN_EPS = 1e-5
RMS_EPS = 1e-6
DEPTH = 1
ALPHA = (2.0 * DEPTH) ** 0.25
LAM_INIT = 0.8 - 0.6 * math.exp(-0.3 * 0)

LANES = 128
ADA_ROWS = 8
VMEM_LIMIT = 60 * 1024 * 1024

_NT = (((1,), (1,)), ((), ()))
_TN = (((0,), (0,)), ((), ()))


def _cparams(sem):
    return pltpu.CompilerParams(dimension_semantics=sem, vmem_limit_bytes=VMEM_LIMIT)


def _silu(a):
    return a * (1.0 / (1.0 + jnp.exp(-a)))


def _log_sigmoid(z):
    return jnp.minimum(z, 0.0) - jnp.log1p(jnp.exp(-jnp.abs(z)))


def _ada_kernel(c_ref, w_ref, b_ref, o_ref):
    s = _silu(c_ref[...]).astype(BF16)
    o_ref[...] = jnp.dot(s, w_ref[...].astype(BF16), preferred_element_type=F32) + b_ref[...]


def _ada(c_rows, w_ada, b_ada):
    d, n_out = w_ada.shape
    tn = min(1024, n_out)
    return pl.pallas_call(
        _ada_kernel,
        grid=(n_out // tn,),
        in_specs=[pl.BlockSpec((ADA_ROWS, d), lambda j: (0, 0)),
                  pl.BlockSpec((d, tn), lambda j: (0, j)),
                  pl.BlockSpec((1, tn), lambda j: (0, j))],
        out_specs=pl.BlockSpec((ADA_ROWS, tn), lambda j: (0, j)),
        out_shape=jax.ShapeDtypeStruct((ADA_ROWS, n_out), F32),
        compiler_params=_cparams(("arbitrary",)),
    )(c_rows, w_ada, b_ada.reshape(1, n_out))


def _proj_kernel(x_ref, sh_ref, sc_ref, wa_ref, wb_ref, wlr_ref, wg_ref, bg_ref, cos_ref, sin_ref,
                 of_ref, ob_ref, lg_ref, u_ref, *, rope, n_f32, n_q, n_rope):
    n = pl.program_id(1)

    @pl.when(n == 0)
    def _():
        u = (x_ref[...] * (1.0 + sc_ref[...]) + sh_ref[...]).astype(BF16)
        u_ref[...] = u
        lr = lax.dot_general(u, wlr_ref[...], _NT, preferred_element_type=F32)
        z = jnp.dot(lr.astype(BF16), wg_ref[...], preferred_element_type=F32) + bg_ref[...]
        lg_ref[...] = _log_sigmoid(z) * (1.0 / GLA_TAU)

    @pl.when(n < n_f32)
    def _():
        of_ref[...] = lax.dot_general(u_ref[...], wa_ref[...], _NT, preferred_element_type=F32)

    @pl.when(jnp.logical_and(n >= n_f32, n < n_f32 + n_rope))
    def _():
        acc = lax.dot_general(u_ref[...], wb_ref[...], _NT, preferred_element_type=F32)
        scale = jnp.where(n < n_f32 + n_q, DIFF_DH ** -0.5, 1.0).astype(F32)
        if rope:
            lane = lax.broadcasted_iota(I32, (acc.shape[0], LANES), 1)
            first = (lane % DIFF_DH) < (DIFF_DH // 2)
            cs = cos_ref[...]
            sn = sin_ref[...]
            for j in range(acc.shape[1] // LANES):
                a = acc[:, j * LANES:(j + 1) * LANES]
                partner = jnp.where(first, pltpu.roll(a, LANES - DIFF_DH // 2, 1), pltpu.roll(a, DIFF_DH // 2, 1))
                ob_ref[:, j * LANES:(j + 1) * LANES] = ((a * cs + partner * sn) * scale).astype(BF16)
        else:
            ob_ref[...] = (acc * scale).astype(BF16)

    @pl.when(n >= n_f32 + n_rope)
    def _():
        ob_ref[...] = lax.dot_general(u_ref[...], wb_ref[...], _NT, preferred_element_type=F32).astype(BF16)


def _proj(x2d, mod3, row_of_tile, w_in, w_diff, w_lr, w_g, b_g, cos_t, sin_t, *, rope, tm, n_tab_tiles):
    m, d = x2d.shape
    tn = 512
    n_f32 = w_diff.shape[0] // tn
    n_tiles = 2 * n_f32
    n_q = n_tiles // 6
    n_rope = n_tiles // 3
    r2 = w_lr.shape[0]
    ng = w_g.shape[1]
    kern = functools.partial(_proj_kernel, rope=rope, n_f32=n_f32, n_q=n_q, n_rope=n_rope)
    return pl.pallas_call(
        kern,
        grid=(m // tm, n_tiles),
        in_specs=[pl.BlockSpec((tm, d), lambda i, j: (i, 0)),
                  pl.BlockSpec((None, 1, d), lambda i, j: (row_of_tile(i), 0, 0)),
                  pl.BlockSpec((None, 1, d), lambda i, j: (row_of_tile(i), 0, 1)),
                  pl.BlockSpec((tn, d), lambda i, j: (jnp.minimum(j, n_f32 - 1), 0)),
                  pl.BlockSpec((tn, d), lambda i, j: (jnp.maximum(j - n_f32, 0), 0)),
                  pl.BlockSpec((r2, d), lambda i, j: (0, 0)),
                  pl.BlockSpec((r2, ng), lambda i, j: (0, 0)),
                  pl.BlockSpec((1, ng), lambda i, j: (0, 0)),
                  pl.BlockSpec((tm, LANES), lambda i, j: (i % n_tab_tiles, 0)),
                  pl.BlockSpec((tm, LANES), lambda i, j: (i % n_tab_tiles, 0))],
        out_specs=[pl.BlockSpec((tm, tn), lambda i, j: (i, jnp.minimum(j, n_f32 - 1))),
                   pl.BlockSpec((tm, tn), lambda i, j: (i, jnp.maximum(j - n_f32, 0))),
                   pl.BlockSpec((tm, ng), lambda i, j: (i, 0))],
        out_shape=[jax.ShapeDtypeStruct((m, n_f32 * tn), F32),
                   jax.ShapeDtypeStruct((m, (n_tiles - n_f32) * tn), BF16),
                   jax.ShapeDtypeStruct((m, ng), F32)],
        scratch_shapes=[pltpu.VMEM((tm, d), BF16)],
        compiler_params=_cparams(("arbitrary", "arbitrary")),
    )(x2d, mod3, mod3, w_in, w_diff, w_lr, w_g, b_g, cos_t, sin_t)


GLA_GROUP = 4
GLA_UNROLL = 2


def _split3(a):
    hi = a.astype(BF16)
    r1 = a - hi.astype(F32)
    mid = r1.astype(BF16)
    return hi, mid, (r1 - mid.astype(F32)).astype(BF16)


def _gla_groups(dirs, *, want_out, scale):
    c = GLA_CHUNK
    rows, dk = dirs[0]["k"].shape
    n_chunks = rows // c
    for d in dirs:
        d["parts"] = jnp.dot(d["sum_mat"], jnp.concatenate(_split3(d["g"]), axis=1),
                             preferred_element_type=F32)
    for d in dirs:
        parts = d["parts"]
        sums = (parts[:, :dk] + parts[:, dk:2 * dk]) + parts[:, 2 * dk:]
        bcum = sums[:rows]
        btot = sums[rows:]
        d["vb"] = d["v"].astype(BF16)
        d["k_out"] = (d["k"] * jnp.exp(btot - bcum)).astype(BF16)
        d["decay"] = jnp.exp(btot)
        if want_out:
            d["q_in"] = ((d["q"] * scale) * jnp.exp(bcum)).astype(BF16)
            d["k_in"] = (d["k"] * jnp.exp(-bcum)).astype(BF16)
    if want_out:
        for d in dirs:
            d["att"] = lax.dot_general(d["q_in"], d["k_in"], _NT, preferred_element_type=F32)
        for d in dirs:
            d["att"] = jnp.where(d["causal"], d["att"], 0.0).astype(BF16)
        for d in dirs:
            d["intra"] = jnp.dot(d["att"], d["vb"], preferred_element_type=F32)
    chunk = lambda ci: slice(ci * c, (ci + 1) * c)
    for d in dirs:
        d["upd"] = [lax.dot_general(d["vb"][chunk(ci)], d["k_out"][chunk(ci)], _TN, preferred_element_type=F32)
                    for ci in range(n_chunks)]
    states = {}
    for d in dirs:
        st = states.get(id(d["st_ref"]))
        if st is None:
            st = d["st_ref"][...]
        d["st_in"] = [None] * n_chunks
        for ci in (range(n_chunks) if d["forward"] else reversed(range(n_chunks))):
            d["st_in"][ci] = st
            st = st * d["decay"][ci * c:ci * c + 1] + d["upd"][ci]
        states[id(d["st_ref"])] = st
    for d in dirs:
        d["st_ref"][...] = states[id(d["st_ref"])]
    if not want_out:
        return [None for _ in dirs]
    outs = []
    for d in dirs:
        outs.append(jnp.concatenate(
            [d["intra"][chunk(ci)] + lax.dot_general(d["q_in"][chunk(ci)], d["st_in"][ci].astype(BF16), _NT,
                                                     preferred_element_type=F32)
             for ci in range(n_chunks)], axis=0))
    return outs


def _gla_kernel(q_ref, k_ref, v_ref, r_ref, gf_ref, gb_ref, kc_ref, vc_ref, gfc_ref, gbc_ref, ng_ref,
                o_ref, of_ref, obk_ref, sf_ref, sb_ref, *, n_lat, n_ctx, dk):
    c = GLA_CHUNK
    scale = dk ** -0.5
    sf_ref[...] = jnp.zeros_like(sf_ref)
    sb_ref[...] = jnp.zeros_like(sb_ref)

    def masks(rows):
        row = lax.broadcasted_iota(I32, (rows, rows), 0)
        col = lax.broadcasted_iota(I32, (rows, rows), 1)
        same = (row // c) == (col // c)
        lower = jnp.logical_and(same, row >= col)
        upper = jnp.logical_and(same, row <= col)
        total = same.astype(BF16)
        return (lower, jnp.concatenate([lower.astype(BF16), total], axis=0),
                upper, jnp.concatenate([upper.astype(BF16), total], axis=0))

    def scan(n, refs_f, refs_b, want_out):
        rows = min(GLA_GROUP * c, n)
        lower, sum_f, upper, sum_b = masks(rows)
        n_groups = n // rows
        unroll = GLA_UNROLL if n_groups % GLA_UNROLL == 0 else 1
        ld = lambda ref, r0: None if ref is None else ref[pl.ds(r0, rows), :]
        names = ("q", "k", "v", "g")

        def body(i, carry):
            items, stores = [], []
            for u in range(unroll):
                rf = pl.multiple_of((i * unroll + u) * rows, rows)
                rb = pl.multiple_of((n_groups - 1 - (i * unroll + u)) * rows, rows)
                items.append(dict(zip(names, (ld(r, rf) for r in refs_f)), st_ref=sf_ref, sum_mat=sum_f,
                                  causal=lower, forward=True))
                items.append(dict(zip(names, (ld(r, rb) for r in refs_b)), st_ref=sb_ref, sum_mat=sum_b,
                                  causal=upper, forward=False))
                stores += [(of_ref, rf), (obk_ref, rb)]
            outs = _gla_groups(items, want_out=want_out, scale=scale)
            if want_out:
                for (ref, r0), out in zip(stores, outs):
                    ref[pl.ds(r0, rows), :] = out
            return carry

        lax.fori_loop(0, n_groups // unroll, body, 0)

    scan(n_ctx, (None, kc_ref, vc_ref, gfc_ref), (None, kc_ref, vc_ref, gbc_ref), False)
    scan(n_lat, (q_ref, k_ref, v_ref, gf_ref), (q_ref, k_ref, v_ref, gb_ref), True)

    tb = min(512, n_lat)

    def fin_body(i, carry):
        r0 = pl.multiple_of(i * tb, tb)
        o = of_ref[pl.ds(r0, tb), :] + obk_ref[pl.ds(r0, tb), :]
        ms = jnp.mean(o * o, axis=-1, keepdims=True)
        o = o * lax.rsqrt(ms + RMS_EPS) * ng_ref[...]
        o_ref[pl.ds(r0, tb), :] = (o * _silu(r_ref[pl.ds(r0, tb), :])).astype(o_ref.dtype)
        return carry

    lax.fori_loop(0, n_lat // tb, fin_body, 0)


def _gla(pf_l, lg_l, pf_c, lg_c, norm_g, *, batch, n_lat, n_ctx):
    h = GLA_HEADS
    dk = pf_l.shape[1] // 6 // h
    dv = 2 * dk
    kern = functools.partial(_gla_kernel, n_lat=n_lat, n_ctx=n_ctx, dk=dk)
    lat = lambda width, off: pl.BlockSpec((n_lat, width), lambda b, i: (b, off + i))
    ctx = lambda width, off: pl.BlockSpec((n_ctx, width), lambda b, i: (b, off + i))
    return pl.pallas_call(
        kern,
        grid=(batch, h),
        in_specs=[lat(dk, 0), lat(dk, h), lat(dv, h), lat(dv, 2 * h),
                  lat(dk, 0), lat(dk, h),
                  ctx(dk, h), ctx(dv, h), ctx(dk, 0), ctx(dk, h),
                  pl.BlockSpec((1, dv), lambda b, i: (0, 0))],
        out_specs=pl.BlockSpec((n_lat, dv), lambda b, i: (b, i)),
        out_shape=jax.ShapeDtypeStruct((batch * n_lat, h * dv), BF16),
        scratch_shapes=[pltpu.VMEM((n_lat, dv), F32), pltpu.VMEM((n_lat, dv), F32),
                        pltpu.VMEM((dv, dk), F32), pltpu.VMEM((dv, dk), F32)],
        compiler_params=_cparams(("arbitrary", "arbitrary")),
    )(pf_l, pf_l, pf_l, pf_l, lg_l, lg_l, pf_c, pf_c, lg_c, lg_c, norm_g.reshape(1, dv))


DIFF_KEY_CHUNK = 512


def _diff_step(q_ref, kc_ref, kl_ref, vc_ref, vl_ref, lam_ref, ng_ref, o_ref, acc_ref, s_w, m_w, s_r, m_r):
    tq = o_ref.shape[0]
    dv = o_ref.shape[1]
    done = acc_ref[...]
    dl = lam_ref[...]
    lam = (jnp.exp(jnp.sum(dl[0:1] * dl[1:2], axis=-1, keepdims=True))
           - jnp.exp(jnp.sum(dl[2:3] * dl[3:4], axis=-1, keepdims=True)) + LAM_INIT)
    inv = 1.0 / done[:, dv:dv + 1]
    o = done[:tq, :dv] * inv[:tq] - done[tq:, :dv] * (inv[tq:] * lam)
    ms = jnp.mean(o * o, axis=-1, keepdims=True)
    o_ref[...] = (o * lax.rsqrt(ms + RMS_EPS) * ng_ref[...] * (1.0 - LAM_INIT)).astype(o_ref.dtype)

    q = q_ref[...]
    lane = lax.broadcasted_iota(I32, q.shape, 1)
    zero = jnp.zeros_like(q)
    qq = jnp.concatenate([jnp.where(lane < DIFF_DH, q, zero), jnp.where(lane >= DIFF_DH, q, zero)], axis=0)
    mx = jnp.max(m_r[...], axis=-1, keepdims=True)
    m_new = jnp.full((2 * tq, LANES), -jnp.inf, F32)
    acc = jnp.zeros((2 * tq, 2 * dv), F32)
    col = 0
    for k_ref, v_ref in ((kc_ref, vc_ref), (kl_ref, vl_ref)):
        n_keys = k_ref.shape[0]
        w = min(DIFF_KEY_CHUNK, n_keys)
        ones_col = (lax.broadcasted_iota(I32, (w, dv), 1) == 0).astype(BF16)
        for r0 in range(0, n_keys, w):
            s_new = lax.dot_general(qq, k_ref[r0:r0 + w, :], _NT, preferred_element_type=F32)
            s_w[:, col:col + w] = s_new
            for t in range(w // LANES):
                m_new = jnp.maximum(m_new, s_new[:, t * LANES:(t + 1) * LANES])
            e = jnp.exp(s_r[:, col:col + w] - mx).astype(BF16)
            v_ext = jnp.concatenate([v_ref[r0:r0 + w, :], ones_col], axis=1)
            acc = acc + jnp.dot(e, v_ext, preferred_element_type=F32)
            col += w
    m_w[...] = m_new
    acc_ref[...] = acc


def _diff_kernel(q_ref, kl_ref, kc_ref, vl_ref, vc_ref, lam_ref, ng_ref, o_ref, acc_ref,
                 sa_ref, sb_ref, ma_ref, mb_ref):
    j = pl.program_id(1)

    @pl.when(j == 0)
    def _():
        sb_ref[...] = jnp.zeros_like(sb_ref)
        mb_ref[...] = jnp.zeros_like(mb_ref)
        acc_ref[...] = jnp.ones_like(acc_ref)

    args = (q_ref, kc_ref, kl_ref, vc_ref, vl_ref, lam_ref, ng_ref, o_ref, acc_ref)

    @pl.when(j % 2 == 0)
    def _():
        _diff_step(*args, sa_ref, ma_ref, sb_ref, mb_ref)

    @pl.when(j % 2 == 1)
    def _():
        _diff_step(*args, sb_ref, mb_ref, sa_ref, ma_ref)


def _diff(pb_l, pb_c, diff_lambda, norm_g, *, batch, n_lat, n_ctx):
    h = DIFF_HEADS
    dv = 2 * DIFF_DH
    tq = min(512, n_lat)
    nq = n_lat // tq
    nblk = h * nq
    cur = lambda j: jnp.minimum(j, nblk - 1)
    prev = lambda j: jnp.clip(j - 1, 0, nblk - 1)
    fin = lambda j: jnp.maximum(j - 2, 0)
    return pl.pallas_call(
        _diff_kernel,
        grid=(batch, nblk + 2),
        in_specs=[pl.BlockSpec((tq, dv), lambda b, j: (b * nq + cur(j) % nq, cur(j) // nq)),
                  pl.BlockSpec((n_lat, dv), lambda b, j: (b, h + cur(j) // nq)),
                  pl.BlockSpec((n_ctx, dv), lambda b, j: (b, h + cur(j) // nq)),
                  pl.BlockSpec((n_lat, dv), lambda b, j: (b, 2 * h + prev(j) // nq)),
                  pl.BlockSpec((n_ctx, dv), lambda b, j: (b, 2 * h + prev(j) // nq)),
                  pl.BlockSpec(diff_lambda.shape, lambda b, j: (0, 0)),
                  pl.BlockSpec((1, dv), lambda b, j: (0, 0))],
        out_specs=pl.BlockSpec((tq, dv), lambda b, j: (b * nq + fin(j) % nq, fin(j) // nq)),
        out_shape=jax.ShapeDtypeStruct((batch * n_lat, h * dv), BF16),
        scratch_shapes=[pltpu.VMEM((2 * tq, 2 * dv), F32),
                        pltpu.VMEM((2 * tq, n_ctx + n_lat), F32), pltpu.VMEM((2 * tq, n_ctx + n_lat), F32),
                        pltpu.VMEM((2 * tq, LANES), F32), pltpu.VMEM((2 * tq, LANES), F32)],
        compiler_params=_cparams(("arbitrary", "arbitrary")),
    )(pb_l, pb_l, pb_c, pb_l, pb_c, diff_lambda, norm_g.reshape(1, dv))


def _layer_norm(y, g, b):
    mu = jnp.mean(y, axis=-1, keepdims=True)
    yc = y - mu
    var = jnp.mean(yc * yc, axis=-1, keepdims=True)
    return yc * lax.rsqrt(var + LN_EPS) * g + b


def _split2(a):
    hi = a.astype(BF16)
    return hi, (a - hi.astype(F32)).astype(BF16)


def _out_kernel(ga_ref, df_ref, x_ref, wo_ref, g1_ref, sh2_ref, sc2_ref, lg_ref, lb_ref, wr_ref,
                x1_ref, h_ref, aff_ref):
    half = ga_ref.shape[1]
    o = (jnp.dot(ga_ref[...], wo_ref[:half, :], preferred_element_type=F32)
         + jnp.dot(df_ref[...], wo_ref[half:, :], preferred_element_type=F32))
    x1 = _layer_norm(ALPHA * x_ref[...] + g1_ref[...] * o, lg_ref[...], lb_ref[...])
    x1_ref[...] = ALPHA * x1
    hmod = x1 * (1.0 + sc2_ref[...]) + sh2_ref[...]
    hbits = pltpu.bitcast(hmod.astype(BF16).astype(F32), I32)
    hw = hbits.shape[1] // 2
    packed = (hbits[:, hw:] & jnp.int32(-65536)) | lax.shift_right_logical(hbits[:, :hw], 16)
    tm = packed.shape[0]
    wt = hw // LANES
    for s in range(wt):
        h_ref[pl.ds(s, tm, stride=wt), :] = packed[:, s * LANES:(s + 1) * LANES]
    h_hi, h_lo = _split2(hmod)
    w_hi, w_lo = _split2(wr_ref[...])
    logits = (lax.dot_general(w_hi, h_hi, _NT, preferred_element_type=F32)
              + lax.dot_general(w_lo, h_hi, _NT, preferred_element_type=F32)
              + lax.dot_general(w_hi, h_lo, _NT, preferred_element_type=F32))
    e = jnp.exp(logits - jnp.max(logits, axis=0, keepdims=True))
    aff_ref[...] = e / jnp.sum(e, axis=0, keepdims=True)


def _out(gla_o, diff_o, x2d, w_o, mod3, ln_g, ln_b, w_router_t, *, batch, n_lat):
    m, d = x2d.shape
    half = gla_o.shape[1]
    n_e = w_router_t.shape[0]
    tm = min(512, n_lat)
    tpb = n_lat // tm
    modspec = lambda k: pl.BlockSpec((None, 1, d), lambda i: (i // tpb, 0, k))
    vec = pl.BlockSpec((1, d), lambda i: (0, 0))
    return pl.pallas_call(
        _out_kernel,
        grid=(m // tm,),
        in_specs=[pl.BlockSpec((tm, half), lambda i: (i, 0)),
                  pl.BlockSpec((tm, half), lambda i: (i, 0)),
                  pl.BlockSpec((tm, d), lambda i: (i, 0)),
                  pl.BlockSpec((2 * half, d), lambda i: (0, 0), pipeline_mode=pl.Buffered(1)),
                  modspec(2), modspec(3), modspec(4), vec, vec,
                  pl.BlockSpec((n_e, d), lambda i: (0, 0))],
        out_specs=[pl.BlockSpec((tm, d), lambda i: (i, 0)),
                   pl.BlockSpec((tm * (d // 2 // LANES), LANES), lambda i: (i, 0)),
                   pl.BlockSpec((None, n_e, tm), lambda i: (i // tpb, 0, i % tpb))],
        out_shape=[jax.ShapeDtypeStruct((m, d), F32),
                   jax.ShapeDtypeStruct((m * (d // 2 // LANES), LANES), I32),
                   jax.ShapeDtypeStruct((batch, n_e, n_lat), F32)],
        compiler_params=_cparams(("arbitrary",)),
    )(gla_o, diff_o, x2d, w_o, mod3, mod3, mod3, ln_g.reshape(1, d), ln_b.reshape(1, d), w_router_t)


def _prefix_count(mask_f32, strict_upper):
    rows, n = mask_f32.shape
    carry = jnp.zeros((rows, 1), F32)
    pieces = []
    for j in range(n // LANES):
        blk = mask_f32[:, j * LANES:(j + 1) * LANES]
        within = jnp.dot(blk.astype(BF16), strict_upper, preferred_element_type=F32)
        pieces.append(within + carry)
        carry = carry + jnp.sum(blk, axis=-1, keepdims=True)
    return jnp.concatenate(pieces, axis=-1)


def _route_kernel(aff_ref, idx_ref, gate_ref, pos_ref, *, cap):
    b = pl.program_id(0)
    v = aff_ref[...]
    n_e, n = v.shape
    bits = pltpu.bitcast(v, I32)

    def search(i, t):
        cand = t | (1 << (30 - i))
        cnt = jnp.sum((bits >= cand).astype(I32), axis=-1, keepdims=True)
        return jnp.where(cnt >= cap, cand, t)

    thr = lax.fori_loop(0, 31, search, jnp.zeros((n_e, 1), I32))
    gt = bits > thr
    eq = bits == thr
    need = (cap - jnp.sum(gt.astype(I32), axis=-1, keepdims=True)).astype(F32)
    r = lax.broadcasted_iota(I32, (LANES, LANES), 0)
    cidx = lax.broadcasted_iota(I32, (LANES, LANES), 1)
    strict_upper = (r < cidx).astype(BF16)
    eq_rank = _prefix_count(eq.astype(F32), strict_upper)
    sel = jnp.logical_or(gt, jnp.logical_and(eq, eq_rank < need))
    pos = _prefix_count(sel.astype(F32), strict_upper)
    pos_ref[...] = jnp.where(sel, pos, -1.0).astype(I32)

    tok = lax.broadcasted_iota(I32, (8, n), 1)
    srow = lax.broadcasted_iota(I32, (8, n), 0)
    tok_hi = (tok >> 6).astype(F32)
    tok_lo = (tok & 63).astype(F32)
    slot = lax.broadcasted_iota(I32, (cap, n), 0)

    def compact(e, carry):
        ve = aff_ref[pl.ds(e, 1), :]
        v_hi = ve.astype(BF16).astype(F32)
        r1 = ve - v_hi
        v_mid = r1.astype(BF16).astype(F32)
        v_lo = r1 - v_mid
        lhs = jnp.where(srow == 0, tok_hi,
              jnp.where(srow == 1, tok_lo,
              jnp.where(srow == 2, v_hi,
              jnp.where(srow == 3, v_mid,
              jnp.where(srow == 4, v_lo, 0.0))))).astype(BF16)
        onehot = (slot == pos_ref[pl.ds(e, 1), :]).astype(F32).astype(BF16)
        res = lax.dot_general(lhs, onehot, _NT, preferred_element_type=F32)
        idx_ref[pl.ds(e, 1), :] = (res[0:1] * 64.0 + res[1:2]).astype(I32) + b * n
        gate_ref[pl.ds(e, 1), :] = (res[2:3] + res[3:4]) + res[4:5]
        return carry

    lax.fori_loop(0, n_e, compact, 0)


def _route(aff_t, *, cap):
    batch, n_e, n = aff_t.shape
    kern = functools.partial(_route_kernel, cap=cap)
    return pl.pallas_call(
        kern,
        grid=(batch,),
        in_specs=[pl.BlockSpec((None, n_e, n), lambda b: (b, 0, 0))],
        out_specs=[pl.BlockSpec((None, n_e, cap), lambda b: (b, 0, 0)),
                   pl.BlockSpec((None, n_e, cap), lambda b: (b, 0, 0))],
        out_shape=[jax.ShapeDtypeStruct((batch, n_e, cap), I32),
                   jax.ShapeDtypeStruct((batch, n_e, cap), F32)],
        scratch_shapes=[pltpu.VMEM((n_e, n), I32)],
        compiler_params=_cparams(("arbitrary",)),
    )(aff_t)


DMA_UNROLL = 8


def _moe_kernel(idx_ref, gate_ref, h_hbm, w1_ref, w3_ref, w2_ref, g2_ref, facc_in, facc_hbm,
                stage_ref, acc_ref, gsem, rsem, wsem, *, rows, cap):
    del facc_in
    e = pl.program_id(0)
    f = pl.program_id(1)
    n_e = pl.num_programs(0)
    nf = pl.num_programs(1)
    slot = e % 2

    def issue_rows(make):
        def body(i, carry):
            for u in range(DMA_UNROLL):
                make(i, u).start()
            return carry
        lax.fori_loop(0, rows // DMA_UNROLL, body, 0)

    wt = stage_ref.shape[1] // rows
    assert acc_ref.shape[1] == DMA_UNROLL

    def x_row(ex, s):
        def make(i, u):
            k = i * DMA_UNROLL + u
            src = pl.multiple_of(idx_ref[ex * rows + k] * wt, wt)
            dst = pl.multiple_of(k * wt, wt)
            return pltpu.make_async_copy(h_hbm.at[pl.ds(src, wt), :], stage_ref.at[s, pl.ds(dst, wt), :], gsem.at[s])
        return make

    def acc_row_in(i, u):
        tok = idx_ref[e * rows + i * DMA_UNROLL + u]
        return pltpu.make_async_copy(facc_hbm.at[pl.ds(tok, 1), :], acc_ref.at[i, pl.ds(u, 1), :], rsem)

    def acc_row_out(i, u):
        tok = idx_ref[e * rows + i * DMA_UNROLL + u]
        return pltpu.make_async_copy(acc_ref.at[i, pl.ds(u, 1), :], facc_hbm.at[pl.ds(tok, 1), :], wsem)

    def wait_x(s):
        pltpu.make_async_copy(h_hbm.at[pl.ds(0, rows * wt), :], stage_ref.at[s], gsem.at[s]).wait()

    def wait_acc_in():
        pltpu.make_async_copy(acc_ref, acc_ref, rsem).wait()

    def wait_acc_out():
        pltpu.make_async_copy(acc_ref, acc_ref, wsem).wait()

    @pl.when(f == 0)
    def _():
        @pl.when(e > 0)
        def _():
            wait_acc_out()
        issue_rows(acc_row_in)

        @pl.when(e == 0)
        def _():
            issue_rows(x_row(0, 0))
        wait_x(slot)

        @pl.when(e + 1 < n_e)
        def _():
            issue_rows(x_row(e + 1, 1 - slot))
        wait_acc_in()

    words = [stage_ref[slot, pl.ds(s, rows, stride=wt), :] for s in range(wt)]
    half = wt * LANES
    x_lo = jnp.concatenate([pltpu.bitcast(u << 16, F32).astype(BF16) for u in words], axis=1)
    x_hi = jnp.concatenate([pltpu.bitcast(u & jnp.int32(-65536), F32).astype(BF16) for u in words],
                           axis=1)
    tf = w1_ref.shape[1]
    w13 = jnp.concatenate([w1_ref[...], w3_ref[...]], axis=1).astype(BF16)
    ag = (jnp.dot(x_lo, w13[:half], preferred_element_type=F32)
          + jnp.dot(x_hi, w13[half:], preferred_element_type=F32))
    hid = (_silu(ag[:, :tf]) * ag[:, tf:]).astype(BF16)
    gate = gate_ref[...]
    sub = acc_ref.shape[1]
    d = acc_ref.shape[2]
    tc = min(512, d)
    for j in range(d // tc):
        cols = slice(j * tc, (j + 1) * tc)
        y = gate * jnp.dot(hid, w2_ref[:, cols].astype(BF16), preferred_element_type=F32)
        for b in range(rows // cap):
            yb = y[b * cap:(b + 1) * cap] * g2_ref[b][:, cols]
            acc_ref[b * cap // sub:(b + 1) * cap // sub, :, cols] += yb.reshape(cap // sub, sub, tc)

    @pl.when(f == nf - 1)
    def _():
        issue_rows(acc_row_out)

        @pl.when(e == n_e - 1)
        def _():
            wait_acc_out()


def _moe(idx_flat, gate_col, h_packed, w1, w3, w2, mod3, facc0, *, rows, cap):
    m, d = facc0.shape
    n_e, _, dff = w1.shape
    tf = min(256, dff)
    kern = functools.partial(_moe_kernel, rows=rows, cap=cap)
    grid_spec = pltpu.PrefetchScalarGridSpec(
        num_scalar_prefetch=1,
        grid=(n_e, dff // tf),
        in_specs=[pl.BlockSpec((None, rows, 1), lambda e, f, idx: (e, 0, 0)),
                  pl.BlockSpec(memory_space=pl.ANY),
                  pl.BlockSpec((None, d, tf), lambda e, f, idx: (e, 0, f)),
                  pl.BlockSpec((None, d, tf), lambda e, f, idx: (e, 0, f)),
                  pl.BlockSpec((None, tf, d), lambda e, f, idx: (e, f, 0)),
                  pl.BlockSpec((ADA_ROWS, 1, d), lambda e, f, idx: (0, 0, 5)),
                  pl.BlockSpec(memory_space=pl.ANY)],
        out_specs=pl.BlockSpec(memory_space=pl.ANY),
        scratch_shapes=[pltpu.VMEM((2, rows * (d // 2 // LANES), LANES), I32),
                        pltpu.VMEM((rows // DMA_UNROLL, DMA_UNROLL, d), F32),
                        pltpu.SemaphoreType.DMA((2,)), pltpu.SemaphoreType.DMA(()), pltpu.SemaphoreType.DMA(())],
    )
    return pl.pallas_call(
        kern,
        grid_spec=grid_spec,
        out_shape=jax.ShapeDtypeStruct((m, d), F32),
        input_output_aliases={7: 0},
        compiler_params=_cparams(("arbitrary", "arbitrary")),
    )(idx_flat, gate_col, h_packed, w1, w3, w2, mod3, facc0)


def _final_kernel(y_ref, lg_ref, lb_ref, o_ref):
    o_ref[...] = _layer_norm(y_ref[...], lg_ref[...], lb_ref[...])


def _final(pre_ln, ln_g, ln_b, *, n_lat):
    m, d = pre_ln.shape
    tm = min(512, n_lat)
    vec = pl.BlockSpec((1, d), lambda i: (0, 0))
    return pl.pallas_call(
        _final_kernel,
        grid=(m // tm,),
        in_specs=[pl.BlockSpec((tm, d), lambda i: (i, 0)), vec, vec],
        out_specs=pl.BlockSpec((tm, d), lambda i: (i, 0)),
        out_shape=jax.ShapeDtypeStruct((m, d), F32),
        compiler_params=_cparams(("arbitrary",)),
    )(pre_ln, ln_g.reshape(1, d), ln_b.reshape(1, d))


def _rope_tables(n_lat):
    rows = n_lat // GRID_W
    row = jnp.repeat(jnp.arange(rows, dtype=F32), GRID_W)
    col = jnp.tile(jnp.arange(GRID_W, dtype=F32), rows)
    n_freq = DIFF_DH // 4
    inv = ROPE_BASE ** (-jnp.arange(n_freq, dtype=F32) / n_freq)
    ang = jnp.concatenate([row[:, None] * inv, col[:, None] * inv], axis=-1)
    cos, sin = jnp.cos(ang), jnp.sin(ang)
    reps = LANES // DIFF_DH
    cos_t = jnp.tile(jnp.concatenate([cos, cos], axis=-1), (1, reps))
    sin_t = jnp.tile(jnp.concatenate([-sin, sin], axis=-1), (1, reps))
    return cos_t, sin_t


def kernel(x, c, ctx, c_ctx, w_ada, b_ada, w_in, w_gate2, b_gate, gla_norm_g, diff_lambda, diff_norm_g, w_o,
           ln1_g, ln1_b, w_router, w1, w3, w2, ln2_g, ln2_b):
    batch, n_lat, d = x.shape
    n_ctx = ctx.shape[1]
    assert w_ada.shape[0] == DEPTH == 1 and batch < ADA_ROWS
    n_e = w_router.shape[-1]
    cap = EC_CAPACITY_FACTOR * n_lat // n_e
    l = 0

    c_rows = jnp.concatenate([c, c_ctx[None, :], jnp.zeros((ADA_ROWS - batch - 1, d), F32)], axis=0)
    mod3 = _ada(c_rows, w_ada[l], b_ada[l]).reshape(ADA_ROWS, 1, 6 * d)

    gqkvr = 2 * (d // 4) + 2 * (d // 2)
    r2 = 2 * GLA_RANK
    w_in_t = w_in[l].T.astype(BF16)
    w_diff = w_in_t[gqkvr + r2:]
    w_lr = w_in_t[gqkvr:gqkvr + r2]
    hk = w_gate2.shape[-1]
    zeros = jnp.zeros((GLA_RANK, hk), F32)
    w_g = jnp.concatenate([jnp.concatenate([w_gate2[l, 0], zeros], axis=1),
                           jnp.concatenate([zeros, w_gate2[l, 1]], axis=1)], axis=0).astype(BF16)
    b_g = b_gate[l].reshape(1, 2 * hk)
    cos_t, sin_t = _rope_tables(n_lat)

    tm_l = min(1024, n_lat)
    tpb = n_lat // tm_l
    x2d = x.reshape(batch * n_lat, d)
    pf_l, pb_l, lg_l = _proj(x2d, mod3, lambda i: i // tpb, w_in_t, w_diff, w_lr, w_g, b_g, cos_t, sin_t,
                             rope=True, tm=tm_l, n_tab_tiles=tpb)
    ctx2d = ctx.reshape(batch * n_ctx, d)
    tm_c = min(1024, batch * n_ctx)
    pf_c, pb_c, lg_c = _proj(ctx2d, mod3, lambda i: batch, w_in_t, w_diff, w_lr, w_g, b_g, cos_t, sin_t,
                             rope=False, tm=tm_c, n_tab_tiles=1)

    gla_o = _gla(pf_l, lg_l, pf_c, lg_c, gla_norm_g[l], batch=batch, n_lat=n_lat, n_ctx=n_ctx)
    diff_o = _diff(pb_l, pb_c, diff_lambda[l], diff_norm_g[l], batch=batch, n_lat=n_lat, n_ctx=n_ctx)

    res, h_packed, aff_t = _out(gla_o, diff_o, x2d, w_o[l].astype(BF16), mod3, ln1_g[l], ln1_b[l],
                                w_router[l].T, batch=batch, n_lat=n_lat)

    idx, gate = _route(aff_t, cap=cap)
    rows = batch * cap
    idx_flat = idx.transpose(1, 0, 2).reshape(n_e * rows)
    gate_col = gate.transpose(1, 0, 2).reshape(n_e, rows, 1)
    pre_ln = _moe(idx_flat, gate_col, h_packed, w1[l], w3[l], w2[l], mod3, res, rows=rows, cap=cap)

    out = _final(pre_ln, ln2_g[l], ln2_b[l], n_lat=n_lat)
    return out.reshape(batch, n_lat, d)
```

```python
import functools
import math

import jax
import jax.numpy as jnp
import numpy as np
from jax import lax
from jax.experimental import pallas as pl
from jax.experimental.pallas import tpu as pltpu

F32 = jnp.float32
BF16 = jnp.bfloat16
I32 = jnp.int32

GRID_W = 64
GLA_HEADS = 4
GLA_RANK = 16
GLA_TAU = 16.0
GLA_CHUNK = 64
DIFF_HEADS = 8
DIFF_DH = 64
ROPE_BASE = 10000.0
EC_CAPACITY_FACTOR = 2
LN_EPS = 1e-5
RMS_EPS = 1e-6
DEPTH = 1
ALPHA = (2.0 * DEPTH) ** 0.25
LAM_INIT = 0.8 - 0.6 * math.exp(-0.3 * 0)

LANES = 128
ADA_ROWS = 8
VMEM_LIMIT = 60 * 1024 * 1024

_NT = (((1,), (1,)), ((), ()))
_TN = (((0,), (0,)), ((), ()))


def _cparams(sem):
    return pltpu.CompilerParams(dimension_semantics=sem, vmem_limit_bytes=VMEM_LIMIT)


def _silu(a):
    return a * (1.0 / (1.0 + jnp.exp(-a)))


def _log_sigmoid(z):
    return jnp.minimum(z, 0.0) - jnp.log1p(jnp.exp(-jnp.abs(z)))


def _ada_kernel(c_ref, w_ref, b_ref, o_ref):
    s = _silu(c_ref[...]).astype(BF16)
    o_ref[...] = jnp.dot(s, w_ref[...].astype(BF16), preferred_element_type=F32) + b_ref[...]


def _ada(c_rows, w_ada, b_ada):
    d, n_out = w_ada.shape
    tn = min(1024, n_out)
    return pl.pallas_call(
        _ada_kernel,
        grid=(n_out // tn,),
        in_specs=[pl.BlockSpec((ADA_ROWS, d), lambda j: (0, 0)),
                  pl.BlockSpec((d, tn), lambda j: (0, j)),
                  pl.BlockSpec((1, tn), lambda j: (0, j))],
        out_specs=pl.BlockSpec((ADA_ROWS, tn), lambda j: (0, j)),
        out_shape=jax.ShapeDtypeStruct((ADA_ROWS, n_out), F32),
        compiler_params=_cparams(("arbitrary",)),
    )(c_rows, w_ada, b_ada.reshape(1, n_out))


def _proj_kernel(x_ref, sh_ref, sc_ref, wa_ref, wb_ref, wlr_ref, wg_ref, bg_ref, cos_ref, sin_ref,
                 of_ref, ob_ref, lg_ref, u_ref, *, rope, n_f32, n_q, n_rope):
    n = pl.program_id(1)

    @pl.when(n == 0)
    def _():
        u = (x_ref[...] * (1.0 + sc_ref[...]) + sh_ref[...]).astype(BF16)
        u_ref[...] = u
        lr = lax.dot_general(u, wlr_ref[...], _NT, preferred_element_type=F32)
        z = jnp.dot(lr.astype(BF16), wg_ref[...], preferred_element_type=F32) + bg_ref[...]
        lg_ref[...] = _log_sigmoid(z) * (1.0 / GLA_TAU)

    @pl.when(n < n_f32)
    def _():
        of_ref[...] = lax.dot_general(u_ref[...], wa_ref[...], _NT, preferred_element_type=F32)

    @pl.when(jnp.logical_and(n >= n_f32, n < n_f32 + n_rope))
    def _():
        acc = lax.dot_general(u_ref[...], wb_ref[...], _NT, preferred_element_type=F32)
        scale = jnp.where(n < n_f32 + n_q, DIFF_DH ** -0.5, 1.0).astype(F32)
        if rope:
            lane = lax.broadcasted_iota(I32, (acc.shape[0], LANES), 1)
            first = (lane % DIFF_DH) < (DIFF_DH // 2)
            cs = cos_ref[...]
            sn = sin_ref[...]
            for j in range(acc.shape[1] // LANES):
                a = acc[:, j * LANES:(j + 1) * LANES]
                partner = jnp.where(first, pltpu.roll(a, LANES - DIFF_DH // 2, 1), pltpu.roll(a, DIFF_DH // 2, 1))
                ob_ref[:, j * LANES:(j + 1) * LANES] = ((a * cs + partner * sn) * scale).astype(BF16)
        else:
            ob_ref[...] = (acc * scale).astype(BF16)

    @pl.when(n >= n_f32 + n_rope)
    def _():
        ob_ref[...] = lax.dot_general(u_ref[...], wb_ref[...], _NT, preferred_element_type=F32).astype(BF16)


def _proj(x2d, mod3, row_of_tile, w_in, w_diff, w_lr, w_g, b_g, cos_t, sin_t, *, rope, tm, n_tab_tiles):
    m, d = x2d.shape
    tn = 512
    n_f32 = w_diff.shape[0] // tn
    n_tiles = 2 * n_f32
    n_q = n_tiles // 6
    n_rope = n_tiles // 3
    r2 = w_lr.shape[0]
    ng = w_g.shape[1]
    kern = functools.partial(_proj_kernel, rope=rope, n_f32=n_f32, n_q=n_q, n_rope=n_rope)
    return pl.pallas_call(
        kern,
        grid=(m // tm, n_tiles),
        in_specs=[pl.BlockSpec((tm, d), lambda i, j: (i, 0)),
                  pl.BlockSpec((None, 1, d), lambda i, j: (row_of_tile(i), 0, 0)),
                  pl.BlockSpec((None, 1, d), lambda i, j: (row_of_tile(i), 0, 1)),
                  pl.BlockSpec((tn, d), lambda i, j: (jnp.minimum(j, n_f32 - 1), 0)),
                  pl.BlockSpec((tn, d), lambda i, j: (jnp.maximum(j - n_f32, 0), 0)),
                  pl.BlockSpec((r2, d), lambda i, j: (0, 0)),
                  pl.BlockSpec((r2, ng), lambda i, j: (0, 0)),
                  pl.BlockSpec((1, ng), lambda i, j: (0, 0)),
                  pl.BlockSpec((tm, LANES), lambda i, j: (i % n_tab_tiles, 0)),
                  pl.BlockSpec((tm, LANES), lambda i, j: (i % n_tab_tiles, 0))],
        out_specs=[pl.BlockSpec((tm, tn), lambda i, j: (i, jnp.minimum(j, n_f32 - 1))),
                   pl.BlockSpec((tm, tn), lambda i, j: (i, jnp.maximum(j - n_f32, 0))),
                   pl.BlockSpec((tm, ng), lambda i, j: (i, 0))],
        out_shape=[jax.ShapeDtypeStruct((m, n_f32 * tn), F32),
                   jax.ShapeDtypeStruct((m, (n_tiles - n_f32) * tn), BF16),
                   jax.ShapeDtypeStruct((m, ng), F32)],
        scratch_shapes=[pltpu.VMEM((tm, d), BF16)],
        compiler_params=_cparams(("arbitrary", "arbitrary")),
    )(x2d, mod3, mod3, w_in, w_diff, w_lr, w_g, b_g, cos_t, sin_t)


GLA_GROUP = 4
GLA_UNROLL = 2


def _split3(a):
    hi = a.astype(BF16)
    r1 = a - hi.astype(F32)
    mid = r1.astype(BF16)
    return hi, mid, (r1 - mid.astype(F32)).astype(BF16)


def _gla_groups(dirs, *, want_out, scale):
    c = GLA_CHUNK
    rows, dk = dirs[0]["k"].shape
    n_chunks = rows // c
    for d in dirs:
        d["parts"] = jnp.dot(d["sum_mat"], jnp.concatenate(_split3(d["g"]), axis=1),
                             preferred_element_type=F32)
    for d in dirs:
        parts = d["parts"]
        sums = (parts[:, :dk] + parts[:, dk:2 * dk]) + parts[:, 2 * dk:]
        bcum = sums[:rows]
        btot = sums[rows:]
        d["vb"] = d["v"].astype(BF16)
        d["k_out"] = (d["k"] * jnp.exp(btot - bcum)).astype(BF16)
        d["decay"] = jnp.exp(btot)
        if want_out:
            d["q_in"] = ((d["q"] * scale) * jnp.exp(bcum)).astype(BF16)
            d["k_in"] = (d["k"] * jnp.exp(-bcum)).astype(BF16)
    if want_out:
        for d in dirs:
            d["att"] = lax.dot_general(d["q_in"], d["k_in"], _NT, preferred_element_type=F32)
        for d in dirs:
            d["att"] = jnp.where(d["causal"], d["att"], 0.0).astype(BF16)
        for d in dirs:
            d["intra"] = jnp.dot(d["att"], d["vb"], preferred_element_type=F32)
    chunk = lambda ci: slice(ci * c, (ci + 1) * c)
    for d in dirs:
        d["upd"] = [lax.dot_general(d["vb"][chunk(ci)], d["k_out"][chunk(ci)], _TN, preferred_element_type=F32)
                    for ci in range(n_chunks)]
    states = {}
    for d in dirs:
        st = states.get(id(d["st_ref"]))
        if st is None:
            st = d["st_ref"][...]
        d["st_in"] = [None] * n_chunks
        for ci in (range(n_chunks) if d["forward"] else reversed(range(n_chunks))):
            d["st_in"][ci] = st
            st = st * d["decay"][ci * c:ci * c + 1] + d["upd"][ci]
        states[id(d["st_ref"])] = st
    for d in dirs:
        d["st_ref"][...] = states[id(d["st_ref"])]
    if not want_out:
        return [None for _ in dirs]
    outs = []
    for d in dirs:
        outs.append(jnp.concatenate(
            [d["intra"][chunk(ci)] + lax.dot_general(d["q_in"][chunk(ci)], d["st_in"][ci].astype(BF16), _NT,
                                                     preferred_element_type=F32)
             for ci in range(n_chunks)], axis=0))
    return outs


def _gla_kernel(q_ref, k_ref, v_ref, r_ref, gf_ref, gb_ref, kc_ref, vc_ref, gfc_ref, gbc_ref, ng_ref,
                o_ref, of_ref, obk_ref, sf_ref, sb_ref, *, n_lat, n_ctx, dk):
    c = GLA_CHUNK
    scale = dk ** -0.5
    sf_ref[...] = jnp.zeros_like(sf_ref)
    sb_ref[...] = jnp.zeros_like(sb_ref)

    def masks(rows):
        row = lax.broadcasted_iota(I32, (rows, rows), 0)
        col = lax.broadcasted_iota(I32, (rows, rows), 1)
        same = (row // c) == (col // c)
        lower = jnp.logical_and(same, row >= col)
        upper = jnp.logical_and(same, row <= col)
        total = same.astype(BF16)
        return (lower, jnp.concatenate([lower.astype(BF16), total], axis=0),
                upper, jnp.concatenate([upper.astype(BF16), total], axis=0))

    def scan(n, refs_f, refs_b, want_out):
        rows = min(GLA_GROUP * c, n)
        lower, sum_f, upper, sum_b = masks(rows)
        n_groups = n // rows
        unroll = GLA_UNROLL if n_groups % GLA_UNROLL == 0 else 1
        ld = lambda ref, r0: None if ref is None else ref[pl.ds(r0, rows), :]
        names = ("q", "k", "v", "g")

        def body(i, carry):
            items, stores = [], []
            for u in range(unroll):
                rf = pl.multiple_of((i * unroll + u) * rows, rows)
                rb = pl.multiple_of((n_groups - 1 - (i * unroll + u)) * rows, rows)
                items.append(dict(zip(names, (ld(r, rf) for r in refs_f)), st_ref=sf_ref, sum_mat=sum_f,
                                  causal=lower, forward=True))
                items.append(dict(zip(names, (ld(r, rb) for r in refs_b)), st_ref=sb_ref, sum_mat=sum_b,
                                  causal=upper, forward=False))
                stores += [(of_ref, rf), (obk_ref, rb)]
            outs = _gla_groups(items, want_out=want_out, scale=scale)
            if want_out:
                for (ref, r0), out in zip(stores, outs):
                    ref[pl.ds(r0, rows), :] = out
            return carry

        lax.fori_loop(0, n_groups // unroll, body, 0)

    scan(n_ctx, (None, kc_ref, vc_ref, gfc_ref), (None, kc_ref, vc_ref, gbc_ref), False)
    scan(n_lat, (q_ref, k_ref, v_ref, gf_ref), (q_ref, k_ref, v_ref, gb_ref), True)

    tb = min(512, n_lat)

    def fin_body(i, carry):
        r0 = pl.multiple_of(i * tb, tb)
        o = of_ref[pl.ds(r0, tb), :] + obk_ref[pl.ds(r0, tb), :]
        ms = jnp.mean(o * o, axis=-1, keepdims=True)
        o = o * lax.rsqrt(ms + RMS_EPS) * ng_ref[...]
        o_ref[pl.ds(r0, tb), :] = (o * _silu(r_ref[pl.ds(r0, tb), :])).astype(o_ref.dtype)
        return carry

    lax.fori_loop(0, n_lat // tb, fin_body, 0)


def _gla(pf_l, lg_l, pf_c, lg_c, norm_g, *, batch, n_lat, n_ctx):
    h = GLA_HEADS
    dk = pf_l.shape[1] // 6 // h
    dv = 2 * dk
    kern = functools.partial(_gla_kernel, n_lat=n_lat, n_ctx=n_ctx, dk=dk)
    lat = lambda width, off: pl.BlockSpec((n_lat, width), lambda b, i: (b, off + i))
    ctx = lambda width, off: pl.BlockSpec((n_ctx, width), lambda b, i: (b, off + i))
    return pl.pallas_call(
        kern,
        grid=(batch, h),
        in_specs=[lat(dk, 0), lat(dk, h), lat(dv, h), lat(dv, 2 * h),
                  lat(dk, 0), lat(dk, h),
                  ctx(dk, h), ctx(dv, h), ctx(dk, 0), ctx(dk, h),
                  pl.BlockSpec((1, dv), lambda b, i: (0, 0))],
        out_specs=pl.BlockSpec((n_lat, dv), lambda b, i: (b, i)),
        out_shape=jax.ShapeDtypeStruct((batch * n_lat, h * dv), BF16),
        scratch_shapes=[pltpu.VMEM((n_lat, dv), F32), pltpu.VMEM((n_lat, dv), F32),
                        pltpu.VMEM((dv, dk), F32), pltpu.VMEM((dv, dk), F32)],
        compiler_params=_cparams(("arbitrary", "arbitrary")),
    )(pf_l, pf_l, pf_l, pf_l, lg_l, lg_l, pf_c, pf_c, lg_c, lg_c, norm_g.reshape(1, dv))


DIFF_KEY_CHUNK = 512


def _diff_step(q_ref, kc_ref, kl_ref, vc_ref, vl_ref, lam_ref, ng_ref, o_ref, acc_ref, s_w, m_w, s_r, m_r):
    tq = o_ref.shape[0]
    dv = o_ref.shape[1]
    done = acc_ref[...]
    dl = lam_ref[...]
    lam = (jnp.exp(jnp.sum(dl[0:1] * dl[1:2], axis=-1, keepdims=True))
           - jnp.exp(jnp.sum(dl[2:3] * dl[3:4], axis=-1, keepdims=True)) + LAM_INIT)
    inv = 1.0 / done[:, dv:dv + 1]
    o = done[:tq, :dv] * inv[:tq] - done[tq:, :dv] * (inv[tq:] * lam)
    ms = jnp.mean(o * o, axis=-1, keepdims=True)
    o_ref[...] = (o * lax.rsqrt(ms + RMS_EPS) * ng_ref[...] * (1.0 - LAM_INIT)).astype(o_ref.dtype)

    q = q_ref[...]
    lane = lax.broadcasted_iota(I32, q.shape, 1)
    zero = jnp.zeros_like(q)
    qq = jnp.concatenate([jnp.where(lane < DIFF_DH, q, zero), jnp.where(lane >= DIFF_DH, q, zero)], axis=0)
    mx = jnp.max(m_r[...], axis=-1, keepdims=True)
    m_new = jnp.full((2 * tq, LANES), -jnp.inf, F32)
    acc = jnp.zeros((2 * tq, 2 * dv), F32)
    col = 0
    for k_ref, v_ref in ((kc_ref, vc_ref), (kl_ref, vl_ref)):
        n_keys = k_ref.shape[0]
        w = min(DIFF_KEY_CHUNK, n_keys)
        ones_col = (lax.broadcasted_iota(I32, (w, dv), 1) == 0).astype(BF16)
        for r0 in range(0, n_keys, w):
            s_new = lax.dot_general(qq, k_ref[r0:r0 + w, :], _NT, preferred_element_type=F32)
            s_w[:, col:col + w] = s_new
            for t in range(w // LANES):
                m_new = jnp.maximum(m_new, s_new[:, t * LANES:(t + 1) * LANES])
            e = jnp.exp(s_r[:, col:col + w] - mx).astype(BF16)
            v_ext = jnp.concatenate([v_ref[r0:r0 + w, :], ones_col], axis=1)
            acc = acc + jnp.dot(e, v_ext, preferred_element_type=F32)
            col += w
    m_w[...] = m_new
    acc_ref[...] = acc


def _diff_kernel(q_ref, kl_ref, kc_ref, vl_ref, vc_ref, lam_ref, ng_ref, o_ref, acc_ref,
                 sa_ref, sb_ref, ma_ref, mb_ref):
    j = pl.program_id(0)

    @pl.when(j == 0)
    def _():
        sb_ref[...] = jnp.zeros_like(sb_ref)
        mb_ref[...] = jnp.zeros_like(mb_ref)
        acc_ref[...] = jnp.ones_like(acc_ref)

    args = (q_ref, kc_ref, kl_ref, vc_ref, vl_ref, lam_ref, ng_ref, o_ref, acc_ref)

    @pl.when(j % 2 == 0)
    def _():
        _diff_step(*args, sa_ref, ma_ref, sb_ref, mb_ref)

    @pl.when(j % 2 == 1)
    def _():
        _diff_step(*args, sb_ref, mb_ref, sa_ref, ma_ref)


def _diff(pb_l, pb_c, diff_lambda, norm_g, *, batch, n_lat, n_ctx):
    h = DIFF_HEADS
    dv = 2 * DIFF_DH
    tq = min(512, n_lat)
    nq = n_lat // tq
    nblk = h * nq
    total = batch * nblk
    cur = lambda j: jnp.minimum(j, total - 1)
    prev = lambda j: jnp.clip(j - 1, 0, total - 1)
    fin = lambda j: jnp.maximum(j - 2, 0)
    sample = lambda g: g // nblk
    head = lambda g: (g % nblk) // nq
    qrow = lambda g: sample(g) * nq + g % nq
    return pl.pallas_call(
        _diff_kernel,
        grid=(total + 2,),
        in_specs=[pl.BlockSpec((tq, dv), lambda j: (qrow(cur(j)), head(cur(j)))),
                  pl.BlockSpec((n_lat, dv), lambda j: (sample(cur(j)), h + head(cur(j)))),
                  pl.BlockSpec((n_ctx, dv), lambda j: (sample(cur(j)), h + head(cur(j)))),
                  pl.BlockSpec((n_lat, dv), lambda j: (sample(prev(j)), 2 * h + head(prev(j)))),
                  pl.BlockSpec((n_ctx, dv), lambda j: (sample(prev(j)), 2 * h + head(prev(j)))),
                  pl.BlockSpec(diff_lambda.shape, lambda j: (0, 0)),
                  pl.BlockSpec((1, dv), lambda j: (0, 0))],
        out_specs=pl.BlockSpec((tq, dv), lambda j: (qrow(fin(j)), head(fin(j)))),
        out_shape=jax.ShapeDtypeStruct((batch * n_lat, h * dv), BF16),
        scratch_shapes=[pltpu.VMEM((2 * tq, 2 * dv), F32),
                        pltpu.VMEM((2 * tq, n_ctx + n_lat), F32), pltpu.VMEM((2 * tq, n_ctx + n_lat), F32),
                        pltpu.VMEM((2 * tq, LANES), F32), pltpu.VMEM((2 * tq, LANES), F32)],
        compiler_params=_cparams(("arbitrary",)),
    )(pb_l, pb_l, pb_c, pb_l, pb_c, diff_lambda, norm_g.reshape(1, dv))


def _layer_norm(y, g, b):
    mu = jnp.mean(y, axis=-1, keepdims=True)
    yc = y - mu
    var = jnp.mean(yc * yc, axis=-1, keepdims=True)
    return yc * lax.rsqrt(var + LN_EPS) * g + b


def _split2(a):
    hi = a.astype(BF16)
    return hi, (a - hi.astype(F32)).astype(BF16)


def _out_kernel(ga_ref, df_ref, x_ref, wo_ref, g1_ref, sh2_ref, sc2_ref, lg_ref, lb_ref, wr_ref,
                x1_ref, h_ref, aff_ref):
    half = ga_ref.shape[1]
    o = (jnp.dot(ga_ref[...], wo_ref[:half, :], preferred_element_type=F32)
         + jnp.dot(df_ref[...], wo_ref[half:, :], preferred_element_type=F32))
    x1 = _layer_norm(ALPHA * x_ref[...] + g1_ref[...] * o, lg_ref[...], lb_ref[...])
    x1_ref[...] = ALPHA * x1
    hmod = x1 * (1.0 + sc2_ref[...]) + sh2_ref[...]
    hbits = pltpu.bitcast(hmod.astype(BF16).astype(F32), I32)
    hw = hbits.shape[1] // 2
    packed = (hbits[:, hw:] & jnp.int32(-65536)) | lax.shift_right_logical(hbits[:, :hw], 16)
    tm = packed.shape[0]
    wt = hw // LANES
    for s in range(wt):
        h_ref[pl.ds(s, tm, stride=wt), :] = packed[:, s * LANES:(s + 1) * LANES]
    h_hi, h_lo = _split2(hmod)
    w_hi, w_lo = _split2(wr_ref[...])
    logits = (lax.dot_general(w_hi, h_hi, _NT, preferred_element_type=F32)
              + lax.dot_general(w_lo, h_hi, _NT, preferred_element_type=F32)
              + lax.dot_general(w_hi, h_lo, _NT, preferred_element_type=F32))
    e = jnp.exp(logits - jnp.max(logits, axis=0, keepdims=True))
    aff_ref[...] = e / jnp.sum(e, axis=0, keepdims=True)


def _out(gla_o, diff_o, x2d, w_o, mod3, ln_g, ln_b, w_router_t, *, batch, n_lat):
    m, d = x2d.shape
    half = gla_o.shape[1]
    n_e = w_router_t.shape[0]
    tm = min(512, n_lat)
    tpb = n_lat // tm
    modspec = lambda k: pl.BlockSpec((None, 1, d), lambda i: (i // tpb, 0, k))
    vec = pl.BlockSpec((1, d), lambda i: (0, 0))
    return pl.pallas_call(
        _out_kernel,
        grid=(m // tm,),
        in_specs=[pl.BlockSpec((tm, half), lambda i: (i, 0)),
                  pl.BlockSpec((tm, half), lambda i: (i, 0)),
                  pl.BlockSpec((tm, d), lambda i: (i, 0)),
                  pl.BlockSpec((2 * half, d), lambda i: (0, 0), pipeline_mode=pl.Buffered(1)),
                  modspec(2), modspec(3), modspec(4), vec, vec,
                  pl.BlockSpec((n_e, d), lambda i: (0, 0))],
        out_specs=[pl.BlockSpec((tm, d), lambda i: (i, 0)),
                   pl.BlockSpec((tm * (d // 2 // LANES), LANES), lambda i: (i, 0)),
                   pl.BlockSpec((None, n_e, tm), lambda i: (i // tpb, 0, i % tpb))],
        out_shape=[jax.ShapeDtypeStruct((m, d), F32),
                   jax.ShapeDtypeStruct((m * (d // 2 // LANES), LANES), I32),
                   jax.ShapeDtypeStruct((batch, n_e, n_lat), F32)],
        compiler_params=_cparams(("arbitrary",)),
    )(gla_o, diff_o, x2d, w_o, mod3, mod3, mod3, ln_g.reshape(1, d), ln_b.reshape(1, d), w_router_t)


def _prefix_count(mask_f32, strict_upper):
    rows, n = mask_f32.shape
    carry = jnp.zeros((rows, 1), F32)
    pieces = []
    for j in range(n // LANES):
        blk = mask_f32[:, j * LANES:(j + 1) * LANES]
        within = jnp.dot(blk.astype(BF16), strict_upper, preferred_element_type=F32)
        pieces.append(within + carry)
        carry = carry + jnp.sum(blk, axis=-1, keepdims=True)
    return jnp.concatenate(pieces, axis=-1)


def _route_kernel(aff_ref, idx_ref, gate_ref, pos_ref, *, cap):
    b = pl.program_id(0)
    v = aff_ref[...]
    n_e, n = v.shape
    bits = pltpu.bitcast(v, I32)

    def search(i, t):
        cand = t | (1 << (30 - i))
        cnt = jnp.sum((bits >= cand).astype(I32), axis=-1, keepdims=True)
        return jnp.where(cnt >= cap, cand, t)

    thr = lax.fori_loop(0, 31, search, jnp.zeros((n_e, 1), I32))
    gt = bits > thr
    eq = bits == thr
    need = (cap - jnp.sum(gt.astype(I32), axis=-1, keepdims=True)).astype(F32)
    r = lax.broadcasted_iota(I32, (LANES, LANES), 0)
    cidx = lax.broadcasted_iota(I32, (LANES, LANES), 1)
    strict_upper = (r < cidx).astype(BF16)
    eq_rank = _prefix_count(eq.astype(F32), strict_upper)
    sel = jnp.logical_or(gt, jnp.logical_and(eq, eq_rank < need))
    pos = _prefix_count(sel.astype(F32), strict_upper)
    pos_ref[...] = jnp.where(sel, pos, -1.0).astype(I32)

    tok = lax.broadcasted_iota(I32, (8, n), 1)
    srow = lax.broadcasted_iota(I32, (8, n), 0)
    tok_hi = (tok >> 6).astype(F32)
    tok_lo = (tok & 63).astype(F32)
    slot = lax.broadcasted_iota(I32, (cap, n), 0)

    def compact(e, carry):
        ve = aff_ref[pl.ds(e, 1), :]
        v_hi = ve.astype(BF16).astype(F32)
        r1 = ve - v_hi
        v_mid = r1.astype(BF16).astype(F32)
        v_lo = r1 - v_mid
        lhs = jnp.where(srow == 0, tok_hi,
              jnp.where(srow == 1, tok_lo,
              jnp.where(srow == 2, v_hi,
              jnp.where(srow == 3, v_mid,
              jnp.where(srow == 4, v_lo, 0.0))))).astype(BF16)
        onehot = (slot == pos_ref[pl.ds(e, 1), :]).astype(F32).astype(BF16)
        res = lax.dot_general(lhs, onehot, _NT, preferred_element_type=F32)
        idx_ref[pl.ds(e, 1), :] = (res[0:1] * 64.0 + res[1:2]).astype(I32) + b * n
        gate_ref[pl.ds(e, 1), :] = (res[2:3] + res[3:4]) + res[4:5]
        return carry

    lax.fori_loop(0, n_e, compact, 0)


def _route(aff_t, *, cap):
    batch, n_e, n = aff_t.shape
    kern = functools.partial(_route_kernel, cap=cap)
    return pl.pallas_call(
        kern,
        grid=(batch,),
        in_specs=[pl.BlockSpec((None, n_e, n), lambda b: (b, 0, 0))],
        out_specs=[pl.BlockSpec((None, n_e, cap), lambda b: (b, 0, 0)),
                   pl.BlockSpec((None, n_e, cap), lambda b: (b, 0, 0))],
        out_shape=[jax.ShapeDtypeStruct((batch, n_e, cap), I32),
                   jax.ShapeDtypeStruct((batch, n_e, cap), F32)],
        scratch_shapes=[pltpu.VMEM((n_e, n), I32)],
        compiler_params=_cparams(("arbitrary",)),
    )(aff_t)


DMA_UNROLL = 8


def _moe_kernel(idx_ref, gate_ref, h_hbm, w1_ref, w3_ref, w2_ref, g2_ref, facc_in, facc_hbm,
                stage_ref, acc_ref, gsem, rsem, wsem, *, rows, cap):
    del facc_in
    e = pl.program_id(0)
    f = pl.program_id(1)
    n_e = pl.num_programs(0)
    nf = pl.num_programs(1)
    slot = e % 2

    def issue_rows(make):
        def body(i, carry):
            for u in range(DMA_UNROLL):
                make(i, u).start()
            return carry
        lax.fori_loop(0, rows // DMA_UNROLL, body, 0)

    wt = stage_ref.shape[1] // rows
    assert acc_ref.shape[1] == DMA_UNROLL

    def x_row(ex, s):
        def make(i, u):
            k = i * DMA_UNROLL + u
            src = pl.multiple_of(idx_ref[ex * rows + k] * wt, wt)
            dst = pl.multiple_of(k * wt, wt)
            return pltpu.make_async_copy(h_hbm.at[pl.ds(src, wt), :], stage_ref.at[s, pl.ds(dst, wt), :], gsem.at[s])
        return make

    def acc_row_in(i, u):
        tok = idx_ref[e * rows + i * DMA_UNROLL + u]
        return pltpu.make_async_copy(facc_hbm.at[pl.ds(tok, 1), :], acc_ref.at[i, pl.ds(u, 1), :], rsem)

    def acc_row_out(i, u):
        tok = idx_ref[e * rows + i * DMA_UNROLL + u]
        return pltpu.make_async_copy(acc_ref.at[i, pl.ds(u, 1), :], facc_hbm.at[pl.ds(tok, 1), :], wsem)

    def wait_x(s):
        pltpu.make_async_copy(h_hbm.at[pl.ds(0, rows * wt), :], stage_ref.at[s], gsem.at[s]).wait()

    def wait_acc_in():
        pltpu.make_async_copy(acc_ref, acc_ref, rsem).wait()

    def wait_acc_out():
        pltpu.make_async_copy(acc_ref, acc_ref, wsem).wait()

    @pl.when(f == 0)
    def _():
        @pl.when(e > 0)
        def _():
            wait_acc_out()
        issue_rows(acc_row_in)

        @pl.when(e == 0)
        def _():
            issue_rows(x_row(0, 0))
        wait_x(slot)

        @pl.when(e + 1 < n_e)
        def _():
            issue_rows(x_row(e + 1, 1 - slot))
        wait_acc_in()

    words = [stage_ref[slot, pl.ds(s, rows, stride=wt), :] for s in range(wt)]
    half = wt * LANES
    x_lo = jnp.concatenate([pltpu.bitcast(u << 16, F32).astype(BF16) for u in words], axis=1)
    x_hi = jnp.concatenate([pltpu.bitcast(u & jnp.int32(-65536), F32).astype(BF16) for u in words],
                           axis=1)
    tf = w1_ref.shape[1]
    w13 = jnp.concatenate([w1_ref[...], w3_ref[...]], axis=1).astype(BF16)
    ag = (jnp.dot(x_lo, w13[:half], preferred_element_type=F32)
          + jnp.dot(x_hi, w13[half:], preferred_element_type=F32))
    hid = (_silu(ag[:, :tf]) * ag[:, tf:]).astype(BF16)
    gate = gate_ref[...]
    sub = acc_ref.shape[1]
    d = acc_ref.shape[2]
    tc = min(512, d)
    for j in range(d // tc):
        cols = slice(j * tc, (j + 1) * tc)
        y = gate * jnp.dot(hid, w2_ref[:, cols].astype(BF16), preferred_element_type=F32)
        for b in range(rows // cap):
            yb = y[b * cap:(b + 1) * cap] * g2_ref[b][:, cols]
            acc_ref[b * cap // sub:(b + 1) * cap // sub, :, cols] += yb.reshape(cap // sub, sub, tc)

    @pl.when(f == nf - 1)
    def _():
        issue_rows(acc_row_out)

        @pl.when(e == n_e - 1)
        def _():
            wait_acc_out()


def _moe(idx_flat, gate_col, h_packed, w1, w3, w2, mod3, facc0, *, rows, cap):
    m, d = facc0.shape
    n_e, _, dff = w1.shape
    tf = min(256, dff)
    kern = functools.partial(_moe_kernel, rows=rows, cap=cap)
    grid_spec = pltpu.PrefetchScalarGridSpec(
        num_scalar_prefetch=1,
        grid=(n_e, dff // tf),
        in_specs=[pl.BlockSpec((None, rows, 1), lambda e, f, idx: (e, 0, 0)),
                  pl.BlockSpec(memory_space=pl.ANY),
                  pl.BlockSpec((None, d, tf), lambda e, f, idx: (e, 0, f)),
                  pl.BlockSpec((None, d, tf), lambda e, f, idx: (e, 0, f)),
                  pl.BlockSpec((None, tf, d), lambda e, f, idx: (e, f, 0)),
                  pl.BlockSpec((ADA_ROWS, 1, d), lambda e, f, idx: (0, 0, 5)),
                  pl.BlockSpec(memory_space=pl.ANY)],
        out_specs=pl.BlockSpec(memory_space=pl.ANY),
        scratch_shapes=[pltpu.VMEM((2, rows * (d // 2 // LANES), LANES), I32),
                        pltpu.VMEM((rows // DMA_UNROLL, DMA_UNROLL, d), F32),
                        pltpu.SemaphoreType.DMA((2,)), pltpu.SemaphoreType.DMA(()), pltpu.SemaphoreType.DMA(())],
    )
    return pl.pallas_call(
        kern,
        grid_spec=grid_spec,
        out_shape=jax.ShapeDtypeStruct((m, d), F32),
        input_output_aliases={7: 0},
        compiler_params=_cparams(("arbitrary", "arbitrary")),
    )(idx_flat, gate_col, h_packed, w1, w3, w2, mod3, facc0)


def _final_kernel(y_ref, lg_ref, lb_ref, o_ref):
    o_ref[...] = _layer_norm(y_ref[...], lg_ref[...], lb_ref[...])


def _final(pre_ln, ln_g, ln_b, *, n_lat):
    m, d = pre_ln.shape
    tm = min(512, n_lat)
    vec = pl.BlockSpec((1, d), lambda i: (0, 0))
    return pl.pallas_call(
        _final_kernel,
        grid=(m // tm,),
        in_specs=[pl.BlockSpec((tm, d), lambda i: (i, 0)), vec, vec],
        out_specs=pl.BlockSpec((tm, d), lambda i: (i, 0)),
        out_shape=jax.ShapeDtypeStruct((m, d), F32),
        compiler_params=_cparams(("arbitrary",)),
    )(pre_ln, ln_g.reshape(1, d), ln_b.reshape(1, d))


def _rope_tables(n_lat):
    rows = n_lat // GRID_W
    row = jnp.repeat(jnp.arange(rows, dtype=F32), GRID_W)
    col = jnp.tile(jnp.arange(GRID_W, dtype=F32), rows)
    n_freq = DIFF_DH // 4
    inv = ROPE_BASE ** (-jnp.arange(n_freq, dtype=F32) / n_freq)
    ang = jnp.concatenate([row[:, None] * inv, col[:, None] * inv], axis=-1)
    cos, sin = jnp.cos(ang), jnp.sin(ang)
    reps = LANES // DIFF_DH
    cos_t = jnp.tile(jnp.concatenate([cos, cos], axis=-1), (1, reps))
    sin_t = jnp.tile(jnp.concatenate([-sin, sin], axis=-1), (1, reps))
    return cos_t, sin_t


def kernel(x, c, ctx, c_ctx, w_ada, b_ada, w_in, w_gate2, b_gate, gla_norm_g, diff_lambda, diff_norm_g, w_o,
           ln1_g, ln1_b, w_router, w1, w3, w2, ln2_g, ln2_b):
    batch, n_lat, d = x.shape
    n_ctx = ctx.shape[1]
    assert w_ada.shape[0] == DEPTH == 1 and batch < ADA_ROWS
    n_e = w_router.shape[-1]
    cap = EC_CAPACITY_FACTOR * n_lat // n_e
    l = 0

    c_rows = jnp.concatenate([c, c_ctx[None, :], jnp.zeros((ADA_ROWS - batch - 1, d), F32)], axis=0)
    mod3 = _ada(c_rows, w_ada[l], b_ada[l]).reshape(ADA_ROWS, 1, 6 * d)

    gqkvr = 2 * (d // 4) + 2 * (d // 2)
    r2 = 2 * GLA_RANK
    w_in_t = w_in[l].T.astype(BF16)
    w_diff = w_in_t[gqkvr + r2:]
    w_lr = w_in_t[gqkvr:gqkvr + r2]
    hk = w_gate2.shape[-1]
    zeros = jnp.zeros((GLA_RANK, hk), F32)
    w_g = jnp.concatenate([jnp.concatenate([w_gate2[l, 0], zeros], axis=1),
                           jnp.concatenate([zeros, w_gate2[l, 1]], axis=1)], axis=0).astype(BF16)
    b_g = b_gate[l].reshape(1, 2 * hk)
    cos_t, sin_t = _rope_tables(n_lat)

    tm_l = min(1024, n_lat)
    tpb = n_lat // tm_l
    x2d = x.reshape(batch * n_lat, d)
    pf_l, pb_l, lg_l = _proj(x2d, mod3, lambda i: i // tpb, w_in_t, w_diff, w_lr, w_g, b_g, cos_t, sin_t,
                             rope=True, tm=tm_l, n_tab_tiles=tpb)
    ctx2d = ctx.reshape(batch * n_ctx, d)
    tm_c = min(1024, batch * n_ctx)
    pf_c, pb_c, lg_c = _proj(ctx2d, mod3, lambda i: batch, w_in_t, w_diff, w_lr, w_g, b_g, cos_t, sin_t,
                             rope=False, tm=tm_c, n_tab_tiles=1)

    gla_o = _gla(pf_l, lg_l, pf_c, lg_c, gla_norm_g[l], batch=batch, n_lat=n_lat, n_ctx=n_ctx)
    diff_o = _diff(pb_l, pb_c, diff_lambda[l], diff_norm_g[l], batch=batch, n_lat=n_lat, n_ctx=n_ctx)

    res, h_packed, aff_t = _out(gla_o, diff_o, x2d, w_o[l].astype(BF16), mod3, ln1_g[l], ln1_b[l],
                                w_router[l].T, batch=batch, n_lat=n_lat)

    idx, gate = _route(aff_t, cap=cap)
    rows = batch * cap
    idx_flat = idx.transpose(1, 0, 2).reshape(n_e * rows)
    gate_col = gate.transpose(1, 0, 2).reshape(n_e, rows, 1)
    pre_ln = _moe(idx_flat, gate_col, h_packed, w1[l], w3[l], w2[l], mod3, res, rows=rows, cap=cap)

    out = _final(pre_ln, ln2_g[l], ln2_b[l], n_lat=n_lat)
    return out.reshape(batch, n_lat, d)
```

```python
import functools
import math

import jax
import jax.numpy as jnp
import numpy as np
from jax import lax
from jax.experimental import pallas as pl
from jax.experimental.pallas import tpu as pltpu

F32 = jnp.float32
BF16 = jnp.bfloat16
I32 = jnp.int32

GRID_W = 64
GLA_HEADS = 4
GLA_RANK = 16
GLA_TAU = 16.0
GLA_CHUNK = 64
DIFF_HEADS = 8
DIFF_DH = 64
ROPE_BASE = 10000.0
EC_CAPACITY_FACTOR = 2
LN_EPS = 1e-5
RMS_EPS = 1e-6
DEPTH = 1
ALPHA = (2.0 * DEPTH) ** 0.25
LAM_INIT = 0.8 - 0.6 * math.exp(-0.3 * 0)

LANES = 128
ADA_ROWS = 8
VMEM_LIMIT = 60 * 1024 * 1024

_NT = (((1,), (1,)), ((), ()))
_TN = (((0,), (0,)), ((), ()))


def _cparams(sem):
    return pltpu.CompilerParams(dimension_semantics=sem, vmem_limit_bytes=VMEM_LIMIT)


def _silu(a):
    return a * (1.0 / (1.0 + jnp.exp(-a)))


def _log_sigmoid(z):
    return jnp.minimum(z, 0.0) - jnp.log1p(jnp.exp(-jnp.abs(z)))


def _ada_kernel(c_ref, w_ref, b_ref, o_ref):
    s = _silu(c_ref[...]).astype(BF16)
    o_ref[...] = jnp.dot(s, w_ref[...].astype(BF16), preferred_element_type=F32) + b_ref[...]


def _ada(c_rows, w_ada, b_ada):
    d, n_out = w_ada.shape
    tn = min(1024, n_out)
    return pl.pallas_call(
        _ada_kernel,
        grid=(n_out // tn,),
        in_specs=[pl.BlockSpec((ADA_ROWS, d), lambda j: (0, 0)),
                  pl.BlockSpec((d, tn), lambda j: (0, j)),
                  pl.BlockSpec((1, tn), lambda j: (0, j))],
        out_specs=pl.BlockSpec((ADA_ROWS, tn), lambda j: (0, j)),
        out_shape=jax.ShapeDtypeStruct((ADA_ROWS, n_out), F32),
        compiler_params=_cparams(("arbitrary",)),
    )(c_rows, w_ada, b_ada.reshape(1, n_out))


def _proj_kernel(x_ref, sh_ref, sc_ref, wa_ref, wb_ref, wlr_ref, wg_ref, bg_ref, cos_ref, sin_ref,
                 of_ref, ob_ref, lg_ref, u_ref, *, rope, n_f32, n_q, n_rope):
    n = pl.program_id(1)

    @pl.when(n == 0)
    def _():
        u = (x_ref[...] * (1.0 + sc_ref[...]) + sh_ref[...]).astype(BF16)
        u_ref[...] = u
        lr = lax.dot_general(u, wlr_ref[...], _NT, preferred_element_type=F32)
        z = jnp.dot(lr.astype(BF16), wg_ref[...], preferred_element_type=F32) + bg_ref[...]
        lg_ref[...] = _log_sigmoid(z) * (1.0 / GLA_TAU)

    @pl.when(n < n_f32)
    def _():
        of_ref[...] = lax.dot_general(u_ref[...], wa_ref[...], _NT, preferred_element_type=F32)

    @pl.when(jnp.logical_and(n >= n_f32, n < n_f32 + n_rope))
    def _():
        acc = lax.dot_general(u_ref[...], wb_ref[...], _NT, preferred_element_type=F32)
        scale = jnp.where(n < n_f32 + n_q, DIFF_DH ** -0.5, 1.0).astype(F32)
        if rope:
            lane = lax.broadcasted_iota(I32, (acc.shape[0], LANES), 1)
            first = (lane % DIFF_DH) < (DIFF_DH // 2)
            cs = cos_ref[...]
            sn = sin_ref[...]
            for j in range(acc.shape[1] // LANES):
                a = acc[:, j * LANES:(j + 1) * LANES]
                partner = jnp.where(first, pltpu.roll(a, LANES - DIFF_DH // 2, 1), pltpu.roll(a, DIFF_DH // 2, 1))
                ob_ref[:, j * LANES:(j + 1) * LANES] = ((a * cs + partner * sn) * scale).astype(BF16)
        else:
            ob_ref[...] = (acc * scale).astype(BF16)

    @pl.when(n >= n_f32 + n_rope)
    def _():
        ob_ref[...] = lax.dot_general(u_ref[...], wb_ref[...], _NT, preferred_element_type=F32).astype(BF16)


def _proj(x2d, mod3, row_of_tile, w_in, w_diff, w_lr, w_g, b_g, cos_t, sin_t, *, rope, tm, n_tab_tiles):
    m, d = x2d.shape
    tn = 512
    n_f32 = w_diff.shape[0] // tn
    n_tiles = 2 * n_f32
    n_q = n_tiles // 6
    n_rope = n_tiles // 3
    r2 = w_lr.shape[0]
    ng = w_g.shape[1]
    kern = functools.partial(_proj_kernel, rope=rope, n_f32=n_f32, n_q=n_q, n_rope=n_rope)
    return pl.pallas_call(
        kern,
        grid=(m // tm, n_tiles),
        in_specs=[pl.BlockSpec((tm, d), lambda i, j: (i, 0)),
                  pl.BlockSpec((None, 1, d), lambda i, j: (row_of_tile(i), 0, 0)),
                  pl.BlockSpec((None, 1, d), lambda i, j: (row_of_tile(i), 0, 1)),
                  pl.BlockSpec((tn, d), lambda i, j: (jnp.minimum(j, n_f32 - 1), 0)),
                  pl.BlockSpec((tn, d), lambda i, j: (jnp.maximum(j - n_f32, 0), 0)),
                  pl.BlockSpec((r2, d), lambda i, j: (0, 0)),
                  pl.BlockSpec((r2, ng), lambda i, j: (0, 0)),
                  pl.BlockSpec((1, ng), lambda i, j: (0, 0)),
                  pl.BlockSpec((tm, LANES), lambda i, j: (i % n_tab_tiles, 0)),
                  pl.BlockSpec((tm, LANES), lambda i, j: (i % n_tab_tiles, 0))],
        out_specs=[pl.BlockSpec((tm, tn), lambda i, j: (i, jnp.minimum(j, n_f32 - 1))),
                   pl.BlockSpec((tm, tn), lambda i, j: (i, jnp.maximum(j - n_f32, 0))),
                   pl.BlockSpec((tm, ng), lambda i, j: (i, 0))],
        out_shape=[jax.ShapeDtypeStruct((m, n_f32 * tn), F32),
                   jax.ShapeDtypeStruct((m, (n_tiles - n_f32) * tn), BF16),
                   jax.ShapeDtypeStruct((m, ng), F32)],
        scratch_shapes=[pltpu.VMEM((tm, d), BF16)],
        compiler_params=_cparams(("arbitrary", "arbitrary")),
    )(x2d, mod3, mod3, w_in, w_diff, w_lr, w_g, b_g, cos_t, sin_t)


GLA_GROUP = 4
GLA_UNROLL = 2


def _split3(a):
    hi = a.astype(BF16)
    r1 = a - hi.astype(F32)
    mid = r1.astype(BF16)
    return hi, mid, (r1 - mid.astype(F32)).astype(BF16)


def _gla_groups(dirs, *, want_out, scale):
    c = GLA_CHUNK
    rows, dk = dirs[0]["k"].shape
    n_chunks = rows // c
    for d in dirs:
        d["parts"] = jnp.dot(d["sum_mat"], jnp.concatenate(_split3(d["g"]), axis=1),
                             preferred_element_type=F32)
    for d in dirs:
        parts = d["parts"]
        sums = (parts[:, :dk] + parts[:, dk:2 * dk]) + parts[:, 2 * dk:]
        bcum = sums[:rows]
        btot = sums[rows:]
        d["vb"] = d["v"].astype(BF16)
        d["k_out"] = (d["k"] * jnp.exp(btot - bcum)).astype(BF16)
        d["decay"] = jnp.exp(btot)
        if want_out:
            d["q_in"] = ((d["q"] * scale) * jnp.exp(bcum)).astype(BF16)
            d["k_in"] = (d["k"] * jnp.exp(-bcum)).astype(BF16)
    if want_out:
        for d in dirs:
            d["att"] = lax.dot_general(d["q_in"], d["k_in"], _NT, preferred_element_type=F32)
        for d in dirs:
            d["att"] = jnp.where(d["causal"], d["att"], 0.0).astype(BF16)
        for d in dirs:
            d["intra"] = jnp.dot(d["att"], d["vb"], preferred_element_type=F32)
    chunk = lambda ci: slice(ci * c, (ci + 1) * c)
    for d in dirs:
        d["upd"] = [lax.dot_general(d["vb"][chunk(ci)], d["k_out"][chunk(ci)], _TN, preferred_element_type=F32)
                    for ci in range(n_chunks)]
    states = {}
    for d in dirs:
        st = states.get(id(d["st_ref"]))
        if st is None:
            st = d["st_ref"][...]
        d["st_in"] = [None] * n_chunks
        for ci in (range(n_chunks) if d["forward"] else reversed(range(n_chunks))):
            d["st_in"][ci] = st
            st = st * d["decay"][ci * c:ci * c + 1] + d["upd"][ci]
        states[id(d["st_ref"])] = st
    for d in dirs:
        d["st_ref"][...] = states[id(d["st_ref"])]
    if not want_out:
        return [None for _ in dirs]
    outs = []
    for d in dirs:
        outs.append(jnp.concatenate(
            [d["intra"][chunk(ci)] + lax.dot_general(d["q_in"][chunk(ci)], d["st_in"][ci].astype(BF16), _NT,
                                                     preferred_element_type=F32)
             for ci in range(n_chunks)], axis=0))
    return outs


def _gla_kernel(q_ref, k_ref, v_ref, r_ref, gf_ref, gb_ref, kc_ref, vc_ref, gfc_ref, gbc_ref, ng_ref,
                o_ref, of_ref, obk_ref, sf_ref, sb_ref, *, n_lat, n_ctx, dk):
    c = GLA_CHUNK
    scale = dk ** -0.5
    sf_ref[...] = jnp.zeros_like(sf_ref)
    sb_ref[...] = jnp.zeros_like(sb_ref)

    def masks(rows):
        row = lax.broadcasted_iota(I32, (rows, rows), 0)
        col = lax.broadcasted_iota(I32, (rows, rows), 1)
        same = (row // c) == (col // c)
        lower = jnp.logical_and(same, row >= col)
        upper = jnp.logical_and(same, row <= col)
        total = same.astype(BF16)
        return (lower, jnp.concatenate([lower.astype(BF16), total], axis=0),
                upper, jnp.concatenate([upper.astype(BF16), total], axis=0))

    def scan(n, refs_f, refs_b, want_out):
        rows = min(GLA_GROUP * c, n)
        lower, sum_f, upper, sum_b = masks(rows)
        n_groups = n // rows
        unroll = GLA_UNROLL if n_groups % GLA_UNROLL == 0 else 1
        ld = lambda ref, r0: None if ref is None else ref[pl.ds(r0, rows), :]
        names = ("q", "k", "v", "g")

        def body(i, carry):
            items, stores = [], []
            for u in range(unroll):
                rf = pl.multiple_of((i * unroll + u) * rows, rows)
                rb = pl.multiple_of((n_groups - 1 - (i * unroll + u)) * rows, rows)
                items.append(dict(zip(names, (ld(r, rf) for r in refs_f)), st_ref=sf_ref, sum_mat=sum_f,
                                  causal=lower, forward=True))
                items.append(dict(zip(names, (ld(r, rb) for r in refs_b)), st_ref=sb_ref, sum_mat=sum_b,
                                  causal=upper, forward=False))
                stores += [(of_ref, rf), (obk_ref, rb)]
            outs = _gla_groups(items, want_out=want_out, scale=scale)
            if want_out:
                for (ref, r0), out in zip(stores, outs):
                    ref[pl.ds(r0, rows), :] = out
            return carry

        lax.fori_loop(0, n_groups // unroll, body, 0)

    scan(n_ctx, (None, kc_ref, vc_ref, gfc_ref), (None, kc_ref, vc_ref, gbc_ref), False)
    scan(n_lat, (q_ref, k_ref, v_ref, gf_ref), (q_ref, k_ref, v_ref, gb_ref), True)

    tb = min(512, n_lat)

    def fin_body(i, carry):
        r0 = pl.multiple_of(i * tb, tb)
        o = of_ref[pl.ds(r0, tb), :] + obk_ref[pl.ds(r0, tb), :]
        ms = jnp.mean(o * o, axis=-1, keepdims=True)
        o = o * lax.rsqrt(ms + RMS_EPS) * ng_ref[...]
        o_ref[pl.ds(r0, tb), :] = (o * _silu(r_ref[pl.ds(r0, tb), :])).astype(o_ref.dtype)
        return carry

    lax.fori_loop(0, n_lat // tb, fin_body, 0)


def _gla(pf_l, lg_l, pf_c, lg_c, norm_g, *, batch, n_lat, n_ctx):
    h = GLA_HEADS
    dk = pf_l.shape[1] // 6 // h
    dv = 2 * dk
    kern = functools.partial(_gla_kernel, n_lat=n_lat, n_ctx=n_ctx, dk=dk)
    lat = lambda width, off: pl.BlockSpec((n_lat, width), lambda b, i: (b, off + i))
    ctx = lambda width, off: pl.BlockSpec((n_ctx, width), lambda b, i: (b, off + i))
    return pl.pallas_call(
        kern,
        grid=(batch, h),
        in_specs=[lat(dk, 0), lat(dk, h), lat(dv, h), lat(dv, 2 * h),
                  lat(dk, 0), lat(dk, h),
                  ctx(dk, h), ctx(dv, h), ctx(dk, 0), ctx(dk, h),
                  pl.BlockSpec((1, dv), lambda b, i: (0, 0))],
        out_specs=pl.BlockSpec((n_lat, dv), lambda b, i: (b, i)),
        out_shape=jax.ShapeDtypeStruct((batch * n_lat, h * dv), BF16),
        scratch_shapes=[pltpu.VMEM((n_lat, dv), F32), pltpu.VMEM((n_lat, dv), F32),
                        pltpu.VMEM((dv, dk), F32), pltpu.VMEM((dv, dk), F32)],
        compiler_params=_cparams(("arbitrary", "arbitrary")),
    )(pf_l, pf_l, pf_l, pf_l, lg_l, lg_l, pf_c, pf_c, lg_c, lg_c, norm_g.reshape(1, dv))


DIFF_KEY_CHUNK = 512


def _diff_step(q_ref, kc_ref, kl_ref, vc_ref, vl_ref, lam_ref, ng_ref, o_ref, acc_ref, s_w, m_w, s_r, m_r):
    tq = o_ref.shape[0]
    dv = o_ref.shape[1]
    done = acc_ref[...]
    dl = lam_ref[...]
    lam = (jnp.exp(jnp.sum(dl[0:1] * dl[1:2], axis=-1, keepdims=True))
           - jnp.exp(jnp.sum(dl[2:3] * dl[3:4], axis=-1, keepdims=True)) + LAM_INIT)
    inv = 1.0 / done[:, dv:dv + 1]
    o = done[:tq, :dv] * inv[:tq] - done[tq:, :dv] * (inv[tq:] * lam)
    ms = jnp.mean(o * o, axis=-1, keepdims=True)
    o_ref[...] = (o * lax.rsqrt(ms + RMS_EPS) * ng_ref[...] * (1.0 - LAM_INIT)).astype(o_ref.dtype)

    q = q_ref[...]
    lane = lax.broadcasted_iota(I32, q.shape, 1)
    zero = jnp.zeros_like(q)
    qq = jnp.concatenate([jnp.where(lane < DIFF_DH, q, zero), jnp.where(lane >= DIFF_DH, q, zero)], axis=0)
    mx = jnp.max(m_r[...], axis=-1, keepdims=True)
    m_new = jnp.full((2 * tq, LANES), -jnp.inf, F32)
    acc = jnp.zeros((2 * tq, 2 * dv), F32)
    col = 0
    for k_ref, v_ref in ((kc_ref, vc_ref), (kl_ref, vl_ref)):
        n_keys = k_ref.shape[0]
        w = min(DIFF_KEY_CHUNK, n_keys)
        ones_col = (lax.broadcasted_iota(I32, (w, dv), 1) == 0).astype(BF16)
        for r0 in range(0, n_keys, w):
            s_new = lax.dot_general(qq, k_ref[r0:r0 + w, :], _NT, preferred_element_type=F32)
            s_w[:, col:col + w] = s_new
            for t in range(w // LANES):
                m_new = jnp.maximum(m_new, s_new[:, t * LANES:(t + 1) * LANES])
            e = jnp.exp(s_r[:, col:col + w] - mx).astype(BF16)
            v_ext = jnp.concatenate([v_ref[r0:r0 + w, :], ones_col], axis=1)
            acc = acc + jnp.dot(e, v_ext, preferred_element_type=F32)
            col += w
    m_w[...] = m_new
    acc_ref[...] = acc


def _diff_kernel(q_ref, kl_ref, kc_ref, vl_ref, vc_ref, lam_ref, ng_ref, o_ref, acc_ref,
                 sa_ref, sb_ref, ma_ref, mb_ref):
    j = pl.program_id(0)

    @pl.when(j == 0)
    def _():
        sb_ref[...] = jnp.zeros_like(sb_ref)
        mb_ref[...] = jnp.zeros_like(mb_ref)
        acc_ref[...] = jnp.ones_like(acc_ref)

    args = (q_ref, kc_ref, kl_ref, vc_ref, vl_ref, lam_ref, ng_ref, o_ref, acc_ref)

    @pl.when(j % 2 == 0)
    def _():
        _diff_step(*args, sa_ref, ma_ref, sb_ref, mb_ref)

    @pl.when(j % 2 == 1)
    def _():
        _diff_step(*args, sb_ref, mb_ref, sa_ref, ma_ref)


def _diff(pb_l, pb_c, diff_lambda, norm_g, *, batch, n_lat, n_ctx):
    h = DIFF_HEADS
    dv = 2 * DIFF_DH
    tq = min(512, n_lat)
    nq = n_lat // tq
    nblk = h * nq
    total = batch * nblk
    cur = lambda j: jnp.minimum(j, total - 1)
    prev = lambda j: jnp.clip(j - 1, 0, total - 1)
    fin = lambda j: jnp.maximum(j - 2, 0)
    sample = lambda g: g // nblk
    head = lambda g: (g % nblk) // nq
    qrow = lambda g: sample(g) * nq + g % nq
    return pl.pallas_call(
        _diff_kernel,
        grid=(total + 2,),
        in_specs=[pl.BlockSpec((tq, dv), lambda j: (qrow(cur(j)), head(cur(j)))),
                  pl.BlockSpec((n_lat, dv), lambda j: (sample(cur(j)), h + head(cur(j)))),
                  pl.BlockSpec((n_ctx, dv), lambda j: (sample(cur(j)), h + head(cur(j)))),
                  pl.BlockSpec((n_lat, dv), lambda j: (sample(prev(j)), 2 * h + head(prev(j)))),
                  pl.BlockSpec((n_ctx, dv), lambda j: (sample(prev(j)), 2 * h + head(prev(j)))),
                  pl.BlockSpec(diff_lambda.shape, lambda j: (0, 0)),
                  pl.BlockSpec((1, dv), lambda j: (0, 0))],
        out_specs=pl.BlockSpec((tq, dv), lambda j: (qrow(fin(j)), head(fin(j)))),
        out_shape=jax.ShapeDtypeStruct((batch * n_lat, h * dv), BF16),
        scratch_shapes=[pltpu.VMEM((2 * tq, 2 * dv), F32),
                        pltpu.VMEM((2 * tq, n_ctx + n_lat), F32), pltpu.VMEM((2 * tq, n_ctx + n_lat), F32),
                        pltpu.VMEM((2 * tq, LANES), F32), pltpu.VMEM((2 * tq, LANES), F32)],
        compiler_params=_cparams(("arbitrary",)),
    )(pb_l, pb_l, pb_c, pb_l, pb_c, diff_lambda, norm_g.reshape(1, dv))


def _layer_norm(y, g, b):
    mu = jnp.mean(y, axis=-1, keepdims=True)
    yc = y - mu
    var = jnp.mean(yc * yc, axis=-1, keepdims=True)
    return yc * lax.rsqrt(var + LN_EPS) * g + b


def _split2(a):
    hi = a.astype(BF16)
    return hi, (a - hi.astype(F32)).astype(BF16)


def _out_kernel(ga_ref, df_ref, x_ref, wo_ref, g1_ref, sh2_ref, sc2_ref, lg_ref, lb_ref, wr_ref,
                x1_ref, h_ref, aff_ref):
    half = ga_ref.shape[1]
    o = (jnp.dot(ga_ref[...], wo_ref[:half, :], preferred_element_type=F32)
         + jnp.dot(df_ref[...], wo_ref[half:, :], preferred_element_type=F32))
    x1 = _layer_norm(ALPHA * x_ref[...] + g1_ref[...] * o, lg_ref[...], lb_ref[...])
    x1_ref[...] = ALPHA * x1
    hmod = x1 * (1.0 + sc2_ref[...]) + sh2_ref[...]
    hbits = pltpu.bitcast(hmod.astype(BF16).astype(F32), I32)
    hw = hbits.shape[1] // 2
    packed = (hbits[:, hw:] & jnp.int32(-65536)) | lax.shift_right_logical(hbits[:, :hw], 16)
    tm = packed.shape[0]
    wt = hw // LANES
    for s in range(wt):
        h_ref[pl.ds(s, tm, stride=wt), :] = packed[:, s * LANES:(s + 1) * LANES]
    h_hi, h_lo = _split2(hmod)
    w_hi, w_lo = _split2(wr_ref[...])
    logits = (lax.dot_general(w_hi, h_hi, _NT, preferred_element_type=F32)
              + lax.dot_general(w_lo, h_hi, _NT, preferred_element_type=F32)
              + lax.dot_general(w_hi, h_lo, _NT, preferred_element_type=F32))
    e = jnp.exp(logits - jnp.max(logits, axis=0, keepdims=True))
    aff_ref[...] = e / jnp.sum(e, axis=0, keepdims=True)


def _out(gla_o, diff_o, x2d, w_o, mod3, ln_g, ln_b, w_router_t, *, batch, n_lat):
    m, d = x2d.shape
    half = gla_o.shape[1]
    n_e = w_router_t.shape[0]
    tm = min(512, n_lat)
    tpb = n_lat // tm
    modspec = lambda k: pl.BlockSpec((None, 1, d), lambda i: (i // tpb, 0, k))
    vec = pl.BlockSpec((1, d), lambda i: (0, 0))
    return pl.pallas_call(
        _out_kernel,
        grid=(m // tm,),
        in_specs=[pl.BlockSpec((tm, half), lambda i: (i, 0)),
                  pl.BlockSpec((tm, half), lambda i: (i, 0)),
                  pl.BlockSpec((tm, d), lambda i: (i, 0)),
                  pl.BlockSpec((2 * half, d), lambda i: (0, 0), pipeline_mode=pl.Buffered(1)),
                  modspec(2), modspec(3), modspec(4), vec, vec,
                  pl.BlockSpec((n_e, d), lambda i: (0, 0))],
        out_specs=[pl.BlockSpec((tm, d), lambda i: (i, 0)),
                   pl.BlockSpec((tm * (d // 2 // LANES), LANES), lambda i: (i, 0)),
                   pl.BlockSpec((None, n_e, tm), lambda i: (i // tpb, 0, i % tpb))],
        out_shape=[jax.ShapeDtypeStruct((m, d), F32),
                   jax.ShapeDtypeStruct((m * (d // 2 // LANES), LANES), I32),
                   jax.ShapeDtypeStruct((batch, n_e, n_lat), F32)],
        compiler_params=_cparams(("arbitrary",)),
    )(gla_o, diff_o, x2d, w_o, mod3, mod3, mod3, ln_g.reshape(1, d), ln_b.reshape(1, d), w_router_t)


def _prefix_count(mask_f32, strict_upper):
    rows, n = mask_f32.shape
    carry = jnp.zeros((rows, 1), F32)
    pieces = []
    for j in range(n // LANES):
        blk = mask_f32[:, j * LANES:(j + 1) * LANES]
        within = jnp.dot(blk.astype(BF16), strict_upper, preferred_element_type=F32)
        pieces.append(within + carry)
        carry = carry + jnp.sum(blk, axis=-1, keepdims=True)
    return jnp.concatenate(pieces, axis=-1)


def _route_kernel(aff_ref, idx_ref, gate_ref, pos_ref, *, cap):
    b = pl.program_id(0)
    v = aff_ref[...]
    n_e, n = v.shape
    bits = pltpu.bitcast(v, I32)

    def search(i, t):
        cand = t | (1 << (30 - i))
        cnt = jnp.sum((bits >= cand).astype(I32), axis=-1, keepdims=True)
        return jnp.where(cnt >= cap, cand, t)

    thr = lax.fori_loop(0, 31, search, jnp.zeros((n_e, 1), I32))
    gt = bits > thr
    eq = bits == thr
    need = (cap - jnp.sum(gt.astype(I32), axis=-1, keepdims=True)).astype(F32)
    r = lax.broadcasted_iota(I32, (LANES, LANES), 0)
    cidx = lax.broadcasted_iota(I32, (LANES, LANES), 1)
    strict_upper = (r < cidx).astype(BF16)
    eq_rank = _prefix_count(eq.astype(F32), strict_upper)
    sel = jnp.logical_or(gt, jnp.logical_and(eq, eq_rank < need))
    pos = _prefix_count(sel.astype(F32), strict_upper)
    pos_ref[...] = jnp.where(sel, pos, -1.0).astype(I32)

    tok = lax.broadcasted_iota(I32, (8, n), 1)
    srow = lax.broadcasted_iota(I32, (8, n), 0)
    tok_hi = (tok >> 6).astype(F32)
    tok_lo = (tok & 63).astype(F32)
    slot = lax.broadcasted_iota(I32, (cap, n), 0)

    def compact(e, carry):
        ve = aff_ref[pl.ds(e, 1), :]
        v_hi = ve.astype(BF16).astype(F32)
        r1 = ve - v_hi
        v_mid = r1.astype(BF16).astype(F32)
        v_lo = r1 - v_mid
        lhs = jnp.where(srow == 0, tok_hi,
              jnp.where(srow == 1, tok_lo,
              jnp.where(srow == 2, v_hi,
              jnp.where(srow == 3, v_mid,
              jnp.where(srow == 4, v_lo, 0.0))))).astype(BF16)
        onehot = (slot == pos_ref[pl.ds(e, 1), :]).astype(F32).astype(BF16)
        res = lax.dot_general(lhs, onehot, _NT, preferred_element_type=F32)
        idx_ref[pl.ds(e, 1), :] = (res[0:1] * 64.0 + res[1:2]).astype(I32) + b * n
        gate_ref[pl.ds(e, 1), :] = (res[2:3] + res[3:4]) + res[4:5]
        return carry

    lax.fori_loop(0, n_e, compact, 0)


def _route(aff_t, *, cap):
    batch, n_e, n = aff_t.shape
    kern = functools.partial(_route_kernel, cap=cap)
    return pl.pallas_call(
        kern,
        grid=(batch,),
        in_specs=[pl.BlockSpec((None, n_e, n), lambda b: (b, 0, 0))],
        out_specs=[pl.BlockSpec((None, n_e, cap), lambda b: (b, 0, 0)),
                   pl.BlockSpec((None, n_e, cap), lambda b: (b, 0, 0))],
        out_shape=[jax.ShapeDtypeStruct((batch, n_e, cap), I32),
                   jax.ShapeDtypeStruct((batch, n_e, cap), F32)],
        scratch_shapes=[pltpu.VMEM((n_e, n), I32)],
        compiler_params=_cparams(("arbitrary",)),
    )(aff_t)


DMA_UNROLL = 8


def _moe_kernel(idx_ref, gate_ref, h_hbm, w1_ref, w3_ref, w2_ref, g2_ref, facc_in, facc_hbm,
                stage_ref, acc_ref, gsem, rsem, wsem, *, rows, cap):
    del facc_in
    e = pl.program_id(0)
    f = pl.program_id(1)
    n_e = pl.num_programs(0)
    nf = pl.num_programs(1)
    slot = e % 2

    def issue_rows(make, both_priorities=False):
        def body(i, carry):
            for u in range(DMA_UNROLL):
                make(i, u).start(priority=u % 2 if both_priorities else 0)
            return carry
        lax.fori_loop(0, rows // DMA_UNROLL, body, 0)

    wt = stage_ref.shape[1] // rows
    assert acc_ref.shape[1] == DMA_UNROLL

    def x_row(ex, s):
        def make(i, u):
            k = i * DMA_UNROLL + u
            src = pl.multiple_of(idx_ref[ex * rows + k] * wt, wt)
            dst = pl.multiple_of(k * wt, wt)
            return pltpu.make_async_copy(h_hbm.at[pl.ds(src, wt), :], stage_ref.at[s, pl.ds(dst, wt), :], gsem.at[s])
        return make

    def acc_row_in(i, u):
        tok = idx_ref[e * rows + i * DMA_UNROLL + u]
        return pltpu.make_async_copy(facc_hbm.at[pl.ds(tok, 1), :], acc_ref.at[i, pl.ds(u, 1), :], rsem)

    def acc_row_out(i, u):
        tok = idx_ref[e * rows + i * DMA_UNROLL + u]
        return pltpu.make_async_copy(acc_ref.at[i, pl.ds(u, 1), :], facc_hbm.at[pl.ds(tok, 1), :], wsem)

    def wait_x(s):
        pltpu.make_async_copy(h_hbm.at[pl.ds(0, rows * wt), :], stage_ref.at[s], gsem.at[s]).wait()

    def wait_acc_in():
        pltpu.make_async_copy(acc_ref, acc_ref, rsem).wait()

    def wait_acc_out():
        pltpu.make_async_copy(acc_ref, acc_ref, wsem).wait()

    @pl.when(f == 0)
    def _():
        @pl.when(e > 0)
        def _():
            wait_acc_out()
        issue_rows(acc_row_in)

        @pl.when(e == 0)
        def _():
            issue_rows(x_row(0, 0))
        wait_x(slot)

        @pl.when(e + 1 < n_e)
        def _():
            issue_rows(x_row(e + 1, 1 - slot))
        wait_acc_in()

    words = [stage_ref[slot, pl.ds(s, rows, stride=wt), :] for s in range(wt)]
    half = wt * LANES
    x_lo = jnp.concatenate([pltpu.bitcast(u << 16, F32).astype(BF16) for u in words], axis=1)
    x_hi = jnp.concatenate([pltpu.bitcast(u & jnp.int32(-65536), F32).astype(BF16) for u in words],
                           axis=1)
    tf = w1_ref.shape[1]
    w13 = jnp.concatenate([w1_ref[...], w3_ref[...]], axis=1).astype(BF16)
    ag = (jnp.dot(x_lo, w13[:half], preferred_element_type=F32)
          + jnp.dot(x_hi, w13[half:], preferred_element_type=F32))
    hid = (_silu(ag[:, :tf]) * ag[:, tf:]).astype(BF16)
    gate = gate_ref[...]
    sub = acc_ref.shape[1]
    d = acc_ref.shape[2]
    tc = min(512, d)
    for j in range(d // tc):
        cols = slice(j * tc, (j + 1) * tc)
        y = gate * jnp.dot(hid, w2_ref[:, cols].astype(BF16), preferred_element_type=F32)
        for b in range(rows // cap):
            yb = y[b * cap:(b + 1) * cap] * g2_ref[b][:, cols]
            acc_ref[b * cap // sub:(b + 1) * cap // sub, :, cols] += yb.reshape(cap // sub, sub, tc)

    @pl.when(f == nf - 1)
    def _():
        issue_rows(acc_row_out, both_priorities=True)

        @pl.when(e == n_e - 1)
        def _():
            wait_acc_out()


def _moe(idx_flat, gate_col, h_packed, w1, w3, w2, mod3, facc0, *, rows, cap):
    m, d = facc0.shape
    n_e, _, dff = w1.shape
    tf = min(256, dff)
    kern = functools.partial(_moe_kernel, rows=rows, cap=cap)
    grid_spec = pltpu.PrefetchScalarGridSpec(
        num_scalar_prefetch=1,
        grid=(n_e, dff // tf),
        in_specs=[pl.BlockSpec((None, rows, 1), lambda e, f, idx: (e, 0, 0)),
                  pl.BlockSpec(memory_space=pl.ANY),
                  pl.BlockSpec((None, d, tf), lambda e, f, idx: (e, 0, f)),
                  pl.BlockSpec((None, d, tf), lambda e, f, idx: (e, 0, f)),
                  pl.BlockSpec((None, tf, d), lambda e, f, idx: (e, f, 0)),
                  pl.BlockSpec((ADA_ROWS, 1, d), lambda e, f, idx: (0, 0, 5)),
                  pl.BlockSpec(memory_space=pl.ANY)],
        out_specs=pl.BlockSpec(memory_space=pl.ANY),
        scratch_shapes=[pltpu.VMEM((2, rows * (d // 2 // LANES), LANES), I32),
                        pltpu.VMEM((rows // DMA_UNROLL, DMA_UNROLL, d), F32),
                        pltpu.SemaphoreType.DMA((2,)), pltpu.SemaphoreType.DMA(()), pltpu.SemaphoreType.DMA(())],
    )
    return pl.pallas_call(
        kern,
        grid_spec=grid_spec,
        out_shape=jax.ShapeDtypeStruct((m, d), F32),
        input_output_aliases={7: 0},
        compiler_params=_cparams(("arbitrary", "arbitrary")),
    )(idx_flat, gate_col, h_packed, w1, w3, w2, mod3, facc0)


def _final_kernel(y_ref, lg_ref, lb_ref, o_ref):
    o_ref[...] = _layer_norm(y_ref[...], lg_ref[...], lb_ref[...])


def _final(pre_ln, ln_g, ln_b, *, n_lat):
    m, d = pre_ln.shape
    tm = min(512, n_lat)
    vec = pl.BlockSpec((1, d), lambda i: (0, 0))
    return pl.pallas_call(
        _final_kernel,
        grid=(m // tm,),
        in_specs=[pl.BlockSpec((tm, d), lambda i: (i, 0)), vec, vec],
        out_specs=pl.BlockSpec((tm, d), lambda i: (i, 0)),
        out_shape=jax.ShapeDtypeStruct((m, d), F32),
        compiler_params=_cparams(("arbitrary",)),
    )(pre_ln, ln_g.reshape(1, d), ln_b.reshape(1, d))


def _rope_tables(n_lat):
    rows = n_lat // GRID_W
    row = jnp.repeat(jnp.arange(rows, dtype=F32), GRID_W)
    col = jnp.tile(jnp.arange(GRID_W, dtype=F32), rows)
    n_freq = DIFF_DH // 4
    inv = ROPE_BASE ** (-jnp.arange(n_freq, dtype=F32) / n_freq)
    ang = jnp.concatenate([row[:, None] * inv, col[:, None] * inv], axis=-1)
    cos, sin = jnp.cos(ang), jnp.sin(ang)
    reps = LANES // DIFF_DH
    cos_t = jnp.tile(jnp.concatenate([cos, cos], axis=-1), (1, reps))
    sin_t = jnp.tile(jnp.concatenate([-sin, sin], axis=-1), (1, reps))
    return cos_t, sin_t


def kernel(x, c, ctx, c_ctx, w_ada, b_ada, w_in, w_gate2, b_gate, gla_norm_g, diff_lambda, diff_norm_g, w_o,
           ln1_g, ln1_b, w_router, w1, w3, w2, ln2_g, ln2_b):
    batch, n_lat, d = x.shape
    n_ctx = ctx.shape[1]
    assert w_ada.shape[0] == DEPTH == 1 and batch < ADA_ROWS
    n_e = w_router.shape[-1]
    cap = EC_CAPACITY_FACTOR * n_lat // n_e
    l = 0

    c_rows = jnp.concatenate([c, c_ctx[None, :], jnp.zeros((ADA_ROWS - batch - 1, d), F32)], axis=0)
    mod3 = _ada(c_rows, w_ada[l], b_ada[l]).reshape(ADA_ROWS, 1, 6 * d)

    gqkvr = 2 * (d // 4) + 2 * (d // 2)
    r2 = 2 * GLA_RANK
    w_in_t = w_in[l].T.astype(BF16)
    w_diff = w_in_t[gqkvr + r2:]
    w_lr = w_in_t[gqkvr:gqkvr + r2]
    hk = w_gate2.shape[-1]
    zeros = jnp.zeros((GLA_RANK, hk), F32)
    w_g = jnp.concatenate([jnp.concatenate([w_gate2[l, 0], zeros], axis=1),
                           jnp.concatenate([zeros, w_gate2[l, 1]], axis=1)], axis=0).astype(BF16)
    b_g = b_gate[l].reshape(1, 2 * hk)
    cos_t, sin_t = _rope_tables(n_lat)

    tm_l = min(1024, n_lat)
    tpb = n_lat // tm_l
    x2d = x.reshape(batch * n_lat, d)
    pf_l, pb_l, lg_l = _proj(x2d, mod3, lambda i: i // tpb, w_in_t, w_diff, w_lr, w_g, b_g, cos_t, sin_t,
                             rope=True, tm=tm_l, n_tab_tiles=tpb)
    ctx2d = ctx.reshape(batch * n_ctx, d)
    tm_c = min(1024, batch * n_ctx)
    pf_c, pb_c, lg_c = _proj(ctx2d, mod3, lambda i: batch, w_in_t, w_diff, w_lr, w_g, b_g, cos_t, sin_t,
                             rope=False, tm=tm_c, n_tab_tiles=1)

    gla_o = _gla(pf_l, lg_l, pf_c, lg_c, gla_norm_g[l], batch=batch, n_lat=n_lat, n_ctx=n_ctx)
    diff_o = _diff(pb_l, pb_c, diff_lambda[l], diff_norm_g[l], batch=batch, n_lat=n_lat, n_ctx=n_ctx)

    res, h_packed, aff_t = _out(gla_o, diff_o, x2d, w_o[l].astype(BF16), mod3, ln1_g[l], ln1_b[l],
                                w_router[l].T, batch=batch, n_lat=n_lat)

    idx, gate = _route(aff_t, cap=cap)
    rows = batch * cap
    idx_flat = idx.transpose(1, 0, 2).reshape(n_e * rows)
    gate_col = gate.transpose(1, 0, 2).reshape(n_e, rows, 1)
    pre_ln = _moe(idx_flat, gate_col, h_packed, w1[l], w3[l], w2[l], mod3, res, rows=rows, cap=cap)

    out = _final(pre_ln, ln2_g[l], ln2_b[l], n_lat=n_lat)
    return out.reshape(batch, n_lat, d)
```
